```python
import functools
import jax, jax.numpy as jnp
from jax import lax
import numpy as np

D_MODEL = 2048
BATCH = 4
SEQ = 2048
DEPTH = 2
DEC_BATCH = 8
DEC_SEQ = 1
PAST_LEN = 16384
PAGE_SIZE = 128

MIX_WIDTH = D_MODEL
N_MEM = 256
MEM_HEADS = 4
MEM_HEAD_DIM = 128
MEM_WIDTH = MEM_HEADS * MEM_HEAD_DIM
TOK_WIDTH = MIX_WIDTH - MEM_WIDTH
FOX_HEAD_DIM = 128
FOX_HEADS = TOK_WIDTH // FOX_HEAD_DIM
RWKV_HEAD_DIM = 64
RWKV_HEADS = TOK_WIDTH // RWKV_HEAD_DIM
DECAY_LORA = 96
ICLR_LORA = 96
SHIFT_DIM = 3 * TOK_WIDTH + DECAY_LORA + ICLR_LORA
FOX_IN = 3 * TOK_WIDTH + FOX_HEADS + MEM_WIDTH + MIX_WIDTH
RWKV_IN = SHIFT_DIM + MEM_WIDTH + MIX_WIDTH
FOX_SPLITS = (TOK_WIDTH, 2 * TOK_WIDTH, 3 * TOK_WIDTH, 3 * TOK_WIDTH + FOX_HEADS, 3 * TOK_WIDTH + FOX_HEADS + MEM_WIDTH)
RWKV_SPLITS = (SHIFT_DIM, SHIFT_DIM + MEM_WIDTH)
SHIFT_SPLITS = (TOK_WIDTH, 2 * TOK_WIDTH, 3 * TOK_WIDTH, 3 * TOK_WIDTH + DECAY_LORA)
N_FOX_LAYERS = (DEPTH + 1) // 2
N_RWKV_LAYERS = DEPTH // 2
Q_BLOCK = 128
RMS_EPS = 1e-6
GN_EPS = 64e-5
NEG_INF = -1e30
CACHE_FORGET_BIAS = 9.0

kernel_name = 'fox_rwkv7_memxattn_decode_step'


def rmsnorm(x, g):
    xf = x.astype(jnp.float32)
    y = xf * lax.rsqrt(jnp.mean(xf * xf, axis=-1, keepdims=True) + RMS_EPS)
    return (y * g.astype(jnp.float32)).astype(x.dtype)


def memory_kv(mem, g_mem, w_mem):
    b = mem.shape[0]
    k, v = jnp.split(rmsnorm(mem, g_mem) @ w_mem, 2, axis=-1)
    shape = (b, N_MEM, MEM_HEADS, MEM_HEAD_DIM)
    return k.reshape(shape), v.reshape(shape)


def memory_attend(q_mem, mem_k, mem_v):
    b, t, _ = q_mem.shape
    q = q_mem.reshape(b, t, MEM_HEADS, MEM_HEAD_DIM)
    s = jnp.einsum('bthd,bmhd->bhtm', q, mem_k).astype(jnp.float32) * (MEM_HEAD_DIM ** -0.5)
    p = jax.nn.softmax(s, axis=-1).astype(mem_v.dtype)
    return jnp.einsum('bhtm,bmhd->bthd', p, mem_v).reshape(b, t, MEM_WIDTH)


def gated_output(o_tok, o_mem, gate, w_o):
    o = jnp.concatenate([o_tok.astype(gate.dtype), o_mem.astype(gate.dtype)], axis=-1)
    return (o * jax.nn.silu(gate)) @ w_o


def suffix_logf(logf):
    return lax.cumsum(logf, axis=1, reverse=True) - logf


def fox_attend(q, k, v, d_q, d_k, q_pos, k_pos):
    s = jnp.einsum('bqhd,bkhd->bhqk', q, k).astype(jnp.float32) * (FOX_HEAD_DIM ** -0.5)
    s = s + jnp.transpose(d_k, (0, 2, 1))[:, :, None, :] - jnp.transpose(d_q, (0, 2, 1))[:, :, :, None]
    causal = k_pos[None, :] <= q_pos[:, None]
    s = jnp.where(causal[None, None], s, NEG_INF)
    p = jax.nn.softmax(s, axis=-1)
    return jnp.einsum('bhqk,bkhd->bqhd', p.astype(v.dtype), v)


def fox_prompt(q, k, v, logf):
    b, t = q.shape[:2]
    d = suffix_logf(logf)
    k_pos = jnp.arange(t)

    def one_block(blk):
        start = blk * Q_BLOCK
        qb = lax.dynamic_slice_in_dim(q, start, Q_BLOCK, axis=1)
        db = lax.dynamic_slice_in_dim(d, start, Q_BLOCK, axis=1)
        return fox_attend(qb, k, v, db, d, start + jnp.arange(Q_BLOCK), k_pos)

    o = lax.map(one_block, jnp.arange(t // Q_BLOCK))
    return jnp.moveaxis(o, 0, 1).reshape(b, t, TOK_WIDTH)


def fox_sample(q, k, v, logf, k_pool, v_pool, logf_pool, page_table):
    db, ds = q.shape[:2]
    k_past = k_pool[page_table].reshape(db, -1, FOX_HEADS, FOX_HEAD_DIM)
    v_past = v_pool[page_table].reshape(db, -1, FOX_HEADS, FOX_HEAD_DIM)
    f_past = logf_pool[page_table].reshape(db, -1, FOX_HEADS).astype(jnp.float32)
    past = k_past.shape[1]
    k_all = jnp.concatenate([k_past, k], axis=1)
    v_all = jnp.concatenate([v_past, v], axis=1)
    d = suffix_logf(jnp.concatenate([f_past, logf], axis=1))
    o = fox_attend(q, k_all, v_all, d[:, past:], d, past + jnp.arange(ds), jnp.arange(past + ds))
    return o.reshape(db, ds, TOK_WIDTH)


def fox_layer(x, g_pre, g_post, w_in, b_f, w_o, mem_k, mem_v, attend):
    b, t, _ = x.shape
    h = rmsnorm(x, g_pre)
    q, k, v, f, q_mem, gate = jnp.split(h @ w_in, FOX_SPLITS, axis=-1)
    hs = (b, t, FOX_HEADS, FOX_HEAD_DIM)
    q, k, v = q.reshape(hs), k.reshape(hs), v.reshape(hs)
    logf = jax.nn.log_sigmoid((f + b_f).astype(jnp.float32))
    o_tok = attend(q, k, v, logf)
    o_mem = memory_attend(q_mem, mem_k, mem_v)
    y = gated_output(o_tok, o_mem, gate, w_o)
    return x + rmsnorm(y, g_post), (k, v, logf)


def wkv_recurrence(r, decay, k, v, kk, a, wkv0):
    def step(S, inp):
        r_t, w_t, k_t, v_t, kk_t, a_t = inp
        sa = jnp.einsum('bhvk,bhk->bhv', S, -kk_t)
        S = S * w_t[:, :, None, :] + sa[..., None] * (kk_t * a_t)[:, :, None, :] + v_t[..., None] * k_t[:, :, None, :]
        return S, jnp.einsum('bhvk,bhk->bhv', S, r_t)

    xs = tuple(jnp.moveaxis(z, 1, 0) for z in (r, decay, k, v, kk, a))
    S, ys = lax.scan(step, wkv0, xs)
    return jnp.moveaxis(ys, 0, 1), S


def rwkv_layer(x, g_pre, g_post, w_in, mu, w0, w_w2, a0, w_a2, k_k, k_a, r_k, ln_w, ln_b,
               w_o, mem_k, mem_v, shift_prev, wkv0):
    f32 = jnp.float32
    b, t, _ = x.shape
    h = rmsnorm(x, g_pre)
    p_shift, q_mem, gate = jnp.split(h @ w_in, RWKV_SPLITS, axis=-1)
    prev = jnp.concatenate([shift_prev.astype(p_shift.dtype), p_shift[:, :-1]], axis=1)
    s = p_shift + (prev - p_shift) * mu
    r, k, v, wl, al = jnp.split(s, SHIFT_SPLITS, axis=-1)
    w = -jax.nn.softplus(-(w0 + jnp.tanh(wl) @ w_w2).astype(f32)) - 0.5
    decay = jnp.exp(-jnp.exp(w))
    a = jax.nn.sigmoid((a0 + al @ w_a2).astype(f32))
    kf = k.astype(f32)
    hs = (b, t, RWKV_HEADS, RWKV_HEAD_DIM)
    kk = (kf * k_k).reshape(hs)
    kk = kk / jnp.maximum(jnp.sqrt(jnp.sum(kk * kk, axis=-1, keepdims=True)), 1e-12)
    k_mod = (kf * (1.0 + (a - 1.0) * k_a)).reshape(hs)
    r_h = r.astype(f32).reshape(hs)
    v_h = v.astype(f32).reshape(hs)
    y, wkv = wkv_recurrence(r_h, decay.reshape(hs), k_mod, v_h, kk, a.reshape(hs), wkv0.astype(f32))
    mean = jnp.mean(y, axis=-1, keepdims=True)
    var = jnp.mean(jnp.square(y - mean), axis=-1, keepdims=True)
    yn = ((y - mean) * lax.rsqrt(var + GN_EPS)).reshape(b, t, TOK_WIDTH) * ln_w + ln_b
    bonus = jnp.sum(r_h * k_mod * r_k.astype(f32).reshape(RWKV_HEADS, RWKV_HEAD_DIM), axis=-1, keepdims=True) * v_h
    o_tok = yn + bonus.reshape(b, t, TOK_WIDTH)
    o_mem = memory_attend(q_mem, mem_k, mem_v)
    out = gated_output(o_tok, o_mem, gate, w_o)
    return x + rmsnorm(out, g_post), (wkv, p_shift[:, -1:])


def setup_inputs(seed: int = 0) -> dict:
    key = jax.random.key(seed)
    ks = jax.random.split(key, 32)
    f32 = jnp.float32

    def nrm(k, shape, scale=1.0):
        return jax.random.normal(k, shape, f32) * scale

    n_pages = PAST_LEN // PAGE_SIZE
    n_used = DEC_BATCH * n_pages
    n_pool = n_used + n_used // 4
    page_table = jax.random.permutation(ks[0], n_pool)[:n_used].reshape(DEC_BATCH, n_pages).astype(jnp.int32)
    return {
        'x_prompt': nrm(ks[1], (BATCH, SEQ, D_MODEL)),
        'x_sample': nrm(ks[2], (DEC_BATCH, DEC_SEQ, D_MODEL)),
        'cache_fox_k': nrm(ks[3], (N_FOX_LAYERS, n_pool, PAGE_SIZE, FOX_HEADS, FOX_HEAD_DIM)),
        'cache_fox_v': nrm(ks[4], (N_FOX_LAYERS, n_pool, PAGE_SIZE, FOX_HEADS, FOX_HEAD_DIM)),
        'cache_fox_logf': jax.nn.log_sigmoid(nrm(ks[5], (N_FOX_LAYERS, n_pool, PAGE_SIZE, FOX_HEADS)) + CACHE_FORGET_BIAS),
        'cache_mem_k': nrm(ks[6], (DEPTH, DEC_BATCH, N_MEM, MEM_HEADS, MEM_HEAD_DIM)),
        'cache_mem_v': nrm(ks[7], (DEPTH, DEC_BATCH, N_MEM, MEM_HEADS, MEM_HEAD_DIM)),
        'state_rwkv_wkv': nrm(ks[8], (N_RWKV_LAYERS, DEC_BATCH, RWKV_HEADS, RWKV_HEAD_DIM, RWKV_HEAD_DIM)),
        'state_rwkv_shift': nrm(ks[9], (N_RWKV_LAYERS, DEC_BATCH, 1, SHIFT_DIM)),
        'page_table': page_table,
        'mem_prompt': nrm(ks[10], (BATCH, N_MEM, D_MODEL)),
        'g_pre': 1.0 + nrm(ks[11], (DEPTH, D_MODEL), 0.05),
        'g_post': 1.0 + nrm(ks[12], (DEPTH, D_MODEL), 0.05),
        'g_mem': 1.0 + nrm(ks[13], (DEPTH, D_MODEL), 0.05),
        'w_mem': nrm(ks[14], (DEPTH, D_MODEL, 2 * MEM_WIDTH), D_MODEL ** -0.5),
        'w_o': nrm(ks[15], (DEPTH, MIX_WIDTH, D_MODEL), MIX_WIDTH ** -0.5),
        'fox_w_in': nrm(ks[16], (N_FOX_LAYERS, D_MODEL, FOX_IN), D_MODEL ** -0.5),
        'fox_b_f': jax.random.uniform(ks[17], (N_FOX_LAYERS, FOX_HEADS), f32, 1.0, 4.0),
        'rwkv_w_in': nrm(ks[18], (N_RWKV_LAYERS, D_MODEL, RWKV_IN), D_MODEL ** -0.5),
        'rwkv_mu': jax.random.uniform(ks[19], (N_RWKV_LAYERS, SHIFT_DIM), f32),
        'rwkv_w0': jax.random.uniform(ks[20], (N_RWKV_LAYERS, TOK_WIDTH), f32, -6.0, -1.0),
        'rwkv_w_w2': nrm(ks[21], (N_RWKV_LAYERS, DECAY_LORA, TOK_WIDTH), 0.5 * DECAY_LORA ** -0.5),
        'rwkv_a0': nrm(ks[22], (N_RWKV_LAYERS, TOK_WIDTH), 0.5),
        'rwkv_w_a2': nrm(ks[23], (N_RWKV_LAYERS, ICLR_LORA, TOK_WIDTH), 0.5 * ICLR_LORA ** -0.5),
        'rwkv_k_k': 0.85 + nrm(ks[24], (N_RWKV_LAYERS, TOK_WIDTH), 0.05),
        'rwkv_k_a': 1.0 + nrm(ks[25], (N_RWKV_LAYERS, TOK_WIDTH), 0.05),
        'rwkv_r_k': nrm(ks[26], (N_RWKV_LAYERS, TOK_WIDTH), 0.1),
        'rwkv_ln_w': 1.0 + nrm(ks[27], (N_RWKV_LAYERS, TOK_WIDTH), 0.05),
        'rwkv_ln_b': nrm(ks[28], (N_RWKV_LAYERS, TOK_WIDTH), 0.01),
    }


def reference(x_prompt, x_sample, cache_fox_k, cache_fox_v, cache_fox_logf, cache_mem_k, cache_mem_v,
              state_rwkv_wkv, state_rwkv_shift, page_table, mem_prompt,
              g_pre, g_post, g_mem, w_mem, w_o, fox_w_in, fox_b_f,
              rwkv_w_in, rwkv_mu, rwkv_w0, rwkv_w_w2, rwkv_a0, rwkv_w_a2,
              rwkv_k_k, rwkv_k_a, rwkv_r_k, rwkv_ln_w, rwkv_ln_b):
    xp, xs = x_prompt, x_sample
    b_prompt = x_prompt.shape[0]
    mem_k_new, mem_v_new = [], []
    fox_k_p, fox_v_p, fox_f_p, fox_k_s, fox_v_s, fox_f_s = [], [], [], [], [], []
    wkv_p, shift_p, wkv_s, shift_s = [], [], [], []
    for i in range(DEPTH):
        j = i // 2
        mk_p, mv_p = memory_kv(mem_prompt, g_mem[i], w_mem[i])
        mem_k_new.append(mk_p)
        mem_v_new.append(mv_p)
        if i % 2 == 0:
            attend_s = functools.partial(fox_sample, k_pool=cache_fox_k[j], v_pool=cache_fox_v[j],
                                         logf_pool=cache_fox_logf[j], page_table=page_table)
            xp, (k_p, v_p, f_p) = fox_layer(xp, g_pre[i], g_post[i], fox_w_in[j], fox_b_f[j], w_o[i],
                                            mk_p, mv_p, fox_prompt)
            xs, (k_s, v_s, f_s) = fox_layer(xs, g_pre[i], g_post[i], fox_w_in[j], fox_b_f[j], w_o[i],
                                            cache_mem_k[i], cache_mem_v[i], attend_s)
            fox_k_p.append(k_p)
            fox_v_p.append(v_p)
            fox_f_p.append(f_p)
            fox_k_s.append(k_s)
            fox_v_s.append(v_s)
            fox_f_s.append(f_s)
        else:
            shift0 = jnp.zeros((b_prompt, 1, SHIFT_DIM), xp.dtype)
            wkv0 = jnp.zeros((b_prompt, RWKV_HEADS, RWKV_HEAD_DIM, RWKV_HEAD_DIM), jnp.float32)
            xp, (st_p, sh_p) = rwkv_layer(xp, g_pre[i], g_post[i], rwkv_w_in[j], rwkv_mu[j], rwkv_w0[j],
                                          rwkv_w_w2[j], rwkv_a0[j], rwkv_w_a2[j], rwkv_k_k[j], rwkv_k_a[j],
                                          rwkv_r_k[j], rwkv_ln_w[j], rwkv_ln_b[j], w_o[i], mk_p, mv_p,
                                          shift0, wkv0)
            xs, (st_s, sh_s) = rwkv_layer(xs, g_pre[i], g_post[i], rwkv_w_in[j], rwkv_mu[j], rwkv_w0[j],
                                          rwkv_w_w2[j], rwkv_a0[j], rwkv_w_a2[j], rwkv_k_k[j], rwkv_k_a[j],
                                          rwkv_r_k[j], rwkv_ln_w[j], rwkv_ln_b[j], w_o[i],
                                          cache_mem_k[i], cache_mem_v[i], state_rwkv_shift[j], state_rwkv_wkv[j])
            wkv_p.append(st_p)
            shift_p.append(sh_p)
            wkv_s.append(st_s)
            shift_s.append(sh_s)
    return (xp, xs,
            jnp.stack(mem_k_new), jnp.stack(mem_v_new),
            jnp.stack(fox_k_p), jnp.stack(fox_v_p), jnp.stack(fox_f_p),
            jnp.stack(fox_k_s), jnp.stack(fox_v_s), jnp.stack(fox_f_s),
            jnp.stack(wkv_p), jnp.stack(shift_p),
            jnp.stack(wkv_s), jnp.stack(shift_s))
```

```python
import functools
import math

import jax
import jax.numpy as jnp
from jax import lax
from jax.experimental import pallas as pl
from jax.experimental.pallas import tpu as pltpu

F32 = jnp.float32
BF16 = jnp.bfloat16
HIGHEST = lax.Precision.HIGHEST

D_MODEL = 2048
N_MEM = 256
MEM_HEADS = 4
MEM_HEAD_DIM = 128
MEM_WIDTH = MEM_HEADS * MEM_HEAD_DIM
TOK_WIDTH = D_MODEL - MEM_WIDTH
FOX_HEAD_DIM = 128
FOX_HEADS = TOK_WIDTH // FOX_HEAD_DIM
RWKV_HEAD_DIM = 64
RWKV_HEADS = TOK_WIDTH // RWKV_HEAD_DIM
LORA = 96
PAGE_SIZE = 128
RMS_EPS = 1e-6
GN_EPS = 64e-5

LANES = 128
SUBLANES = 8
V7X_VMEM_BYTES = 64 * 1024 * 1024
VMEM_LIMIT_BYTES = V7X_VMEM_BYTES - 8 * 1024 * 1024

GATE_OFF = 0
TOKA_OFF = D_MODEL
TOKB_OFF = TOKA_OFF + TOK_WIDTH
TOKC_OFF = TOKB_OFF + TOK_WIDTH
QMEM_OFF = TOKC_OFF + TOK_WIDTH
MAIN_WIDTH = QMEM_OFF + MEM_WIDTH
FOX_EXTRA = LANES
RWKV_EXTRA = 2 * LANES

HEAD_PAD = 16
CHUNK = 64


def _params(*sem):
    return pltpu.CompilerParams(dimension_semantics=sem, vmem_limit_bytes=VMEM_LIMIT_BYTES)


def _rms_scale(x):
    return lax.rsqrt(jnp.mean(x * x, axis=-1, keepdims=True) + RMS_EPS)


def _sigmoid(x):
    return 1.0 / (1.0 + jnp.exp(-x))


def _softplus(x):
    return jnp.maximum(x, 0.0) + jnp.log(1.0 + jnp.exp(-jnp.abs(x)))


def _norm_proj_kernel(x_ref, g_ref, w_ref, *rest, has_extra):
    if has_extra:
        we_ref, o_ref, oe_ref, h_ref = rest
    else:
        o_ref, h_ref = rest

    @pl.when(pl.program_id(1) == 0)
    def _():
        x = x_ref[...]
        h = (x * _rms_scale(x) * g_ref[...]).astype(BF16)
        h_ref[...] = h
        if has_extra:
            oe_ref[...] = jnp.dot(h, we_ref[...], preferred_element_type=F32)

    o_ref[...] = jnp.dot(h_ref[...], w_ref[...], preferred_element_type=F32)


def norm_proj(x, g, w, w_extra=None, *, tm, tn):
    m, d = x.shape
    n = w.shape[1]
    assert m % tm == 0 and n % tn == 0
    has_extra = w_extra is not None
    in_specs = [
        pl.BlockSpec((tm, d), lambda i, j: (i, 0)),
        pl.BlockSpec((1, d), lambda i, j: (0, 0)),
        pl.BlockSpec((d, tn), lambda i, j: (0, j)),
    ]
    out_specs = [pl.BlockSpec((tm, tn), lambda i, j: (i, j))]
    out_shape = [jax.ShapeDtypeStruct((m, n), F32)]
    args = [x, g.reshape(1, d), w]
    if has_extra:
        e = w_extra.shape[1]
        in_specs.append(pl.BlockSpec((d, e), lambda i, j: (0, 0)))
        out_specs.append(pl.BlockSpec((tm, e), lambda i, j: (i, 0)))
        out_shape.append(jax.ShapeDtypeStruct((m, e), F32))
        args.append(w_extra)
    outs = pl.pallas_call(
        functools.partial(_norm_proj_kernel, has_extra=has_extra),
        grid=(m // tm, n // tn),
        in_specs=in_specs,
        out_specs=out_specs,
        out_shape=out_shape,
        scratch_shapes=[pltpu.VMEM((tm, d), BF16)],
        compiler_params=_params("parallel", "arbitrary"),
        name="norm_proj",
    )(*args)
    return outs if has_extra else outs[0]


def _fox_gate_kernel(f_ref, b_ref, logf_ref, ccol_ref, crow_ref, carry_ref, *, tb):
    @pl.when(pl.program_id(1) == 0)
    def _():
        carry_ref[...] = jnp.zeros_like(carry_ref)

    logf = -_softplus(-(f_ref[...] + b_ref[...]))
    logf_ref[...] = logf
    row = lax.broadcasted_iota(jnp.int32, (tb, tb), 0)
    col = lax.broadcasted_iota(jnp.int32, (tb, tb), 1)
    tril = jnp.where(col <= row, 1.0, 0.0).astype(F32)
    c = jnp.dot(tril, logf, precision=HIGHEST, preferred_element_type=F32) + carry_ref[...]
    ccol_ref[...] = c
    crow_ref[0] = c.T[:HEAD_PAD]
    carry_ref[...] = c[tb - 1:tb, :]


def fox_gate(f_ext, b_pad, *, batch, seq, tb):
    nt = seq // tb
    row_spec = pl.BlockSpec((tb, LANES), lambda b, t: (b * nt + t, 0))
    return pl.pallas_call(
        functools.partial(_fox_gate_kernel, tb=tb),
        grid=(batch, nt),
        in_specs=[row_spec, pl.BlockSpec((1, LANES), lambda b, t: (0, 0))],
        out_specs=[row_spec, row_spec, pl.BlockSpec((1, HEAD_PAD, tb), lambda b, t: (b, 0, t))],
        out_shape=[
            jax.ShapeDtypeStruct((batch * seq, LANES), F32),
            jax.ShapeDtypeStruct((batch * seq, LANES), F32),
            jax.ShapeDtypeStruct((batch, HEAD_PAD, seq), F32),
        ],
        scratch_shapes=[pltpu.VMEM((1, LANES), F32)],
        compiler_params=_params("parallel", "arbitrary"),
        name="fox_gate",
    )(f_ext, b_pad)


_MASKED = -1e30


def _fox_flash_kernel(q_ref, k_ref, v_ref, ccol_ref, crow_ref, o_ref, kb_ref, vb_ref, *, tq):
    h = pl.program_id(1)
    qi = pl.program_id(2)

    @pl.when(qi == 0)
    def _():
        kb_ref[...] = k_ref[...].astype(BF16)
        vb_ref[...] = v_ref[...].astype(BF16)

    q = q_ref[...].astype(BF16)
    lane = lax.broadcasted_iota(jnp.int32, (tq, LANES), 1)
    cq = jnp.sum(jnp.where(lane == h, ccol_ref[...], 0.0), axis=-1, keepdims=True)
    scale = FOX_HEAD_DIM ** -0.5

    def step(c, carry, diagonal):
        m, l, acc = carry
        start = pl.multiple_of(c * tq, tq)
        kc = kb_ref[pl.ds(start, tq), :]
        vc = vb_ref[pl.ds(start, tq), :]
        ck = crow_ref[0, 0, :, pl.ds(start, tq)]
        s = lax.dot_general(q, kc, (((1,), (1,)), ((), ())), preferred_element_type=F32)
        s = s * scale + (cq - ck)
        if diagonal:
            row = lax.broadcasted_iota(jnp.int32, (tq, tq), 0)
            col = lax.broadcasted_iota(jnp.int32, (tq, tq), 1)
            s = jnp.where(col <= row, s, _MASKED)
        m_new = jnp.maximum(m, jnp.max(s, axis=-1, keepdims=True))
        p = jnp.exp(s - m_new)
        alpha = jnp.exp(m - m_new)
        l = alpha * l + jnp.sum(p, axis=-1, keepdims=True)
        acc = alpha * acc + jnp.dot(p.astype(BF16), vc, preferred_element_type=F32)
        return m_new, l, acc

    init = (jnp.full((tq, 1), _MASKED, F32), jnp.zeros((tq, 1), F32), jnp.zeros((tq, FOX_HEAD_DIM), F32))
    carry = lax.fori_loop(0, qi, lambda c, cr: step(c, cr, False), init)
    _, l, acc = step(qi, carry, True)
    o_ref[...] = acc / l


def fox_flash(main, ccol, crow, *, batch, seq, tq):
    nq = seq // tq
    qb, kb, vb = TOKA_OFF // LANES, TOKB_OFF // LANES, TOKC_OFF // LANES
    crow4 = crow.reshape(batch, HEAD_PAD, 1, seq)
    return pl.pallas_call(
        functools.partial(_fox_flash_kernel, tq=tq),
        grid=(batch, FOX_HEADS, nq),
        in_specs=[
            pl.BlockSpec((tq, LANES), lambda b, h, i: (b * nq + i, qb + h)),
            pl.BlockSpec((seq, LANES), lambda b, h, i: (b, kb + h)),
            pl.BlockSpec((seq, LANES), lambda b, h, i: (b, vb + h)),
            pl.BlockSpec((tq, LANES), lambda b, h, i: (b * nq + i, 0)),
            pl.BlockSpec((1, 1, 1, seq), lambda b, h, i: (b, h, 0, 0)),
        ],
        out_specs=pl.BlockSpec((tq, LANES), lambda b, h, i: (b * nq + i, h)),
        out_shape=jax.ShapeDtypeStruct((batch * seq, TOK_WIDTH), F32),
        scratch_shapes=[pltpu.VMEM((seq, LANES), BF16), pltpu.VMEM((seq, LANES), BF16)],
        compiler_params=_params("parallel", "parallel", "arbitrary"),
        name="fox_flash",
    )(main, main, main, ccol, crow4)


def _head_mask():
    row = lax.broadcasted_iota(jnp.int32, (HEAD_PAD, TOK_WIDTH), 0)
    col = lax.broadcasted_iota(jnp.int32, (HEAD_PAD, TOK_WIDTH), 1)
    lo = row * FOX_HEAD_DIM
    return (col >= lo) & (col < lo + FOX_HEAD_DIM)


def _fox_decode_kernel(pt_ref, q_ref, kn_ref, vn_ref, f_ref, bf_ref, kp_ref, vp_ref, lf_ref,
                       o_ref, lfn_ref, qbd_ref, m_ref, l_ref, acc_ref, carry_ref, *, n_pages):
    del pt_ref
    b = pl.program_id(0)
    j = pl.program_id(1)
    scale = FOX_HEAD_DIM ** -0.5
    hmask = _head_mask()

    @pl.when(j == 0)
    def _():
        q = q_ref[pl.ds(b, 1), :]
        q_bd = jnp.where(hmask, jnp.broadcast_to(q, (HEAD_PAD, TOK_WIDTH)), 0.0)
        qbd_ref[...] = q_bd.astype(BF16)
        logf_new = -_softplus(-(f_ref[pl.ds(b, 1), :] + bf_ref[...]))
        lfn_ref[pl.ds(b, 1), :] = logf_new
        eye = lax.broadcasted_iota(jnp.int32, (HEAD_PAD, LANES), 0) == lax.broadcasted_iota(jnp.int32, (HEAD_PAD, LANES), 1)
        carry_ref[...] = jnp.sum(jnp.where(eye, jnp.broadcast_to(logf_new, (HEAD_PAD, LANES)), 0.0), axis=-1, keepdims=True)
        m_ref[...] = jnp.sum(q_bd * kn_ref[pl.ds(b, 1), :], axis=-1, keepdims=True) * scale
        l_ref[...] = jnp.ones_like(l_ref)
        acc_ref[...] = jnp.broadcast_to(vn_ref[pl.ds(b, 1), :], (HEAD_PAD, TOK_WIDTH))

    lf = lf_ref[0]
    src = lax.broadcasted_iota(jnp.int32, (PAGE_SIZE, PAGE_SIZE), 0)
    dst = lax.broadcasted_iota(jnp.int32, (PAGE_SIZE, PAGE_SIZE), 1)
    later = jnp.where(src > dst, 1.0, 0.0).astype(F32)
    bias = jnp.dot(lf, later, precision=HIGHEST, preferred_element_type=F32) + carry_ref[...]
    s = lax.dot_general(qbd_ref[...], kp_ref[0].astype(BF16), (((1,), (1,)), ((), ())), preferred_element_type=F32)
    s = s * scale + bias
    m_old = m_ref[...]
    m_new = jnp.maximum(m_old, jnp.max(s, axis=-1, keepdims=True))
    p = jnp.exp(s - m_new)
    alpha = jnp.exp(m_old - m_new)
    l_ref[...] = alpha * l_ref[...] + jnp.sum(p, axis=-1, keepdims=True)
    acc_ref[...] = alpha * acc_ref[...] + jnp.dot(p.astype(BF16), vp_ref[0].astype(BF16), preferred_element_type=F32)
    m_ref[...] = m_new
    carry_ref[...] = carry_ref[...] + jnp.sum(lf, axis=-1, keepdims=True)

    @pl.when(j == n_pages - 1)
    def _():
        out = jnp.where(hmask, acc_ref[...] / l_ref[...], 0.0)
        o_ref[pl.ds(b, 1), :] = jnp.sum(out, axis=0, keepdims=True)


def fox_decode(page_table, q, k_new, v_new, f_ext, b_pad, k_pool, v_pool, logf_pool_t):
    batch, n_pages = page_table.shape
    whole = lambda shape: pl.BlockSpec(shape, lambda b, j, pt: (0,) * len(shape))
    page = lambda b, j, pt: (pt[b, n_pages - 1 - j], 0, 0)
    grid_spec = pltpu.PrefetchScalarGridSpec(
        num_scalar_prefetch=1,
        grid=(batch, n_pages),
        in_specs=[
            whole((batch, TOK_WIDTH)), whole((batch, TOK_WIDTH)), whole((batch, TOK_WIDTH)),
            whole((batch, LANES)), whole((1, LANES)),
            pl.BlockSpec((1, PAGE_SIZE, TOK_WIDTH), page),
            pl.BlockSpec((1, PAGE_SIZE, TOK_WIDTH), page),
            pl.BlockSpec((1, HEAD_PAD, PAGE_SIZE), page),
        ],
        out_specs=[whole((batch, TOK_WIDTH)), whole((batch, LANES))],
        scratch_shapes=[
            pltpu.VMEM((HEAD_PAD, TOK_WIDTH), BF16),
            pltpu.VMEM((HEAD_PAD, 1), F32),
            pltpu.VMEM((HEAD_PAD, 1), F32),
            pltpu.VMEM((HEAD_PAD, TOK_WIDTH), F32),
            pltpu.VMEM((HEAD_PAD, 1), F32),
        ],
    )
    return pl.pallas_call(
        functools.partial(_fox_decode_kernel, n_pages=n_pages),
        grid_spec=grid_spec,
        out_shape=[jax.ShapeDtypeStruct((batch, TOK_WIDTH), F32), jax.ShapeDtypeStruct((batch, LANES), F32)],
        compiler_params=_params("arbitrary", "arbitrary"),
        name="fox_decode",
    )(page_table, q, k_new, v_new, f_ext, b_pad, k_pool, v_pool, logf_pool_t)


def _mem_attn_kernel(q_ref, k_ref, v_ref, o_ref, *, tq):
    rows = max(tq, SUBLANES)
    scale = MEM_HEAD_DIM ** -0.5
    for h in range(MEM_HEADS):
        sl = slice(h * MEM_HEAD_DIM, (h + 1) * MEM_HEAD_DIM)
        qh = jnp.broadcast_to(q_ref[0, :, sl], (rows, MEM_HEAD_DIM)).astype(BF16)
        kh = k_ref[:, sl].astype(BF16)
        vh = v_ref[:, sl].astype(BF16)
        s = lax.dot_general(qh, kh, (((1,), (1,)), ((), ())), preferred_element_type=F32) * scale
        e = jnp.exp(s - jnp.max(s, axis=-1, keepdims=True))
        oh = jnp.dot(e.astype(BF16), vh, preferred_element_type=F32) / jnp.sum(e, axis=-1, keepdims=True)
        o_ref[0, :, sl] = oh[:tq]


def mem_attn(main3, mem_k, mem_v, *, tq):
    batch, seq, _ = main3.shape
    kv_spec = pl.BlockSpec((N_MEM, MEM_WIDTH), lambda b, i: (b, 0))
    return pl.pallas_call(
        functools.partial(_mem_attn_kernel, tq=tq),
        grid=(batch, seq // tq),
        in_specs=[pl.BlockSpec((1, tq, MEM_WIDTH), lambda b, i: (b, i, QMEM_OFF // MEM_WIDTH)), kv_spec, kv_spec],
        out_specs=pl.BlockSpec((1, tq, MEM_WIDTH), lambda b, i: (b, i, 0)),
        out_shape=jax.ShapeDtypeStruct((batch, seq, MEM_WIDTH), F32),
        compiler_params=_params("parallel", "parallel"),
        name="mem_attn",
    )(main3, mem_k, mem_v)


def _out_proj_kernel(ot_ref, om_ref, gate_ref, x_ref, wt_ref, wm_ref, gp_ref, y_ref):
    gate = gate_ref[...]
    sg = gate * _sigmoid(gate)
    a = (ot_ref[...] * sg[:, :TOK_WIDTH]).astype(BF16)
    b = (om_ref[...] * sg[:, TOK_WIDTH:]).astype(BF16)
    y = jnp.dot(a, wt_ref[...], preferred_element_type=F32) + jnp.dot(b, wm_ref[...], preferred_element_type=F32)
    y_ref[...] = x_ref[...] + y * _rms_scale(y) * gp_ref[...]


def out_proj(o_tok, o_mem, main, x, w_tok, w_mem, g_post, *, tm):
    m = x.shape[0]
    row = lambda width: pl.BlockSpec((tm, width), lambda i: (i, 0))
    const = lambda shape: pl.BlockSpec(shape, lambda i: (0, 0))
    return pl.pallas_call(
        _out_proj_kernel,
        grid=(m // tm,),
        in_specs=[row(TOK_WIDTH), row(MEM_WIDTH), row(D_MODEL), row(D_MODEL),
                  const((TOK_WIDTH, D_MODEL)), const((MEM_WIDTH, D_MODEL)), const((1, D_MODEL))],
        out_specs=row(D_MODEL),
        out_shape=jax.ShapeDtypeStruct((m, D_MODEL), F32),
        compiler_params=_params("parallel"),
        name="out_proj",
    )(o_tok, o_mem, main, x, w_tok, w_mem, g_post.reshape(1, D_MODEL))


PAIR = 2 * RWKV_HEAD_DIM
assert PAIR == LANES
N_PAIRS = RWKV_HEADS // 2


def _pair_ones():
    row = lax.broadcasted_iota(jnp.int32, (PAIR, PAIR), 0) // RWKV_HEAD_DIM
    col = lax.broadcasted_iota(jnp.int32, (PAIR, PAIR), 1) // RWKV_HEAD_DIM
    return jnp.where(row == col, 1.0, 0.0).astype(F32)


def _head_sum(x, ones):
    return jnp.dot(x, ones, precision=HIGHEST, preferred_element_type=F32)


def _stack(x):
    lane = lax.broadcasted_iota(jnp.int32, x.shape, 1)
    first = lane < RWKV_HEAD_DIM
    return jnp.concatenate([jnp.where(first, x, 0.0), jnp.where(first, 0.0, x)], axis=0)


def _bdot(a, b):
    return jnp.dot(a.astype(BF16), b.astype(BF16), preferred_element_type=F32)


def _wkv_chunk_terms(r, k, v, a_vec, b_vec, lw):
    c = r.shape[0]
    row = lax.broadcasted_iota(jnp.int32, (c, c), 0)
    col = lax.broadcasted_iota(jnp.int32, (c, c), 1)
    cum = jnp.dot(jnp.where(col <= row, 1.0, 0.0).astype(F32), lw, precision=HIGHEST, preferred_element_type=F32)
    last = cum[c - 1:c, :]
    e_pos = jnp.exp(cum)
    e_neg = jnp.exp(-cum)
    e_rem = jnp.exp(last - cum)
    a_t = _stack(a_vec * jnp.exp(cum - lw))
    r_t = _stack(r * e_pos)
    v_s = _stack(v)
    x = jnp.concatenate([a_t, r_t], axis=0).astype(BF16)
    y = jnp.concatenate([_stack(b_vec * e_neg), _stack(k * e_neg)], axis=0).astype(BF16)
    g = lax.dot_general(x, y, (((1,), (1,)), ((), ())), preferred_element_type=F32)
    gi = lax.broadcasted_iota(jnp.int32, g.shape, 0)
    gj = lax.broadcasted_iota(jnp.int32, g.shape, 1)
    ti, tj = gi & (c - 1), gj & (c - 1)
    g = jnp.where(tj < ti + gi // (2 * c), g, 0.0)
    l_ab, l_ak = g[:2 * c, :2 * c], g[:2 * c, 2 * c:]
    l_rb, l_rk = g[2 * c:, :2 * c], g[2 * c:, 2 * c:]
    eye = jnp.where(lax.broadcasted_iota(jnp.int32, (2 * c, 2 * c), 0) == lax.broadcasted_iota(jnp.int32, (2 * c, 2 * c), 1),
                    1.0, 0.0).astype(F32)
    t_inv = eye + l_ab
    l_pow = l_ab
    for _ in range(int(math.log2(c)) - 1):
        l_pow = _bdot(l_pow, l_pow)
        t_inv = t_inv + _bdot(l_pow, t_inv)
    akv = _bdot(l_ak, v_s)
    ta = _bdot(t_inv, jnp.concatenate([a_t, akv], axis=1))
    rbx = _bdot(l_rb, ta)
    r_new = r_t + rbx[:, :PAIR]
    y0 = rbx[:, PAIR:] + _bdot(l_rk, v_s)
    mn = _bdot(_stack(b_vec * e_rem).T, ta)
    m_c = mn[:, :PAIR] + eye * jnp.exp(last)
    n_c = mn[:, PAIR:] + _bdot(_stack(k * e_rem).T, v_s)
    return r_new, y0, m_c, n_c


def _token_shift(cur, prev_row, mu):
    rolled = pltpu.roll(cur, 1, 0)
    first = lax.broadcasted_iota(jnp.int32, cur.shape, 0) == 0
    prev = jnp.where(first, prev_row, rolled)
    return cur + (prev - cur) * mu


def _rwkv_transforms(r, k, v, e, w0, a0, k_k, k_a, ww2, wa2, ones):
    w = -_softplus(-(w0 + jnp.dot(jnp.tanh(e).astype(BF16), ww2, preferred_element_type=F32))) - 0.5
    lw = -jnp.exp(w)
    a = _sigmoid(a0 + jnp.dot(e.astype(BF16), wa2, preferred_element_type=F32))
    kk = k * k_k
    kk = kk / jnp.maximum(jnp.sqrt(_head_sum(kk * kk, ones)), 1e-12)
    k_mod = k * (1.0 + (a - 1.0) * k_a)
    return lw, a, kk, k_mod


def _rwkv_output(y, r, k_mod, v, r_k, ln_w, ln_b, ones):
    mean = _head_sum(y, ones) * (1.0 / RWKV_HEAD_DIM)
    d = y - mean
    var = _head_sum(d * d, ones) * (1.0 / RWKV_HEAD_DIM)
    yn = d * lax.rsqrt(var + GN_EPS) * ln_w + ln_b
    return yn + _head_sum(r * k_mod * r_k, ones) * v


def _wkv_prompt_kernel(r_ref, k_ref, v_ref, e_ref, mur_ref, muk_ref, muv_ref, mue_ref, w0_ref, a0_ref,
                       kk_ref, ka_ref, rk_ref, lnw_ref, lnb_ref, ww2_ref, wa2_ref,
                       o_ref, st_ref, pr_ref, pk_ref, pv_ref, pe_ref, s_ref, y_ref, *, tb):
    t = pl.program_id(2)

    @pl.when(t == 0)
    def _():
        for ref in (pr_ref, pk_ref, pv_ref, pe_ref, s_ref):
            ref[...] = jnp.zeros_like(ref)

    ones = _pair_ones()
    r = _token_shift(r_ref[...], pr_ref[...], mur_ref[...])
    k = _token_shift(k_ref[...], pk_ref[...], muk_ref[...])
    v = _token_shift(v_ref[...], pv_ref[...], muv_ref[...])
    e = _token_shift(e_ref[...], pe_ref[...], mue_ref[...])
    for cur, prev in ((r_ref, pr_ref), (k_ref, pk_ref), (v_ref, pv_ref), (e_ref, pe_ref)):
        prev[...] = cur[tb - 1:tb, :]
    lw, a, kk, k_mod = _rwkv_transforms(r, k, v, e, w0_ref[...], a0_ref[...], kk_ref[...], ka_ref[...],
                                        ww2_ref[...], wa2_ref[...], ones)
    terms = []
    for c0 in range(0, tb, CHUNK):
        sl = slice(c0, c0 + CHUNK)
        terms.append(_wkv_chunk_terms(r[sl], k_mod[sl], v[sl], -kk[sl], (kk * a)[sl], lw[sl]))
    s = s_ref[...]
    for i, (r_new, y0, m_c, n_c) in enumerate(terms):
        ys = _bdot(r_new, s) + y0
        y_ref[i * CHUNK:(i + 1) * CHUNK, :] = ys[:CHUNK] + ys[CHUNK:]
        s = _bdot(m_c, s) + n_c
    s_ref[...] = s
    o_ref[...] = _rwkv_output(y_ref[...], r, k_mod, v, rk_ref[...], lnw_ref[...], lnb_ref[...], ones)

    @pl.when(t == pl.num_programs(2) - 1)
    def _():
        st_ref[0, 0] = s


def wkv_prompt(main, ext, mu_rkv, mu_e, w0, a0, k_k, k_a, r_k, ln_w, ln_b, ww2, wa2, *, batch, seq, tb):
    nt = seq // tb
    rb, kb, vb = TOKA_OFF // LANES, TOKB_OFF // LANES, TOKC_OFF // LANES
    rows = lambda off: pl.BlockSpec((tb, LANES), lambda b, p, t: (b * nt + t, off + p))
    vec = lambda off: pl.BlockSpec((1, LANES), lambda b, p, t: (0, off + p))
    lora = pl.BlockSpec((RWKV_EXTRA, LANES), lambda b, p, t: (0, p))
    return pl.pallas_call(
        functools.partial(_wkv_prompt_kernel, tb=tb),
        grid=(batch, N_PAIRS, nt),
        in_specs=[
            rows(rb), rows(kb), rows(vb),
            pl.BlockSpec((tb, RWKV_EXTRA), lambda b, p, t: (b * nt + t, 0)),
            vec(0), vec(N_PAIRS), vec(2 * N_PAIRS),
            pl.BlockSpec((1, RWKV_EXTRA), lambda b, p, t: (0, 0)),
            vec(0), vec(0), vec(0), vec(0), vec(0), vec(0), vec(0), lora, lora,
        ],
        out_specs=[
            pl.BlockSpec((tb, LANES), lambda b, p, t: (b * nt + t, p)),
            pl.BlockSpec((1, 1, PAIR, PAIR), lambda b, p, t: (b, p, 0, 0)),
        ],
        out_shape=[
            jax.ShapeDtypeStruct((batch * seq, TOK_WIDTH), F32),
            jax.ShapeDtypeStruct((batch, N_PAIRS, PAIR, PAIR), F32),
        ],
        scratch_shapes=[
            pltpu.VMEM((1, LANES), F32), pltpu.VMEM((1, LANES), F32), pltpu.VMEM((1, LANES), F32),
            pltpu.VMEM((1, RWKV_EXTRA), F32), pltpu.VMEM((PAIR, PAIR), F32), pltpu.VMEM((tb, LANES), F32),
        ],
        compiler_params=_params("parallel", "parallel", "arbitrary"),
        name="wkv_prompt",
    )(main, main, main, ext, mu_rkv, mu_rkv, mu_rkv, mu_e, w0, a0, k_k, k_a, r_k, ln_w, ln_b, ww2, wa2)


def _wkv_step_kernel(cur_ref, prev_ref, e_ref, pe_ref, mu_ref, mue_ref, w0_ref, a0_ref, kk_ref, ka_ref, rk_ref,
                     lnw_ref, lnb_ref, ww2_ref, wa2_ref, st_ref, o_ref, stn_ref):
    hd = RWKV_HEAD_DIM
    e_cur = e_ref[0]
    e = e_cur + (pe_ref[0] - e_cur) * mue_ref[...]
    e_rows = jnp.broadcast_to(e, (SUBLANES, RWKV_EXTRA))
    e_tanh = jnp.tanh(e_rows).astype(BF16)
    e_lin = e_rows.astype(BF16)
    eye = lax.broadcasted_iota(jnp.int32, (hd, hd), 0) == lax.broadcasted_iota(jnp.int32, (hd, hd), 1)

    def mixed(i, h):
        cur = cur_ref[0, i, h:h + 1, :]
        return cur + (prev_ref[0, i, h:h + 1, :] - cur) * mu_ref[i, h:h + 1, :]

    for h in range(RWKV_HEADS):
        row = lambda ref: ref[h:h + 1, :]
        r, k, v = mixed(0, h), mixed(1, h), mixed(2, h)
        w = -_softplus(-(row(w0_ref) + jnp.dot(e_tanh, ww2_ref[h], preferred_element_type=F32)[0:1])) - 0.5
        decay = jnp.exp(-jnp.exp(w))
        a = _sigmoid(row(a0_ref) + jnp.dot(e_lin, wa2_ref[h], preferred_element_type=F32)[0:1])
        kk = k * row(kk_ref)
        kk = kk / jnp.maximum(jnp.sqrt(jnp.sum(kk * kk, axis=-1, keepdims=True)), 1e-12)
        k_mod = k * (1.0 + (a - 1.0) * row(ka_ref))
        s = st_ref[0, h]
        sa = jnp.sum(s * (-kk), axis=-1, keepdims=True)
        v_col = jnp.sum(jnp.where(eye, jnp.broadcast_to(v, (hd, hd)), 0.0), axis=-1, keepdims=True)
        s_new = s * decay + sa * (kk * a) + v_col * k_mod
        stn_ref[0, h] = s_new
        y_col = jnp.sum(s_new * r, axis=-1, keepdims=True)
        y = jnp.sum(jnp.where(eye, jnp.broadcast_to(y_col, (hd, hd)), 0.0), axis=0, keepdims=True)
        mean = jnp.mean(y, axis=-1, keepdims=True)
        d = y - mean
        var = jnp.mean(d * d, axis=-1, keepdims=True)
        yn = d * lax.rsqrt(var + GN_EPS) * row(lnw_ref) + row(lnb_ref)
        o_ref[0, h:h + 1, :] = yn + jnp.sum(r * k_mod * row(rk_ref), axis=-1, keepdims=True) * v


def wkv_step(cur, prev, e_cur, e_prev, mu, mu_e, w0, a0, k_k, k_a, r_k, ln_w, ln_b, ww2h, wa2h, state):
    batch = cur.shape[0]
    hshape = (RWKV_HEADS, RWKV_HEAD_DIM)
    per_row = lambda shape: pl.BlockSpec((1,) + shape, lambda b: (b,) + (0,) * len(shape))
    const = lambda shape: pl.BlockSpec(shape, lambda b: (0,) * len(shape))
    return pl.pallas_call(
        _wkv_step_kernel,
        grid=(batch,),
        in_specs=[per_row((3,) + hshape), per_row((3,) + hshape), per_row((1, RWKV_EXTRA)), per_row((1, RWKV_EXTRA)),
                  const((3,) + hshape), const((1, RWKV_EXTRA))] + [const(hshape)] * 7
                 + [const((RWKV_HEADS, RWKV_EXTRA, RWKV_HEAD_DIM))] * 2 + [per_row(hshape + (RWKV_HEAD_DIM,))],
        out_specs=[per_row(hshape), per_row(hshape + (RWKV_HEAD_DIM,))],
        out_shape=[jax.ShapeDtypeStruct((batch,) + hshape, F32),
                   jax.ShapeDtypeStruct((batch,) + hshape + (RWKV_HEAD_DIM,), F32)],
        compiler_params=_params("parallel"),
        name="wkv_step",
    )(cur, prev, e_cur, e_prev, mu, mu_e, w0, a0, k_k, k_a, r_k, ln_w, ln_b, ww2h, wa2h, state)


def _relayout_in_proj(w_in, tok_cols, extra_cols, extra_width):
    tok = w_in[:, :tok_cols]
    extra = w_in[:, tok_cols:tok_cols + extra_cols]
    q_mem = w_in[:, tok_cols + extra_cols:tok_cols + extra_cols + MEM_WIDTH]
    gate = w_in[:, tok_cols + extra_cols + MEM_WIDTH:]
    main = jnp.concatenate([gate, tok, q_mem], axis=1).astype(BF16)
    extra = jnp.pad(extra, ((0, 0), (0, extra_width - extra_cols))).astype(BF16)
    return main, extra


def _pad_lanes(v, width):
    return jnp.pad(v, (0, width - v.shape[0])).reshape(1, width)


def _split_w_o(w_o):
    return w_o[:TOK_WIDTH].astype(BF16), w_o[TOK_WIDTH:].astype(BF16)


def _memory_kv(mem, g_mem, w_mem):
    kv = norm_proj(mem, g_mem, w_mem.astype(BF16), tm=mem.shape[0], tn=MEM_WIDTH)
    return kv[:, :MEM_WIDTH], kv[:, MEM_WIDTH:]


def kernel(x_prompt, x_sample, cache_fox_k, cache_fox_v, cache_fox_logf, cache_mem_k, cache_mem_v, state_rwkv_wkv, state_rwkv_shift, page_table, mem_prompt, g_pre, g_post, g_mem, w_mem, w_o, fox_w_in, fox_b_f, rwkv_w_in, rwkv_mu, rwkv_w0, rwkv_w_w2, rwkv_a0, rwkv_w_a2, rwkv_k_k, rwkv_k_a, rwkv_r_k, rwkv_ln_w, rwkv_ln_b):
    bp, seq, _ = x_prompt.shape
    bs = x_sample.shape[0]
    n_pool = cache_fox_k.shape[1]
    xp = x_prompt.reshape(bp * seq, D_MODEL)
    xs = x_sample.reshape(bs, D_MODEL)
    mem = mem_prompt.reshape(bp * N_MEM, D_MODEL)
    tm_p, tn = 1024, 512
    mem_shape = (bp, N_MEM, MEM_HEADS, MEM_HEAD_DIM)

    w_main, w_ext = _relayout_in_proj(fox_w_in[0], 3 * TOK_WIDTH, FOX_HEADS, FOX_EXTRA)
    w_tok, w_mo = _split_w_o(w_o[0])
    b_pad = _pad_lanes(fox_b_f[0], LANES)
    mk0, mv0 = _memory_kv(mem, g_mem[0], w_mem[0])

    main_p, ext_p = norm_proj(xp, g_pre[0], w_main, w_ext, tm=tm_p, tn=tn)
    logf_p, ccol, crow = fox_gate(ext_p, b_pad, batch=bp, seq=seq, tb=256)
    o_tok = fox_flash(main_p, ccol, crow, batch=bp, seq=seq, tq=512)
    o_mem = mem_attn(main_p.reshape(bp, seq, MAIN_WIDTH), mk0, mv0, tq=1024)
    xp = out_proj(o_tok, o_mem.reshape(bp * seq, MEM_WIDTH), main_p, xp, w_tok, w_mo, g_post[0], tm=256)

    main_s, ext_s = norm_proj(xs, g_pre[0], w_main, w_ext, tm=bs, tn=tn)
    q_s, k_s, v_s = (main_s[:, off:off + TOK_WIDTH] for off in (TOKA_OFF, TOKB_OFF, TOKC_OFF))
    logf_pool_t = jnp.pad(jnp.transpose(cache_fox_logf[0], (0, 2, 1)), ((0, 0), (0, HEAD_PAD - FOX_HEADS), (0, 0)))
    o_tok_s, logf_s = fox_decode(page_table, q_s, k_s, v_s, ext_s, b_pad,
                                 cache_fox_k[0].reshape(n_pool, PAGE_SIZE, TOK_WIDTH),
                                 cache_fox_v[0].reshape(n_pool, PAGE_SIZE, TOK_WIDTH), logf_pool_t)
    o_mem_s = mem_attn(main_s.reshape(bs, 1, MAIN_WIDTH), cache_mem_k[0].reshape(bs * N_MEM, MEM_WIDTH),
                       cache_mem_v[0].reshape(bs * N_MEM, MEM_WIDTH), tq=1)
    xs = out_proj(o_tok_s, o_mem_s.reshape(bs, MEM_WIDTH), main_s, xs, w_tok, w_mo, g_post[0], tm=bs)

    head_shape = (FOX_HEADS, FOX_HEAD_DIM)
    fox_k_p = main_p[:, TOKB_OFF:TOKC_OFF].reshape((1, bp, seq) + head_shape)
    fox_v_p = main_p[:, TOKC_OFF:QMEM_OFF].reshape((1, bp, seq) + head_shape)
    fox_f_p = logf_p[:, :FOX_HEADS].reshape(1, bp, seq, FOX_HEADS)
    fox_k_s = k_s.reshape((1, bs, 1) + head_shape)
    fox_v_s = v_s.reshape((1, bs, 1) + head_shape)
    fox_f_s = logf_s[:, :FOX_HEADS].reshape(1, bs, 1, FOX_HEADS)

    w_main, w_ext = _relayout_in_proj(rwkv_w_in[0], 3 * TOK_WIDTH, 2 * LORA, RWKV_EXTRA)
    w_tok, w_mo = _split_w_o(w_o[1])
    mk1, mv1 = _memory_kv(mem, g_mem[1], w_mem[1])
    mu_rkv = rwkv_mu[0][:3 * TOK_WIDTH].reshape(1, 3 * TOK_WIDTH)
    mu_e = _pad_lanes(rwkv_mu[0][3 * TOK_WIDTH:], RWKV_EXTRA)
    ww2 = jnp.pad(rwkv_w_w2[0], ((0, RWKV_EXTRA - LORA), (0, 0))).astype(BF16)
    wa2 = jnp.pad(rwkv_w_a2[0], ((LORA, RWKV_EXTRA - 2 * LORA), (0, 0))).astype(BF16)
    vecs = [p[0] for p in (rwkv_w0, rwkv_a0, rwkv_k_k, rwkv_k_a, rwkv_r_k, rwkv_ln_w, rwkv_ln_b)]

    main_p, ext_p = norm_proj(xp, g_pre[1], w_main, w_ext, tm=tm_p, tn=tn)
    o_tok, st_pairs = wkv_prompt(main_p, ext_p, mu_rkv, mu_e, *[v.reshape(1, TOK_WIDTH) for v in vecs], ww2, wa2,
                                 batch=bp, seq=seq, tb=512)
    o_mem = mem_attn(main_p.reshape(bp, seq, MAIN_WIDTH), mk1, mv1, tq=1024)
    xp = out_proj(o_tok, o_mem.reshape(bp * seq, MEM_WIDTH), main_p, xp, w_tok, w_mo, g_post[1], tm=256)

    main_s, ext_s = norm_proj(xs, g_pre[1], w_main, w_ext, tm=bs, tn=tn)
    hshape = (RWKV_HEADS, RWKV_HEAD_DIM)
    shift_prev = state_rwkv_shift[0][:, 0, :]
    cur = main_s[:, TOKA_OFF:QMEM_OFF].reshape((bs, 3) + hshape)
    prev = shift_prev[:, :3 * TOK_WIDTH].reshape((bs, 3) + hshape)
    e_prev = jnp.pad(shift_prev[:, 3 * TOK_WIDTH:], ((0, 0), (0, RWKV_EXTRA - 2 * LORA))).reshape(bs, 1, RWKV_EXTRA)
    per_head = lambda w: jnp.transpose(w.reshape((RWKV_EXTRA,) + hshape), (1, 0, 2))
    o_tok_s, wkv_s = wkv_step(cur, prev, ext_s.reshape(bs, 1, RWKV_EXTRA), e_prev, mu_rkv.reshape((3,) + hshape), mu_e,
                              *[v.reshape(hshape) for v in vecs], per_head(ww2), per_head(wa2), state_rwkv_wkv[0])
    o_mem_s = mem_attn(main_s.reshape(bs, 1, MAIN_WIDTH), cache_mem_k[1].reshape(bs * N_MEM, MEM_WIDTH),
                       cache_mem_v[1].reshape(bs * N_MEM, MEM_WIDTH), tq=1)
    xs = out_proj(o_tok_s.reshape(bs, TOK_WIDTH), o_mem_s.reshape(bs, MEM_WIDTH), main_s, xs, w_tok, w_mo, g_post[1], tm=bs)

    st6 = st_pairs.reshape(bp, N_PAIRS, 2, RWKV_HEAD_DIM, 2, RWKV_HEAD_DIM)
    st_heads = jnp.stack([st6[:, :, 0, :, 0, :], st6[:, :, 1, :, 1, :]], axis=2)
    wkv_p = jnp.swapaxes(st_heads.reshape((bp,) + hshape + (RWKV_HEAD_DIM,)), -1, -2)
    last_p = main_p.reshape(bp, seq, MAIN_WIDTH)[:, seq - 1, TOKA_OFF:QMEM_OFF]
    last_e = ext_p.reshape(bp, seq, RWKV_EXTRA)[:, seq - 1, :2 * LORA]
    shift_p = jnp.concatenate([last_p, last_e], axis=-1).reshape(1, bp, 1, 3 * TOK_WIDTH + 2 * LORA)
    shift_s = jnp.concatenate([main_s[:, TOKA_OFF:QMEM_OFF], ext_s[:, :2 * LORA]], axis=-1).reshape(1, bs, 1, 3 * TOK_WIDTH + 2 * LORA)

    return (xp.reshape(bp, seq, D_MODEL), xs.reshape(bs, 1, D_MODEL),
            jnp.stack([mk0.reshape(mem_shape), mk1.reshape(mem_shape)]),
            jnp.stack([mv0.reshape(mem_shape), mv1.reshape(mem_shape)]),
            fox_k_p, fox_v_p, fox_f_p, fox_k_s, fox_v_s, fox_f_s,
            wkv_p[None], shift_p, wkv_s[None], shift_s)
```

```python
import functools
import math

import jax
import jax.numpy as jnp
from jax import lax
from jax.experimental import pallas as pl
from jax.experimental.pallas import tpu as pltpu

F32 = jnp.float32
BF16 = jnp.bfloat16

D_MODEL = 2048
N_MEM = 256
MEM_HEADS = 4
MEM_HEAD_DIM = 128
MEM_WIDTH = MEM_HEADS * MEM_HEAD_DIM
TOK_WIDTH = D_MODEL - MEM_WIDTH
FOX_HEAD_DIM = 128
FOX_HEADS = TOK_WIDTH // FOX_HEAD_DIM
RWKV_HEAD_DIM = 64
RWKV_HEADS = TOK_WIDTH // RWKV_HEAD_DIM
LORA = 96
PAGE_SIZE = 128
RMS_EPS = 1e-6
GN_EPS = 64e-5

LANES = 128
SUBLANES = 8
V7X_VMEM_BYTES = 64 * 1024 * 1024
VMEM_LIMIT_BYTES = V7X_VMEM_BYTES - 8 * 1024 * 1024

GATE_OFF = 0
TOKA_OFF = D_MODEL
TOKB_OFF = TOKA_OFF + TOK_WIDTH
TOKC_OFF = TOKB_OFF + TOK_WIDTH
QMEM_OFF = TOKC_OFF + TOK_WIDTH
MAIN_WIDTH = QMEM_OFF + MEM_WIDTH
FOX_EXTRA = LANES
RWKV_EXTRA = 2 * LANES

HEAD_PAD = 16
CHUNK = 64


def _params(*sem):
    return pltpu.CompilerParams(dimension_semantics=sem, vmem_limit_bytes=VMEM_LIMIT_BYTES)


def _rms_scale(x):
    return lax.rsqrt(jnp.mean(x * x, axis=-1, keepdims=True) + RMS_EPS)


def _sigmoid(x):
    return 1.0 / (1.0 + jnp.exp(-x))


def _softplus(x):
    return jnp.maximum(x, 0.0) + jnp.log(1.0 + jnp.exp(-jnp.abs(x)))


def _split3(x):
    hi = x.astype(BF16)
    r1 = x - hi.astype(F32)
    mid = r1.astype(BF16)
    lo = (r1 - mid.astype(F32)).astype(BF16)
    return hi, mid, lo


def _dot_exact_rhs(x, rhs):
    rows = x.shape[0]
    parts = jnp.dot(jnp.concatenate(_split3(x), axis=0), rhs.astype(BF16), preferred_element_type=F32)
    return parts[:rows] + parts[rows:2 * rows] + parts[2 * rows:]


def _dot_exact_lhs(lhs, x):
    cols = x.shape[1]
    parts = jnp.dot(lhs.astype(BF16), jnp.concatenate(_split3(x), axis=1), preferred_element_type=F32)
    return parts[:, :cols] + parts[:, cols:2 * cols] + parts[:, 2 * cols:]


def _norm_proj_kernel(x_ref, g_ref, w_ref, *rest, has_extra):
    if has_extra:
        we_ref, o_ref, oe_ref, h_ref = rest
    else:
        o_ref, h_ref = rest

    @pl.when(pl.program_id(1) == 0)
    def _():
        x = x_ref[...]
        h = (x * _rms_scale(x) * g_ref[...]).astype(BF16)
        h_ref[...] = h
        if has_extra:
            oe_ref[...] = jnp.dot(h, we_ref[...], preferred_element_type=F32)

    o_ref[...] = jnp.dot(h_ref[...], w_ref[...], preferred_element_type=F32)


def norm_proj(x, g, w, w_extra=None, *, tm, tn):
    m, d = x.shape
    n = w.shape[1]
    assert m % tm == 0 and n % tn == 0
    has_extra = w_extra is not None
    in_specs = [
        pl.BlockSpec((tm, d), lambda i, j: (i, 0)),
        pl.BlockSpec((1, d), lambda i, j: (0, 0)),
        pl.BlockSpec((d, tn), lambda i, j: (0, j)),
    ]
    out_specs = [pl.BlockSpec((tm, tn), lambda i, j: (i, j))]
    out_shape = [jax.ShapeDtypeStruct((m, n), F32)]
    args = [x, g.reshape(1, d), w]
    if has_extra:
        e = w_extra.shape[1]
        in_specs.append(pl.BlockSpec((d, e), lambda i, j: (0, 0)))
        out_specs.append(pl.BlockSpec((tm, e), lambda i, j: (i, 0)))
        out_shape.append(jax.ShapeDtypeStruct((m, e), F32))
        args.append(w_extra)
    outs = pl.pallas_call(
        functools.partial(_norm_proj_kernel, has_extra=has_extra),
        grid=(m // tm, n // tn),
        in_specs=in_specs,
        out_specs=out_specs,
        out_shape=out_shape,
        scratch_shapes=[pltpu.VMEM((tm, d), BF16)],
        compiler_params=_params("parallel", "arbitrary"),
        name="norm_proj",
    )(*args)
    return outs if has_extra else outs[0]


def _fox_gate_kernel(f_ref, b_ref, logf_ref, ccol_ref, crow_ref, carry_ref, *, tb):
    @pl.when(pl.program_id(1) == 0)
    def _():
        carry_ref[...] = jnp.zeros_like(carry_ref)

    logf = -_softplus(-(f_ref[...] + b_ref[...]))
    logf_ref[...] = logf
    row = lax.broadcasted_iota(jnp.int32, (tb, tb), 0)
    col = lax.broadcasted_iota(jnp.int32, (tb, tb), 1)
    c = _dot_exact_lhs(jnp.where(col <= row, 1.0, 0.0), logf) + carry_ref[...]
    ccol_ref[...] = c
    crow_ref[0] = c.T[:HEAD_PAD]
    carry_ref[...] = c[tb - 1:tb, :]


def fox_gate(f_ext, b_pad, *, batch, seq, tb):
    nt = seq // tb
    row_spec = pl.BlockSpec((tb, LANES), lambda b, t: (b * nt + t, 0))
    return pl.pallas_call(
        functools.partial(_fox_gate_kernel, tb=tb),
        grid=(batch, nt),
        in_specs=[row_spec, pl.BlockSpec((1, LANES), lambda b, t: (0, 0))],
        out_specs=[row_spec, row_spec, pl.BlockSpec((1, HEAD_PAD, tb), lambda b, t: (b, 0, t))],
        out_shape=[
            jax.ShapeDtypeStruct((batch * seq, LANES), F32),
            jax.ShapeDtypeStruct((batch * seq, LANES), F32),
            jax.ShapeDtypeStruct((batch, HEAD_PAD, seq), F32),
        ],
        scratch_shapes=[pltpu.VMEM((1, LANES), F32)],
        compiler_params=_params("parallel", "arbitrary"),
        name="fox_gate",
    )(f_ext, b_pad)


_MASKED = -1e30


def _fox_flash_kernel(q_ref, k_ref, v_ref, ccol_ref, crow_ref, o_ref, kb_ref, vb_ref, *, tq):
    h = pl.program_id(1)
    qi = pl.program_id(2)

    @pl.when(qi == 0)
    def _():
        kb_ref[...] = k_ref[...].astype(BF16)
        vb_ref[...] = v_ref[...].astype(BF16)

    q = q_ref[...].astype(BF16)
    lane = lax.broadcasted_iota(jnp.int32, (tq, LANES), 1)
    cq = jnp.sum(jnp.where(lane == h, ccol_ref[...], 0.0), axis=-1, keepdims=True)
    scale = FOX_HEAD_DIM ** -0.5

    def step(c, carry, diagonal):
        m, l, acc = carry
        start = pl.multiple_of(c * tq, tq)
        kc = kb_ref[pl.ds(start, tq), :]
        vc = vb_ref[pl.ds(start, tq), :]
        ck = crow_ref[0, 0, :, pl.ds(start, tq)]
        s = lax.dot_general(q, kc, (((1,), (1,)), ((), ())), preferred_element_type=F32)
        s = s * scale + (cq - ck)
        if diagonal:
            row = lax.broadcasted_iota(jnp.int32, (tq, tq), 0)
            col = lax.broadcasted_iota(jnp.int32, (tq, tq), 1)
            s = jnp.where(col <= row, s, _MASKED)
        m_new = jnp.maximum(m, jnp.max(s, axis=-1, keepdims=True))
        p = jnp.exp(s - m_new)
        alpha = jnp.exp(m - m_new)
        l = alpha * l + jnp.sum(p, axis=-1, keepdims=True)
        acc = alpha * acc + jnp.dot(p.astype(BF16), vc, preferred_element_type=F32)
        return m_new, l, acc

    init = (jnp.full((tq, 1), _MASKED, F32), jnp.zeros((tq, 1), F32), jnp.zeros((tq, FOX_HEAD_DIM), F32))
    carry = lax.fori_loop(0, qi, lambda c, cr: step(c, cr, False), init)
    _, l, acc = step(qi, carry, True)
    o_ref[...] = acc / l


def fox_flash(main, ccol, crow, *, batch, seq, tq):
    nq = seq // tq
    qb, kb, vb = TOKA_OFF // LANES, TOKB_OFF // LANES, TOKC_OFF // LANES
    crow4 = crow.reshape(batch, HEAD_PAD, 1, seq)
    return pl.pallas_call(
        functools.partial(_fox_flash_kernel, tq=tq),
        grid=(batch, FOX_HEADS, nq),
        in_specs=[
            pl.BlockSpec((tq, LANES), lambda b, h, i: (b * nq + i, qb + h)),
            pl.BlockSpec((seq, LANES), lambda b, h, i: (b, kb + h)),
            pl.BlockSpec((seq, LANES), lambda b, h, i: (b, vb + h)),
            pl.BlockSpec((tq, LANES), lambda b, h, i: (b * nq + i, 0)),
            pl.BlockSpec((1, 1, 1, seq), lambda b, h, i: (b, h, 0, 0)),
        ],
        out_specs=pl.BlockSpec((tq, LANES), lambda b, h, i: (b * nq + i, h)),
        out_shape=jax.ShapeDtypeStruct((batch * seq, TOK_WIDTH), F32),
        scratch_shapes=[pltpu.VMEM((seq, LANES), BF16), pltpu.VMEM((seq, LANES), BF16)],
        compiler_params=_params("parallel", "parallel", "arbitrary"),
        name="fox_flash",
    )(main, main, main, ccol, crow4)


PAGE_ROWS = PAGE_SIZE * HEAD_PAD
N_SLOTS = 2
PAGES_PER_STEP = 4


def _fox_decode_kernel(pt_ref, q_ref, kn_ref, vn_ref, f_ref, bf_ref, *rest, n_pages):
    pps = PAGES_PER_STEP
    lf_refs = rest[:pps]
    k_hbm, v_hbm, o_ref, lfn_ref, kbuf, vbuf, sem, m_ref, l_ref, acc_ref, carry_ref = rest[pps:]
    b = pl.program_id(0)
    j = pl.program_id(1)
    n_steps = n_pages // pps
    n = b * n_steps + j
    n_total = pl.num_programs(0) * n_steps
    slot = lax.rem(n, N_SLOTS)
    scale = FOX_HEAD_DIM ** -0.5

    def group_copies(step, dst_slot):
        bb = lax.div(step, n_steps)
        last = n_pages - 1 - lax.rem(step, n_steps) * pps
        copies = []
        for i in range(pps):
            page = pt_ref[bb, last - i]
            for t, (src, dst) in enumerate(((k_hbm, kbuf), (v_hbm, vbuf))):
                copies.append(pltpu.make_async_copy(src.at[page], dst.at[dst_slot, i, :, pl.ds(0, FOX_HEADS), :],
                                                    sem.at[t, dst_slot]))
        return copies

    @pl.when(n == 0)
    def _():
        pad = jnp.zeros((N_SLOTS, pps, PAGE_SIZE, HEAD_PAD - FOX_HEADS, FOX_HEAD_DIM), F32)
        kbuf[:, :, :, FOX_HEADS:, :] = pad
        vbuf[:, :, :, FOX_HEADS:, :] = pad
        for c in group_copies(n, slot):
            c.start()

    @pl.when(n + 1 < n_total)
    def _():
        for c in group_copies(n + 1, 1 - slot):
            c.start()

    q = q_ref[0]

    @pl.when(j == 0)
    def _():
        logf_new = -_softplus(-(f_ref[pl.ds(b, 1), :] + bf_ref[...]))
        lfn_ref[pl.ds(b, 1), :] = logf_new
        eye = lax.broadcasted_iota(jnp.int32, (HEAD_PAD, LANES), 0) == lax.broadcasted_iota(jnp.int32, (HEAD_PAD, LANES), 1)
        carry_ref[...] = jnp.sum(jnp.where(eye, jnp.broadcast_to(logf_new, (HEAD_PAD, LANES)), 0.0), axis=-1, keepdims=True)
        m_ref[...] = jnp.sum(q * kn_ref[0], axis=-1, keepdims=True) * scale
        l_ref[...] = jnp.ones_like(l_ref)
        acc_ref[...] = vn_ref[0]

    lfs = [ref[0] for ref in lf_refs]
    src = lax.broadcasted_iota(jnp.int32, (PAGE_SIZE, PAGE_ROWS), 0)
    dst = lax.broadcasted_iota(jnp.int32, (PAGE_SIZE, PAGE_ROWS), 1) // HEAD_PAD
    later_rows = jnp.where(src > dst, 1.0, 0.0)
    inside = _dot_exact_rhs(jnp.concatenate(lfs, axis=0), later_rows)
    carries = [carry_ref[...]]
    for lf in lfs:
        carries.append(carries[-1] + jnp.sum(lf, axis=-1, keepdims=True))
    carry_ref[...] = carries[-1]

    for c in group_copies(n, slot):
        c.wait()
    own = (lax.broadcasted_iota(jnp.int32, (HEAD_PAD, PAGE_ROWS), 1) & (HEAD_PAD - 1)) == lax.broadcasted_iota(jnp.int32, (HEAD_PAD, PAGE_ROWS), 0)
    qb = q.astype(BF16)
    scores = []
    for i in range(pps):
        k_rows = kbuf[slot, i].reshape(PAGE_ROWS, FOX_HEAD_DIM).astype(BF16)
        s = lax.dot_general(qb, k_rows, (((1,), (1,)), ((), ())), preferred_element_type=F32)
        bias = inside[i * HEAD_PAD:(i + 1) * HEAD_PAD] + carries[i]
        scores.append(jnp.where(own, s * scale + bias, _MASKED))
    m_old = m_ref[...]
    m_new = m_old
    for s in scores:
        m_new = jnp.maximum(m_new, jnp.max(s, axis=-1, keepdims=True))
    alpha = jnp.exp(m_old - m_new)
    l_new = alpha * l_ref[...]
    acc = alpha * acc_ref[...]
    for i, s in enumerate(scores):
        p = jnp.exp(s - m_new)
        l_new = l_new + jnp.sum(p, axis=-1, keepdims=True)
        v_rows = vbuf[slot, i].reshape(PAGE_ROWS, FOX_HEAD_DIM).astype(BF16)
        acc = acc + jnp.dot(p.astype(BF16), v_rows, preferred_element_type=F32)
    m_ref[...] = m_new
    l_ref[...] = l_new
    acc_ref[...] = acc

    @pl.when(j == n_steps - 1)
    def _():
        o_ref[0] = acc / l_new


def fox_decode(page_table, q, k_new, v_new, f_ext, b_pad, k_pool, v_pool, logf_pool_t):
    batch, n_pages = page_table.shape
    pps = PAGES_PER_STEP
    assert n_pages % pps == 0
    whole = lambda shape: pl.BlockSpec(shape, lambda b, j, pt: (0,) * len(shape))
    per_row = pl.BlockSpec((1, HEAD_PAD, FOX_HEAD_DIM), lambda b, j, pt: (b, 0, 0))
    lf_spec = lambda i: pl.BlockSpec((1, HEAD_PAD, PAGE_SIZE), lambda b, j, pt: (pt[b, n_pages - 1 - j * pps - i], 0, 0))
    buf = pltpu.VMEM((N_SLOTS, pps, PAGE_SIZE, HEAD_PAD, FOX_HEAD_DIM), F32)
    grid_spec = pltpu.PrefetchScalarGridSpec(
        num_scalar_prefetch=1,
        grid=(batch, n_pages // pps),
        in_specs=[per_row, per_row, per_row, whole((batch, LANES)), whole((1, LANES))]
                 + [lf_spec(i) for i in range(pps)]
                 + [pl.BlockSpec(memory_space=pl.ANY), pl.BlockSpec(memory_space=pl.ANY)],
        out_specs=[per_row, whole((batch, LANES))],
        scratch_shapes=[
            buf, buf, pltpu.SemaphoreType.DMA((2, N_SLOTS)),
            pltpu.VMEM((HEAD_PAD, 1), F32),
            pltpu.VMEM((HEAD_PAD, 1), F32),
            pltpu.VMEM((HEAD_PAD, FOX_HEAD_DIM), F32),
            pltpu.VMEM((HEAD_PAD, 1), F32),
        ],
    )
    return pl.pallas_call(
        functools.partial(_fox_decode_kernel, n_pages=n_pages),
        grid_spec=grid_spec,
        out_shape=[jax.ShapeDtypeStruct((batch, HEAD_PAD, FOX_HEAD_DIM), F32), jax.ShapeDtypeStruct((batch, LANES), F32)],
        compiler_params=_params("arbitrary", "arbitrary"),
        name="fox_decode",
    )(page_table, q, k_new, v_new, f_ext, b_pad, *([logf_pool_t] * pps), k_pool, v_pool)


def _mem_attn_kernel(q_ref, k_ref, v_ref, o_ref, *, tq):
    rows = max(tq, SUBLANES)
    scale = MEM_HEAD_DIM ** -0.5
    for h in range(MEM_HEADS):
        sl = slice(h * MEM_HEAD_DIM, (h + 1) * MEM_HEAD_DIM)
        qh = jnp.broadcast_to(q_ref[0, :, sl], (rows, MEM_HEAD_DIM)).astype(BF16)
        kh = k_ref[:, sl].astype(BF16)
        vh = v_ref[:, sl].astype(BF16)
        s = lax.dot_general(qh, kh, (((1,), (1,)), ((), ())), preferred_element_type=F32) * scale
        e = jnp.exp(s - jnp.max(s, axis=-1, keepdims=True))
        oh = jnp.dot(e.astype(BF16), vh, preferred_element_type=F32) / jnp.sum(e, axis=-1, keepdims=True)
        o_ref[0, :, sl] = oh[:tq]


def mem_attn(main3, mem_k, mem_v, *, tq):
    batch, seq, _ = main3.shape
    kv_spec = pl.BlockSpec((N_MEM, MEM_WIDTH), lambda b, i: (b, 0))
    return pl.pallas_call(
        functools.partial(_mem_attn_kernel, tq=tq),
        grid=(batch, seq // tq),
        in_specs=[pl.BlockSpec((1, tq, MEM_WIDTH), lambda b, i: (b, i, QMEM_OFF // MEM_WIDTH)), kv_spec, kv_spec],
        out_specs=pl.BlockSpec((1, tq, MEM_WIDTH), lambda b, i: (b, i, 0)),
        out_shape=jax.ShapeDtypeStruct((batch, seq, MEM_WIDTH), F32),
        compiler_params=_params("parallel", "parallel"),
        name="mem_attn",
    )(main3, mem_k, mem_v)


def _out_proj_kernel(ot_ref, om_ref, gate_ref, x_ref, wt_ref, wm_ref, gp_ref, y_ref):
    gate = gate_ref[...]
    sg = gate * _sigmoid(gate)
    a = (ot_ref[...] * sg[:, :TOK_WIDTH]).astype(BF16)
    b = (om_ref[...] * sg[:, TOK_WIDTH:]).astype(BF16)
    y = jnp.dot(a, wt_ref[...], preferred_element_type=F32) + jnp.dot(b, wm_ref[...], preferred_element_type=F32)
    y_ref[...] = x_ref[...] + y * _rms_scale(y) * gp_ref[...]


def out_proj(o_tok, o_mem, main, x, w_tok, w_mem, g_post, *, tm):
    m = x.shape[0]
    row = lambda width: pl.BlockSpec((tm, width), lambda i: (i, 0))
    const = lambda shape: pl.BlockSpec(shape, lambda i: (0, 0))
    return pl.pallas_call(
        _out_proj_kernel,
        grid=(m // tm,),
        in_specs=[row(TOK_WIDTH), row(MEM_WIDTH), row(D_MODEL), row(D_MODEL),
                  const((TOK_WIDTH, D_MODEL)), const((MEM_WIDTH, D_MODEL)), const((1, D_MODEL))],
        out_specs=row(D_MODEL),
        out_shape=jax.ShapeDtypeStruct((m, D_MODEL), F32),
        compiler_params=_params("parallel"),
        name="out_proj",
    )(o_tok, o_mem, main, x, w_tok, w_mem, g_post.reshape(1, D_MODEL))


PAIR = 2 * RWKV_HEAD_DIM
assert PAIR == LANES
N_PAIRS = RWKV_HEADS // 2


def _pair_ones():
    row = lax.broadcasted_iota(jnp.int32, (PAIR, PAIR), 0) // RWKV_HEAD_DIM
    col = lax.broadcasted_iota(jnp.int32, (PAIR, PAIR), 1) // RWKV_HEAD_DIM
    return jnp.where(row == col, 1.0, 0.0).astype(F32)


def _head_sum(x, ones):
    return _dot_exact_rhs(x, ones)


def _stack(x):
    lane = lax.broadcasted_iota(jnp.int32, x.shape, 1)
    first = lane < RWKV_HEAD_DIM
    return jnp.concatenate([jnp.where(first, x, 0.0), jnp.where(first, 0.0, x)], axis=0)


def _bdot(a, b):
    return jnp.dot(a.astype(BF16), b.astype(BF16), preferred_element_type=F32)


def _wkv_chunk_terms(r, k, v, a_vec, b_vec, lw):
    c = CHUNK
    chunks = [slice(c0, c0 + c) for c0 in range(0, r.shape[0], c)]
    row = lax.broadcasted_iota(jnp.int32, (c, c), 0)
    col = lax.broadcasted_iota(jnp.int32, (c, c), 1)
    tril = jnp.where(col <= row, 1.0, 0.0)
    gi = lax.broadcasted_iota(jnp.int32, (4 * c, 4 * c), 0)
    gj = lax.broadcasted_iota(jnp.int32, (4 * c, 4 * c), 1)
    keep = (gj & (c - 1)) < (gi & (c - 1)) + gi // (2 * c)
    eye = jnp.where(lax.broadcasted_iota(jnp.int32, (2 * c, 2 * c), 0) == lax.broadcasted_iota(jnp.int32, (2 * c, 2 * c), 1),
                    1.0, 0.0).astype(F32)

    cum = [_dot_exact_lhs(tril, lw[s]) for s in chunks]
    last = [x[c - 1:c, :] for x in cum]
    a_t = [_stack(a_vec[s] * jnp.exp(cu - lw[s])) for s, cu in zip(chunks, cum)]
    r_t = [_stack(r[s] * jnp.exp(cu)) for s, cu in zip(chunks, cum)]
    v_s = [_stack(v[s]) for s in chunks]
    xs = [jnp.concatenate([a, rr], axis=0).astype(BF16) for a, rr in zip(a_t, r_t)]
    ys = [jnp.concatenate([_stack(b_vec[s] * jnp.exp(-cu)), _stack(k[s] * jnp.exp(-cu))], axis=0).astype(BF16)
          for s, cu in zip(chunks, cum)]
    g = [jnp.where(keep, lax.dot_general(x, y, (((1,), (1,)), ((), ())), preferred_element_type=F32), 0.0)
         for x, y in zip(xs, ys)]
    l_ab = [x[:2 * c, :2 * c] for x in g]
    l_ak = [x[:2 * c, 2 * c:] for x in g]
    l_rb = [x[2 * c:, :2 * c] for x in g]
    l_rk = [x[2 * c:, 2 * c:] for x in g]
    t_inv = [eye + x for x in l_ab]
    l_pow = l_ab
    for _ in range(int(math.log2(c)) - 1):
        l_pow = [_bdot(x, x) for x in l_pow]
        t_inv = [t + _bdot(p, t) for p, t in zip(l_pow, t_inv)]
    akv = [_bdot(x, y) for x, y in zip(l_ak, v_s)]
    ta = [_bdot(t, jnp.concatenate([a, u], axis=1)) for t, a, u in zip(t_inv, a_t, akv)]
    rbx = [_bdot(x, y) for x, y in zip(l_rb, ta)]
    rkv = [_bdot(x, y) for x, y in zip(l_rk, v_s)]
    r_new = [x + y[:, :PAIR] for x, y in zip(r_t, rbx)]
    y0 = [x[:, PAIR:] + y for x, y in zip(rbx, rkv)]
    e_rem = [jnp.exp(la - cu) for la, cu in zip(last, cum)]
    mn = [_bdot(_stack(b_vec[s] * e).T, t) for s, e, t in zip(chunks, e_rem, ta)]
    kv = [_bdot(_stack(k[s] * e).T, vv) for s, e, vv in zip(chunks, e_rem, v_s)]
    m_c = [x[:, :PAIR] + eye * jnp.exp(la) for x, la in zip(mn, last)]
    n_c = [x[:, PAIR:] + y for x, y in zip(mn, kv)]
    return list(zip(r_new, y0, m_c, n_c))


def _token_shift(cur, prev_row, mu):
    rolled = pltpu.roll(cur, 1, 0)
    first = lax.broadcasted_iota(jnp.int32, cur.shape, 0) == 0
    prev = jnp.where(first, prev_row, rolled)
    return cur + (prev - cur) * mu


def _rwkv_transforms(r, k, v, e, w0, a0, k_k, k_a, ww2, wa2, ones):
    w = -_softplus(-(w0 + jnp.dot(jnp.tanh(e).astype(BF16), ww2, preferred_element_type=F32))) - 0.5
    lw = -jnp.exp(w)
    a = _sigmoid(a0 + jnp.dot(e.astype(BF16), wa2, preferred_element_type=F32))
    kk = k * k_k
    kk = kk / jnp.maximum(jnp.sqrt(_head_sum(kk * kk, ones)), 1e-12)
    k_mod = k * (1.0 + (a - 1.0) * k_a)
    return lw, a, kk, k_mod


def _rwkv_output(y, r, k_mod, v, r_k, ln_w, ln_b, ones):
    mean = _head_sum(y, ones) * (1.0 / RWKV_HEAD_DIM)
    d = y - mean
    var = _head_sum(d * d, ones) * (1.0 / RWKV_HEAD_DIM)
    yn = d * lax.rsqrt(var + GN_EPS) * ln_w + ln_b
    return yn + _head_sum(r * k_mod * r_k, ones) * v


def _wkv_prompt_kernel(r_ref, k_ref, v_ref, e_ref, mur_ref, muk_ref, muv_ref, mue_ref, w0_ref, a0_ref,
                       kk_ref, ka_ref, rk_ref, lnw_ref, lnb_ref, ww2_ref, wa2_ref,
                       o_ref, st_ref, pr_ref, pk_ref, pv_ref, pe_ref, s_ref, y_ref, *, tb):
    t = pl.program_id(2)

    @pl.when(t == 0)
    def _():
        for ref in (pr_ref, pk_ref, pv_ref, pe_ref, s_ref):
            ref[...] = jnp.zeros_like(ref)

    ones = _pair_ones()
    r = _token_shift(r_ref[...], pr_ref[...], mur_ref[...])
    k = _token_shift(k_ref[...], pk_ref[...], muk_ref[...])
    v = _token_shift(v_ref[...], pv_ref[...], muv_ref[...])
    e = _token_shift(e_ref[...], pe_ref[...], mue_ref[...])
    for cur, prev in ((r_ref, pr_ref), (k_ref, pk_ref), (v_ref, pv_ref), (e_ref, pe_ref)):
        prev[...] = cur[tb - 1:tb, :]
    lw, a, kk, k_mod = _rwkv_transforms(r, k, v, e, w0_ref[...], a0_ref[...], kk_ref[...], ka_ref[...],
                                        ww2_ref[...], wa2_ref[...], ones)
    terms = _wkv_chunk_terms(r, k_mod, v, -kk, kk * a, lw)
    s = s_ref[...]
    for i, (r_new, y0, m_c, n_c) in enumerate(terms):
        ys = _bdot(r_new, s) + y0
        y_ref[i * CHUNK:(i + 1) * CHUNK, :] = ys[:CHUNK] + ys[CHUNK:]
        s = _bdot(m_c, s) + n_c
    s_ref[...] = s
    o_ref[...] = _rwkv_output(y_ref[...], r, k_mod, v, rk_ref[...], lnw_ref[...], lnb_ref[...], ones)

    @pl.when(t == pl.num_programs(2) - 1)
    def _():
        st_ref[0, 0] = s


def wkv_prompt(main, ext, mu_rkv, mu_e, w0, a0, k_k, k_a, r_k, ln_w, ln_b, ww2, wa2, *, batch, seq, tb):
    nt = seq // tb
    rb, kb, vb = TOKA_OFF // LANES, TOKB_OFF // LANES, TOKC_OFF // LANES
    rows = lambda off: pl.BlockSpec((tb, LANES), lambda b, p, t: (b * nt + t, off + p))
    vec = lambda off: pl.BlockSpec((1, LANES), lambda b, p, t: (0, off + p))
    lora = pl.BlockSpec((RWKV_EXTRA, LANES), lambda b, p, t: (0, p))
    return pl.pallas_call(
        functools.partial(_wkv_prompt_kernel, tb=tb),
        grid=(batch, N_PAIRS, nt),
        in_specs=[
            rows(rb), rows(kb), rows(vb),
            pl.BlockSpec((tb, RWKV_EXTRA), lambda b, p, t: (b * nt + t, 0)),
            vec(0), vec(N_PAIRS), vec(2 * N_PAIRS),
            pl.BlockSpec((1, RWKV_EXTRA), lambda b, p, t: (0, 0)),
            vec(0), vec(0), vec(0), vec(0), vec(0), vec(0), vec(0), lora, lora,
        ],
        out_specs=[
            pl.BlockSpec((tb, LANES), lambda b, p, t: (b * nt + t, p)),
            pl.BlockSpec((1, 1, PAIR, PAIR), lambda b, p, t: (b, p, 0, 0)),
        ],
        out_shape=[
            jax.ShapeDtypeStruct((batch * seq, TOK_WIDTH), F32),
            jax.ShapeDtypeStruct((batch, N_PAIRS, PAIR, PAIR), F32),
        ],
        scratch_shapes=[
            pltpu.VMEM((1, LANES), F32), pltpu.VMEM((1, LANES), F32), pltpu.VMEM((1, LANES), F32),
            pltpu.VMEM((1, RWKV_EXTRA), F32), pltpu.VMEM((PAIR, PAIR), F32), pltpu.VMEM((tb, LANES), F32),
        ],
        compiler_params=_params("parallel", "parallel", "arbitrary"),
        name="wkv_prompt",
    )(main, main, main, ext, mu_rkv, mu_rkv, mu_rkv, mu_e, w0, a0, k_k, k_a, r_k, ln_w, ln_b, ww2, wa2)


def _wkv_step_kernel(cur_ref, prev_ref, e_ref, pe_ref, mu_ref, mue_ref, w0_ref, a0_ref, kk_ref, ka_ref, rk_ref,
                     lnw_ref, lnb_ref, ww2_ref, wa2_ref, st_ref, o_ref, stn_ref):
    hd = RWKV_HEAD_DIM
    e_cur = e_ref[0]
    e = e_cur + (pe_ref[0] - e_cur) * mue_ref[...]
    e_rows = jnp.broadcast_to(e, (SUBLANES, RWKV_EXTRA))
    e_tanh = jnp.tanh(e_rows).astype(BF16)
    e_lin = e_rows.astype(BF16)
    eye = lax.broadcasted_iota(jnp.int32, (hd, hd), 0) == lax.broadcasted_iota(jnp.int32, (hd, hd), 1)

    def mixed(i, h):
        cur = cur_ref[0, i, h:h + 1, :]
        return cur + (prev_ref[0, i, h:h + 1, :] - cur) * mu_ref[i, h:h + 1, :]

    for h in range(RWKV_HEADS):
        row = lambda ref: ref[h:h + 1, :]
        r, k, v = mixed(0, h), mixed(1, h), mixed(2, h)
        w = -_softplus(-(row(w0_ref) + jnp.dot(e_tanh, ww2_ref[h], preferred_element_type=F32)[0:1])) - 0.5
        decay = jnp.exp(-jnp.exp(w))
        a = _sigmoid(row(a0_ref) + jnp.dot(e_lin, wa2_ref[h], preferred_element_type=F32)[0:1])
        kk = k * row(kk_ref)
        kk = kk / jnp.maximum(jnp.sqrt(jnp.sum(kk * kk, axis=-1, keepdims=True)), 1e-12)
        k_mod = k * (1.0 + (a - 1.0) * row(ka_ref))
        s = st_ref[0, h]
        sa = jnp.sum(s * (-kk), axis=-1, keepdims=True)
        v_col = jnp.sum(jnp.where(eye, jnp.broadcast_to(v, (hd, hd)), 0.0), axis=-1, keepdims=True)
        s_new = s * decay + sa * (kk * a) + v_col * k_mod
        stn_ref[0, h] = s_new
        y_col = jnp.sum(s_new * r, axis=-1, keepdims=True)
        y = jnp.sum(jnp.where(eye, jnp.broadcast_to(y_col, (hd, hd)), 0.0), axis=0, keepdims=True)
        mean = jnp.mean(y, axis=-1, keepdims=True)
        d = y - mean
        var = jnp.mean(d * d, axis=-1, keepdims=True)
        yn = d * lax.rsqrt(var + GN_EPS) * row(lnw_ref) + row(lnb_ref)
        o_ref[0, h:h + 1, :] = yn + jnp.sum(r * k_mod * row(rk_ref), axis=-1, keepdims=True) * v


def wkv_step(cur, prev, e_cur, e_prev, mu, mu_e, w0, a0, k_k, k_a, r_k, ln_w, ln_b, ww2h, wa2h, state):
    batch = cur.shape[0]
    hshape = (RWKV_HEADS, RWKV_HEAD_DIM)
    per_row = lambda shape: pl.BlockSpec((1,) + shape, lambda b: (b,) + (0,) * len(shape))
    const = lambda shape: pl.BlockSpec(shape, lambda b: (0,) * len(shape))
    return pl.pallas_call(
        _wkv_step_kernel,
        grid=(batch,),
        in_specs=[per_row((3,) + hshape), per_row((3,) + hshape), per_row((1, RWKV_EXTRA)), per_row((1, RWKV_EXTRA)),
                  const((3,) + hshape), const((1, RWKV_EXTRA))] + [const(hshape)] * 7
                 + [const((RWKV_HEADS, RWKV_EXTRA, RWKV_HEAD_DIM))] * 2 + [per_row(hshape + (RWKV_HEAD_DIM,))],
        out_specs=[per_row(hshape), per_row(hshape + (RWKV_HEAD_DIM,))],
        out_shape=[jax.ShapeDtypeStruct((batch,) + hshape, F32),
                   jax.ShapeDtypeStruct((batch,) + hshape + (RWKV_HEAD_DIM,), F32)],
        compiler_params=_params("parallel"),
        name="wkv_step",
    )(cur, prev, e_cur, e_prev, mu, mu_e, w0, a0, k_k, k_a, r_k, ln_w, ln_b, ww2h, wa2h, state)


def _relayout_in_proj(w_in, tok_cols, extra_cols, extra_width):
    tok = w_in[:, :tok_cols]
    extra = w_in[:, tok_cols:tok_cols + extra_cols]
    q_mem = w_in[:, tok_cols + extra_cols:tok_cols + extra_cols + MEM_WIDTH]
    gate = w_in[:, tok_cols + extra_cols + MEM_WIDTH:]
    main = jnp.concatenate([gate, tok, q_mem], axis=1).astype(BF16)
    extra = jnp.pad(extra, ((0, 0), (0, extra_width - extra_cols))).astype(BF16)
    return main, extra


def _pad_lanes(v, width):
    return jnp.pad(v, (0, width - v.shape[0])).reshape(1, width)


def _split_w_o(w_o):
    return w_o[:TOK_WIDTH].astype(BF16), w_o[TOK_WIDTH:].astype(BF16)


def _memory_kv(mem, g_mem, w_mem):
    kv = norm_proj(mem, g_mem, w_mem.astype(BF16), tm=mem.shape[0], tn=MEM_WIDTH)
    return kv[:, :MEM_WIDTH], kv[:, MEM_WIDTH:]


def kernel(x_prompt, x_sample, cache_fox_k, cache_fox_v, cache_fox_logf, cache_mem_k, cache_mem_v, state_rwkv_wkv, state_rwkv_shift, page_table, mem_prompt, g_pre, g_post, g_mem, w_mem, w_o, fox_w_in, fox_b_f, rwkv_w_in, rwkv_mu, rwkv_w0, rwkv_w_w2, rwkv_a0, rwkv_w_a2, rwkv_k_k, rwkv_k_a, rwkv_r_k, rwkv_ln_w, rwkv_ln_b):
    bp, seq, _ = x_prompt.shape
    bs = x_sample.shape[0]
    xp = x_prompt.reshape(bp * seq, D_MODEL)
    xs = x_sample.reshape(bs, D_MODEL)
    mem = mem_prompt.reshape(bp * N_MEM, D_MODEL)
    tm_p, tn = 1024, 512
    mem_shape = (bp, N_MEM, MEM_HEADS, MEM_HEAD_DIM)

    w_main, w_ext = _relayout_in_proj(fox_w_in[0], 3 * TOK_WIDTH, FOX_HEADS, FOX_EXTRA)
    w_tok, w_mo = _split_w_o(w_o[0])
    b_pad = _pad_lanes(fox_b_f[0], LANES)
    mk0, mv0 = _memory_kv(mem, g_mem[0], w_mem[0])

    main_p, ext_p = norm_proj(xp, g_pre[0], w_main, w_ext, tm=tm_p, tn=tn)
    logf_p, ccol, crow = fox_gate(ext_p, b_pad, batch=bp, seq=seq, tb=256)
    o_tok = fox_flash(main_p, ccol, crow, batch=bp, seq=seq, tq=512)
    o_mem = mem_attn(main_p.reshape(bp, seq, MAIN_WIDTH), mk0, mv0, tq=1024)
    xp = out_proj(o_tok, o_mem.reshape(bp * seq, MEM_WIDTH), main_p, xp, w_tok, w_mo, g_post[0], tm=256)

    main_s, ext_s = norm_proj(xs, g_pre[0], w_main, w_ext, tm=bs, tn=tn)
    q_s, k_s, v_s = (main_s[:, off:off + TOK_WIDTH] for off in (TOKA_OFF, TOKB_OFF, TOKC_OFF))
    logf_pool_t = jnp.pad(jnp.transpose(cache_fox_logf[0], (0, 2, 1)), ((0, 0), (0, HEAD_PAD - FOX_HEADS), (0, 0)))
    pad_heads = lambda a: jnp.pad(a.reshape(bs, FOX_HEADS, FOX_HEAD_DIM), ((0, 0), (0, HEAD_PAD - FOX_HEADS), (0, 0)))
    o_heads_s, logf_s = fox_decode(page_table, pad_heads(q_s), pad_heads(k_s), pad_heads(v_s), ext_s, b_pad,
                                   cache_fox_k[0], cache_fox_v[0], logf_pool_t)
    o_tok_s = o_heads_s[:, :FOX_HEADS, :].reshape(bs, TOK_WIDTH)
    o_mem_s = mem_attn(main_s.reshape(bs, 1, MAIN_WIDTH), cache_mem_k[0].reshape(bs * N_MEM, MEM_WIDTH),
                       cache_mem_v[0].reshape(bs * N_MEM, MEM_WIDTH), tq=1)
    xs = out_proj(o_tok_s, o_mem_s.reshape(bs, MEM_WIDTH), main_s, xs, w_tok, w_mo, g_post[0], tm=bs)

    head_shape = (FOX_HEADS, FOX_HEAD_DIM)
    fox_k_p = main_p[:, TOKB_OFF:TOKC_OFF].reshape((1, bp, seq) + head_shape)
    fox_v_p = main_p[:, TOKC_OFF:QMEM_OFF].reshape((1, bp, seq) + head_shape)
    fox_f_p = logf_p[:, :FOX_HEADS].reshape(1, bp, seq, FOX_HEADS)
    fox_k_s = k_s.reshape((1, bs, 1) + head_shape)
    fox_v_s = v_s.reshape((1, bs, 1) + head_shape)
    fox_f_s = logf_s[:, :FOX_HEADS].reshape(1, bs, 1, FOX_HEADS)

    w_main, w_ext = _relayout_in_proj(rwkv_w_in[0], 3 * TOK_WIDTH, 2 * LORA, RWKV_EXTRA)
    w_tok, w_mo = _split_w_o(w_o[1])
    mk1, mv1 = _memory_kv(mem, g_mem[1], w_mem[1])
    mu_rkv = rwkv_mu[0][:3 * TOK_WIDTH].reshape(1, 3 * TOK_WIDTH)
    mu_e = _pad_lanes(rwkv_mu[0][3 * TOK_WIDTH:], RWKV_EXTRA)
    ww2 = jnp.pad(rwkv_w_w2[0], ((0, RWKV_EXTRA - LORA), (0, 0))).astype(BF16)
    wa2 = jnp.pad(rwkv_w_a2[0], ((LORA, RWKV_EXTRA - 2 * LORA), (0, 0))).astype(BF16)
    vecs = [p[0] for p in (rwkv_w0, rwkv_a0, rwkv_k_k, rwkv_k_a, rwkv_r_k, rwkv_ln_w, rwkv_ln_b)]

    main_p, ext_p = norm_proj(xp, g_pre[1], w_main, w_ext, tm=tm_p, tn=tn)
    o_tok, st_pairs = wkv_prompt(main_p, ext_p, mu_rkv, mu_e, *[v.reshape(1, TOK_WIDTH) for v in vecs], ww2, wa2,
                                 batch=bp, seq=seq, tb=512)
    o_mem = mem_attn(main_p.reshape(bp, seq, MAIN_WIDTH), mk1, mv1, tq=1024)
    xp = out_proj(o_tok, o_mem.reshape(bp * seq, MEM_WIDTH), main_p, xp, w_tok, w_mo, g_post[1], tm=256)

    main_s, ext_s = norm_proj(xs, g_pre[1], w_main, w_ext, tm=bs, tn=tn)
    hshape = (RWKV_HEADS, RWKV_HEAD_DIM)
    shift_prev = state_rwkv_shift[0][:, 0, :]
    cur = main_s[:, TOKA_OFF:QMEM_OFF].reshape((bs, 3) + hshape)
    prev = shift_prev[:, :3 * TOK_WIDTH].reshape((bs, 3) + hshape)
    e_prev = jnp.pad(shift_prev[:, 3 * TOK_WIDTH:], ((0, 0), (0, RWKV_EXTRA - 2 * LORA))).reshape(bs, 1, RWKV_EXTRA)
    per_head = lambda w: jnp.transpose(w.reshape((RWKV_EXTRA,) + hshape), (1, 0, 2))
    o_tok_s, wkv_s = wkv_step(cur, prev, ext_s.reshape(bs, 1, RWKV_EXTRA), e_prev, mu_rkv.reshape((3,) + hshape), mu_e,
                              *[v.reshape(hshape) for v in vecs], per_head(ww2), per_head(wa2), state_rwkv_wkv[0])
    o_mem_s = mem_attn(main_s.reshape(bs, 1, MAIN_WIDTH), cache_mem_k[1].reshape(bs * N_MEM, MEM_WIDTH),
                       cache_mem_v[1].reshape(bs * N_MEM, MEM_WIDTH), tq=1)
    xs = out_proj(o_tok_s.reshape(bs, TOK_WIDTH), o_mem_s.reshape(bs, MEM_WIDTH), main_s, xs, w_tok, w_mo, g_post[1], tm=bs)

    st6 = st_pairs.reshape(bp, N_PAIRS, 2, RWKV_HEAD_DIM, 2, RWKV_HEAD_DIM)
    st_heads = jnp.stack([st6[:, :, 0, :, 0, :], st6[:, :, 1, :, 1, :]], axis=2)
    wkv_p = jnp.swapaxes(st_heads.reshape((bp,) + hshape + (RWKV_HEAD_DIM,)), -1, -2)
    last_p = main_p.reshape(bp, seq, MAIN_WIDTH)[:, seq - 1, TOKA_OFF:QMEM_OFF]
    last_e = ext_p.reshape(bp, seq, RWKV_EXTRA)[:, seq - 1, :2 * LORA]
    shift_p = jnp.concatenate([last_p, last_e], axis=-1).reshape(1, bp, 1, 3 * TOK_WIDTH + 2 * LORA)
    shift_s = jnp.concatenate([main_s[:, TOKA_OFF:QMEM_OFF], ext_s[:, :2 * LORA]], axis=-1).reshape(1, bs, 1, 3 * TOK_WIDTH + 2 * LORA)

    return (xp.reshape(bp, seq, D_MODEL), xs.reshape(bs, 1, D_MODEL),
            jnp.stack([mk0.reshape(mem_shape), mk1.reshape(mem_shape)]),
            jnp.stack([mv0.reshape(mem_shape), mv1.reshape(mem_shape)]),
            fox_k_p, fox_v_p, fox_f_p, fox_k_s, fox_v_s, fox_f_s,
            wkv_p[None], shift_p, wkv_s[None], shift_s)
```

```python
import functools
import math

import jax
import jax.numpy as jnp
from jax import lax
from jax.experimental import pallas as pl
from jax.experimental.pallas import tpu as pltpu

F32 = jnp.float32
BF16 = jnp.bfloat16

D_MODEL = 2048
N_MEM = 256
MEM_HEADS = 4
MEM_HEAD_DIM = 128
MEM_WIDTH = MEM_HEADS * MEM_HEAD_DIM
TOK_WIDTH = D_MODEL - MEM_WIDTH
FOX_HEAD_DIM = 128
FOX_HEADS = TOK_WIDTH // FOX_HEAD_DIM
RWKV_HEAD_DIM = 64
RWKV_HEADS = TOK_WIDTH // RWKV_HEAD_DIM
LORA = 96
PAGE_SIZE = 128
RMS_EPS = 1e-6
GN_EPS = 64e-5

LANES = 128
SUBLANES = 8
V7X_VMEM_BYTES = 64 * 1024 * 1024
VMEM_LIMIT_BYTES = V7X_VMEM_BYTES - 8 * 1024 * 1024

GATE_OFF = 0
TOKA_OFF = D_MODEL
TOKB_OFF = TOKA_OFF + TOK_WIDTH
TOKC_OFF = TOKB_OFF + TOK_WIDTH
QMEM_OFF = TOKC_OFF + TOK_WIDTH
MAIN_WIDTH = QMEM_OFF + MEM_WIDTH
FOX_EXTRA = LANES
RWKV_EXTRA = 2 * LANES

HEAD_PAD = 16
CHUNK = 64


def _params(*sem):
    return pltpu.CompilerParams(dimension_semantics=sem, vmem_limit_bytes=VMEM_LIMIT_BYTES)


def _rms_scale(x):
    return lax.rsqrt(jnp.mean(x * x, axis=-1, keepdims=True) + RMS_EPS)


def _sigmoid(x):
    return 1.0 / (1.0 + jnp.exp(-x))


def _softplus(x):
    return jnp.maximum(x, 0.0) + jnp.log(1.0 + jnp.exp(-jnp.abs(x)))


def _split3(x):
    hi = x.astype(BF16)
    r1 = x - hi.astype(F32)
    mid = r1.astype(BF16)
    lo = (r1 - mid.astype(F32)).astype(BF16)
    return hi, mid, lo


def _dot_exact_rhs(x, rhs):
    rows = x.shape[0]
    parts = jnp.dot(jnp.concatenate(_split3(x), axis=0), rhs.astype(BF16), preferred_element_type=F32)
    return parts[:rows] + parts[rows:2 * rows] + parts[2 * rows:]


def _dot_exact_lhs(lhs, x):
    cols = x.shape[1]
    parts = jnp.dot(lhs.astype(BF16), jnp.concatenate(_split3(x), axis=1), preferred_element_type=F32)
    return parts[:, :cols] + parts[:, cols:2 * cols] + parts[:, 2 * cols:]


TN = 4 * LANES
GATE_BLOCKS = D_MODEL // TN
TOK_BLOCKS = TOK_WIDTH // TN
QMEM_BLOCK = GATE_BLOCKS + 3 * TOK_BLOCKS
GM_WIDTH = D_MODEL + MEM_WIDTH
N_GROUPS128 = TOK_WIDTH // LANES
assert MAIN_WIDTH == (QMEM_BLOCK + 1) * TN and MEM_WIDTH == TN


def _nt_dot(a, b):
    return lax.dot_general(a, b, (((1,), (1,)), ((), ())), preferred_element_type=F32)


def _norm_proj_kernel(x_ref, g_ref, w_ref, *rest, has_extra, head_mode):
    rest = list(rest)
    we_ref = rest.pop(0) if has_extra else None
    h_ref = rest.pop()
    oe_ref = rest.pop() if has_extra else None
    j = pl.program_id(1)

    @pl.when(j == 0)
    def _():
        x = x_ref[...]
        h = (x * _rms_scale(x) * g_ref[...]).astype(BF16)
        h_ref[...] = h
        if has_extra:
            oe_ref[...] = _nt_dot(h, we_ref[...])

    res = _nt_dot(h_ref[...], w_ref[...])
    if not head_mode:
        rest[0][...] = res
        return
    gm_ref, groups = rest[0], rest[1:]

    @pl.when((j < GATE_BLOCKS) | (j == QMEM_BLOCK))
    def _():
        gm_ref[...] = res

    for gi, ref in enumerate(groups):
        first = GATE_BLOCKS + gi * TOK_BLOCKS

        @pl.when((j >= first) & (j < first + TOK_BLOCKS))
        def _(ref=ref):
            for hh in range(TN // LANES):
                ref[0, hh] = res[:, hh * LANES:(hh + 1) * LANES]


def norm_proj(x, g, w_t, w_extra_t=None, *, tm, heads=None):
    m, d = x.shape
    n = w_t.shape[0]
    assert m % tm == 0 and n % TN == 0
    has_extra = w_extra_t is not None
    in_specs = [
        pl.BlockSpec((tm, d), lambda i, j: (i, 0)),
        pl.BlockSpec((1, d), lambda i, j: (0, 0)),
        pl.BlockSpec((TN, d), lambda i, j: (j, 0)),
    ]
    args = [x, g.reshape(1, d), w_t]
    if has_extra:
        in_specs.append(pl.BlockSpec(w_extra_t.shape, lambda i, j: (0, 0)))
        args.append(w_extra_t)
    if heads is None:
        out_specs = [pl.BlockSpec((tm, TN), lambda i, j: (i, j))]
        out_shape = [jax.ShapeDtypeStruct((m, n), F32)]
    else:
        batch, seq = heads
        assert n == MAIN_WIDTH and seq % tm == 0
        per_seq = seq // tm
        gm_col = lambda j: jnp.where(j == QMEM_BLOCK, GATE_BLOCKS, jnp.minimum(j, GATE_BLOCKS - 1))
        out_specs = [pl.BlockSpec((tm, TN), lambda i, j: (i, gm_col(j)))]
        out_shape = [jax.ShapeDtypeStruct((m, GM_WIDTH), F32)]
        for gi in range(3):
            first = GATE_BLOCKS + gi * TOK_BLOCKS
            out_specs.append(pl.BlockSpec(
                (1, TN // LANES, tm, LANES),
                lambda i, j, first=first: (i // per_seq, jnp.clip(j - first, 0, TOK_BLOCKS - 1), i % per_seq, 0)))
            out_shape.append(jax.ShapeDtypeStruct((batch, N_GROUPS128, seq, LANES), F32))
    if has_extra:
        e = w_extra_t.shape[0]
        out_specs.append(pl.BlockSpec((tm, e), lambda i, j: (i, 0)))
        out_shape.append(jax.ShapeDtypeStruct((m, e), F32))
    outs = pl.pallas_call(
        functools.partial(_norm_proj_kernel, has_extra=has_extra, head_mode=heads is not None),
        grid=(m // tm, n // TN),
        in_specs=in_specs,
        out_specs=out_specs,
        out_shape=out_shape,
        scratch_shapes=[pltpu.VMEM((tm, d), BF16)],
        compiler_params=_params("parallel", "arbitrary"),
        name="norm_proj",
    )(*args)
    return outs if len(outs) > 1 else outs[0]


def _fox_gate_kernel(f_ref, b_ref, logf_ref, ccol_ref, crow_ref, carry_ref, *, tb):
    @pl.when(pl.program_id(1) == 0)
    def _():
        carry_ref[...] = jnp.zeros_like(carry_ref)

    logf = -_softplus(-(f_ref[...] + b_ref[...]))
    logf_ref[...] = logf
    row = lax.broadcasted_iota(jnp.int32, (tb, tb), 0)
    col = lax.broadcasted_iota(jnp.int32, (tb, tb), 1)
    c = _dot_exact_lhs(jnp.where(col <= row, 1.0, 0.0), logf) + carry_ref[...]
    ccol_ref[...] = c
    crow_ref[0] = c.T[:HEAD_PAD]
    carry_ref[...] = c[tb - 1:tb, :]


def fox_gate(f_ext, b_pad, *, batch, seq, tb):
    nt = seq // tb
    row_spec = pl.BlockSpec((tb, LANES), lambda b, t: (b * nt + t, 0))
    return pl.pallas_call(
        functools.partial(_fox_gate_kernel, tb=tb),
        grid=(batch, nt),
        in_specs=[row_spec, pl.BlockSpec((1, LANES), lambda b, t: (0, 0))],
        out_specs=[row_spec, row_spec, pl.BlockSpec((1, HEAD_PAD, tb), lambda b, t: (b, 0, t))],
        out_shape=[
            jax.ShapeDtypeStruct((batch * seq, LANES), F32),
            jax.ShapeDtypeStruct((batch * seq, LANES), F32),
            jax.ShapeDtypeStruct((batch, HEAD_PAD, seq), F32),
        ],
        scratch_shapes=[pltpu.VMEM((1, LANES), F32)],
        compiler_params=_params("parallel", "arbitrary"),
        name="fox_gate",
    )(f_ext, b_pad)


_MASKED = -1e30


def _fox_flash_kernel(q_ref, k_ref, v_ref, ccol_ref, crow_ref, o_ref, kb_ref, vb_ref, *, tq):
    h = pl.program_id(1)
    qi = pl.program_id(2)

    @pl.when(qi == 0)
    def _():
        kb_ref[...] = k_ref[0, 0].astype(BF16)
        vb_ref[...] = v_ref[0, 0].astype(BF16)

    q = q_ref[0, 0].astype(BF16)
    lane = lax.broadcasted_iota(jnp.int32, (tq, LANES), 1)
    cq = jnp.sum(jnp.where(lane == h, ccol_ref[...], 0.0), axis=-1, keepdims=True)
    scale = FOX_HEAD_DIM ** -0.5

    def step(c, carry, diagonal):
        m, l, acc = carry
        start = pl.multiple_of(c * tq, tq)
        kc = kb_ref[pl.ds(start, tq), :]
        vc = vb_ref[pl.ds(start, tq), :]
        ck = crow_ref[0, 0, :, pl.ds(start, tq)]
        s = lax.dot_general(q, kc, (((1,), (1,)), ((), ())), preferred_element_type=F32)
        s = s * scale + (cq - ck)
        if diagonal:
            row = lax.broadcasted_iota(jnp.int32, (tq, tq), 0)
            col = lax.broadcasted_iota(jnp.int32, (tq, tq), 1)
            s = jnp.where(col <= row, s, _MASKED)
        m_new = jnp.maximum(m, jnp.max(s, axis=-1, keepdims=True))
        p = jnp.exp(s - m_new)
        alpha = jnp.exp(m - m_new)
        l = alpha * l + jnp.sum(p, axis=-1, keepdims=True)
        acc = alpha * acc + jnp.dot(p.astype(BF16), vc, preferred_element_type=F32)
        return m_new, l, acc

    init = (jnp.full((tq, 1), _MASKED, F32), jnp.zeros((tq, 1), F32), jnp.zeros((tq, FOX_HEAD_DIM), F32))
    carry = lax.fori_loop(0, qi, lambda c, cr: step(c, cr, False), init)
    _, l, acc = step(qi, carry, True)
    o_ref[...] = acc / l


def fox_flash(q, k, v, ccol, crow, *, tq):
    batch, _, seq, _ = q.shape
    nq = seq // tq
    crow4 = crow.reshape(batch, HEAD_PAD, 1, seq)
    kv_spec = pl.BlockSpec((1, 1, seq, FOX_HEAD_DIM), lambda b, h, i: (b, h, 0, 0))
    return pl.pallas_call(
        functools.partial(_fox_flash_kernel, tq=tq),
        grid=(batch, FOX_HEADS, nq),
        in_specs=[
            pl.BlockSpec((1, 1, tq, FOX_HEAD_DIM), lambda b, h, i: (b, h, i, 0)),
            kv_spec, kv_spec,
            pl.BlockSpec((tq, LANES), lambda b, h, i: (b * nq + i, 0)),
            pl.BlockSpec((1, 1, 1, seq), lambda b, h, i: (b, h, 0, 0)),
        ],
        out_specs=pl.BlockSpec((tq, LANES), lambda b, h, i: (b * nq + i, h)),
        out_shape=jax.ShapeDtypeStruct((batch * seq, TOK_WIDTH), F32),
        scratch_shapes=[pltpu.VMEM((seq, LANES), BF16), pltpu.VMEM((seq, LANES), BF16)],
        compiler_params=_params("parallel", "parallel", "arbitrary"),
        name="fox_flash",
    )(q, k, v, ccol, crow4)


PAGES_PER_STEP = 4


def _head_mask():
    row = lax.broadcasted_iota(jnp.int32, (HEAD_PAD, TOK_WIDTH), 0)
    col = lax.broadcasted_iota(jnp.int32, (HEAD_PAD, TOK_WIDTH), 1)
    lo = row * FOX_HEAD_DIM
    return (col >= lo) & (col < lo + FOX_HEAD_DIM)


def _fox_decode_kernel(pt_ref, q_ref, kn_ref, vn_ref, f_ref, bf_ref, *rest):
    del pt_ref
    pps = PAGES_PER_STEP
    lf_refs, k_refs, v_refs = rest[:pps], rest[pps:2 * pps], rest[2 * pps:3 * pps]
    o_ref, lfn_ref, qbd_ref, m_ref, l_ref, acc_ref, carry_ref = rest[3 * pps:]
    b = pl.program_id(0)
    j = pl.program_id(1)
    scale = FOX_HEAD_DIM ** -0.5
    hmask = _head_mask()

    @pl.when(j == 0)
    def _():
        q = q_ref[pl.ds(b, 1), :]
        q_bd = jnp.where(hmask, jnp.broadcast_to(q, (HEAD_PAD, TOK_WIDTH)), 0.0)
        qbd_ref[...] = q_bd.astype(BF16)
        logf_new = -_softplus(-(f_ref[pl.ds(b, 1), :] + bf_ref[...]))
        lfn_ref[pl.ds(b, 1), :] = logf_new
        eye = lax.broadcasted_iota(jnp.int32, (HEAD_PAD, LANES), 0) == lax.broadcasted_iota(jnp.int32, (HEAD_PAD, LANES), 1)
        carry_ref[...] = jnp.sum(jnp.where(eye, jnp.broadcast_to(logf_new, (HEAD_PAD, LANES)), 0.0), axis=-1, keepdims=True)
        m_ref[...] = jnp.sum(q_bd * kn_ref[pl.ds(b, 1), :], axis=-1, keepdims=True) * scale
        l_ref[...] = jnp.ones_like(l_ref)
        acc_ref[...] = jnp.broadcast_to(vn_ref[pl.ds(b, 1), :], (HEAD_PAD, TOK_WIDTH))

    def page_rows(ref):
        return jnp.concatenate([ref[0, 0, h].astype(BF16) for h in range(FOX_HEADS)], axis=1)

    lfs = [ref[0] for ref in lf_refs]
    src = lax.broadcasted_iota(jnp.int32, (PAGE_SIZE, PAGE_SIZE), 0)
    dst = lax.broadcasted_iota(jnp.int32, (PAGE_SIZE, PAGE_SIZE), 1)
    inside = _dot_exact_rhs(jnp.concatenate(lfs, axis=0), jnp.where(src > dst, 1.0, 0.0))
    carries = [carry_ref[...]]
    for lf in lfs:
        carries.append(carries[-1] + jnp.sum(lf, axis=-1, keepdims=True))
    carry_ref[...] = carries[-1]

    q_bd = qbd_ref[...]
    scores = []
    for i in range(pps):
        s = lax.dot_general(q_bd, page_rows(k_refs[i]), (((1,), (1,)), ((), ())), preferred_element_type=F32)
        scores.append(s * scale + (inside[i * HEAD_PAD:(i + 1) * HEAD_PAD] + carries[i]))
    m_old = m_ref[...]
    m_new = m_old
    for s in scores:
        m_new = jnp.maximum(m_new, jnp.max(s, axis=-1, keepdims=True))
    alpha = jnp.exp(m_old - m_new)
    l_new = alpha * l_ref[...]
    acc = alpha * acc_ref[...]
    for i, s in enumerate(scores):
        p = jnp.exp(s - m_new)
        l_new = l_new + jnp.sum(p, axis=-1, keepdims=True)
        acc = acc + jnp.dot(p.astype(BF16), page_rows(v_refs[i]), preferred_element_type=F32)
    m_ref[...] = m_new
    l_ref[...] = l_new
    acc_ref[...] = acc

    @pl.when(j == pl.num_programs(1) - 1)
    def _():
        out = jnp.where(hmask, acc / l_new, 0.0)
        o_ref[pl.ds(b, 1), :] = jnp.sum(out, axis=0, keepdims=True)


def fox_decode(page_table, q, k_new, v_new, f_ext, b_pad, k_pool, v_pool, logf_pool_t, *, layer):
    batch, n_pages = page_table.shape
    pps = PAGES_PER_STEP
    assert n_pages % pps == 0
    whole = lambda shape: pl.BlockSpec(shape, lambda b, j, pt: (0,) * len(shape))
    page_id = lambda i: (lambda b, j, pt: pt[b, n_pages - 1 - j * pps - i])
    lf_spec = lambda i: pl.BlockSpec((1, HEAD_PAD, PAGE_SIZE), lambda b, j, pt: (page_id(i)(b, j, pt), 0, 0))
    kv_spec = lambda i: pl.BlockSpec((1, 1, FOX_HEADS, PAGE_SIZE, FOX_HEAD_DIM),
                                     lambda b, j, pt: (layer, page_id(i)(b, j, pt), 0, 0, 0))
    grid_spec = pltpu.PrefetchScalarGridSpec(
        num_scalar_prefetch=1,
        grid=(batch, n_pages // pps),
        in_specs=[whole((batch, TOK_WIDTH))] * 3 + [whole((batch, LANES)), whole((1, LANES))]
                 + [lf_spec(i) for i in range(pps)] + [kv_spec(i) for i in range(pps)] * 2,
        out_specs=[whole((batch, TOK_WIDTH)), whole((batch, LANES))],
        scratch_shapes=[
            pltpu.VMEM((HEAD_PAD, TOK_WIDTH), BF16),
            pltpu.VMEM((HEAD_PAD, 1), F32),
            pltpu.VMEM((HEAD_PAD, 1), F32),
            pltpu.VMEM((HEAD_PAD, TOK_WIDTH), F32),
            pltpu.VMEM((HEAD_PAD, 1), F32),
        ],
    )
    return pl.pallas_call(
        _fox_decode_kernel,
        grid_spec=grid_spec,
        out_shape=[jax.ShapeDtypeStruct((batch, TOK_WIDTH), F32), jax.ShapeDtypeStruct((batch, LANES), F32)],
        compiler_params=_params("arbitrary", "arbitrary"),
        name="fox_decode",
    )(page_table, q, k_new, v_new, f_ext, b_pad, *([logf_pool_t] * pps), *([k_pool] * pps), *([v_pool] * pps))


def _mem_attn_kernel(q_ref, k_ref, v_ref, o_ref, *, tq):
    rows = max(tq, SUBLANES)
    scale = MEM_HEAD_DIM ** -0.5
    for h in range(MEM_HEADS):
        sl = slice(h * MEM_HEAD_DIM, (h + 1) * MEM_HEAD_DIM)
        qh = jnp.broadcast_to(q_ref[0, :, sl], (rows, MEM_HEAD_DIM)).astype(BF16)
        kh = k_ref[:, sl].astype(BF16)
        vh = v_ref[:, sl].astype(BF16)
        s = lax.dot_general(qh, kh, (((1,), (1,)), ((), ())), preferred_element_type=F32) * scale
        e = jnp.exp(s - jnp.max(s, axis=-1, keepdims=True))
        oh = jnp.dot(e.astype(BF16), vh, preferred_element_type=F32) / jnp.sum(e, axis=-1, keepdims=True)
        o_ref[0, :, sl] = oh[:tq]


def mem_attn(main3, mem_k, mem_v, *, tq, q_block):
    batch, seq, _ = main3.shape
    kv_spec = pl.BlockSpec((N_MEM, MEM_WIDTH), lambda b, i: (b, 0))
    return pl.pallas_call(
        functools.partial(_mem_attn_kernel, tq=tq),
        grid=(batch, seq // tq),
        in_specs=[pl.BlockSpec((1, tq, MEM_WIDTH), lambda b, i: (b, i, q_block)), kv_spec, kv_spec],
        out_specs=pl.BlockSpec((1, tq, MEM_WIDTH), lambda b, i: (b, i, 0)),
        out_shape=jax.ShapeDtypeStruct((batch, seq, MEM_WIDTH), F32),
        compiler_params=_params("parallel", "parallel"),
        name="mem_attn",
    )(main3, mem_k, mem_v)


def _out_proj_kernel(ot_ref, om_ref, gate_ref, x_ref, wt_ref, wm_ref, gp_ref, y_ref):
    gate = gate_ref[...]
    sg = gate * _sigmoid(gate)
    a = (ot_ref[...] * sg[:, :TOK_WIDTH]).astype(BF16)
    b = (om_ref[...] * sg[:, TOK_WIDTH:]).astype(BF16)
    y = jnp.dot(a, wt_ref[...], preferred_element_type=F32) + jnp.dot(b, wm_ref[...], preferred_element_type=F32)
    y_ref[...] = x_ref[...] + y * _rms_scale(y) * gp_ref[...]


def out_proj(o_tok, o_mem, main, x, w_tok, w_mem, g_post, *, tm):
    m = x.shape[0]
    row = lambda width: pl.BlockSpec((tm, width), lambda i: (i, 0))
    const = lambda shape: pl.BlockSpec(shape, lambda i: (0, 0))
    return pl.pallas_call(
        _out_proj_kernel,
        grid=(m // tm,),
        in_specs=[row(TOK_WIDTH), row(MEM_WIDTH), row(D_MODEL), row(D_MODEL),
                  const((TOK_WIDTH, D_MODEL)), const((MEM_WIDTH, D_MODEL)), const((1, D_MODEL))],
        out_specs=row(D_MODEL),
        out_shape=jax.ShapeDtypeStruct((m, D_MODEL), F32),
        compiler_params=_params("parallel"),
        name="out_proj",
    )(o_tok, o_mem, main, x, w_tok, w_mem, g_post.reshape(1, D_MODEL))


PAIR = 2 * RWKV_HEAD_DIM
assert PAIR == LANES
N_PAIRS = RWKV_HEADS // 2


def _pair_ones():
    row = lax.broadcasted_iota(jnp.int32, (PAIR, PAIR), 0) // RWKV_HEAD_DIM
    col = lax.broadcasted_iota(jnp.int32, (PAIR, PAIR), 1) // RWKV_HEAD_DIM
    return jnp.where(row == col, 1.0, 0.0).astype(F32)


def _head_sum(x, ones):
    return _dot_exact_rhs(x, ones)


def _stack(x):
    lane = lax.broadcasted_iota(jnp.int32, x.shape, 1)
    first = lane < RWKV_HEAD_DIM
    return jnp.concatenate([jnp.where(first, x, 0.0), jnp.where(first, 0.0, x)], axis=0)


def _bdot(a, b):
    return jnp.dot(a.astype(BF16), b.astype(BF16), preferred_element_type=F32)


def _wkv_chunk_terms(r, k, v, a_vec, b_vec, lw):
    c = CHUNK
    chunks = [slice(c0, c0 + c) for c0 in range(0, r.shape[0], c)]
    row = lax.broadcasted_iota(jnp.int32, (c, c), 0)
    col = lax.broadcasted_iota(jnp.int32, (c, c), 1)
    tril = jnp.where(col <= row, 1.0, 0.0)
    gi = lax.broadcasted_iota(jnp.int32, (4 * c, 4 * c), 0)
    gj = lax.broadcasted_iota(jnp.int32, (4 * c, 4 * c), 1)
    keep = (gj & (c - 1)) < (gi & (c - 1)) + gi // (2 * c)
    eye = jnp.where(lax.broadcasted_iota(jnp.int32, (2 * c, 2 * c), 0) == lax.broadcasted_iota(jnp.int32, (2 * c, 2 * c), 1),
                    1.0, 0.0).astype(F32)

    cum = [_dot_exact_lhs(tril, lw[s]) for s in chunks]
    last = [x[c - 1:c, :] for x in cum]
    a_t = [_stack(a_vec[s] * jnp.exp(cu - lw[s])) for s, cu in zip(chunks, cum)]
    r_t = [_stack(r[s] * jnp.exp(cu)) for s, cu in zip(chunks, cum)]
    v_s = [_stack(v[s]) for s in chunks]
    xs = [jnp.concatenate([a, rr], axis=0).astype(BF16) for a, rr in zip(a_t, r_t)]
    ys = [jnp.concatenate([_stack(b_vec[s] * jnp.exp(-cu)), _stack(k[s] * jnp.exp(-cu))], axis=0).astype(BF16)
          for s, cu in zip(chunks, cum)]
    g = [jnp.where(keep, lax.dot_general(x, y, (((1,), (1,)), ((), ())), preferred_element_type=F32), 0.0)
         for x, y in zip(xs, ys)]
    l_ab = [x[:2 * c, :2 * c] for x in g]
    l_ak = [x[:2 * c, 2 * c:] for x in g]
    l_rb = [x[2 * c:, :2 * c] for x in g]
    l_rk = [x[2 * c:, 2 * c:] for x in g]
    t_inv = [eye + x for x in l_ab]
    l_pow = l_ab
    for _ in range(int(math.log2(c)) - 1):
        l_pow = [_bdot(x, x) for x in l_pow]
        t_inv = [t + _bdot(p, t) for p, t in zip(l_pow, t_inv)]
    akv = [_bdot(x, y) for x, y in zip(l_ak, v_s)]
    ta = [_bdot(t, jnp.concatenate([a, u], axis=1)) for t, a, u in zip(t_inv, a_t, akv)]
    rbx = [_bdot(x, y) for x, y in zip(l_rb, ta)]
    rkv = [_bdot(x, y) for x, y in zip(l_rk, v_s)]
    r_new = [x + y[:, :PAIR] for x, y in zip(r_t, rbx)]
    y0 = [x[:, PAIR:] + y for x, y in zip(rbx, rkv)]
    e_rem = [jnp.exp(la - cu) for la, cu in zip(last, cum)]
    mn = [_bdot(_stack(b_vec[s] * e).T, t) for s, e, t in zip(chunks, e_rem, ta)]
    kv = [_bdot(_stack(k[s] * e).T, vv) for s, e, vv in zip(chunks, e_rem, v_s)]
    m_c = [x[:, :PAIR] + eye * jnp.exp(la) for x, la in zip(mn, last)]
    n_c = [x[:, PAIR:] + y for x, y in zip(mn, kv)]
    return list(zip(r_new, y0, m_c, n_c))


def _token_shift(cur, prev_row, mu):
    rolled = pltpu.roll(cur, 1, 0)
    first = lax.broadcasted_iota(jnp.int32, cur.shape, 0) == 0
    prev = jnp.where(first, prev_row, rolled)
    return cur + (prev - cur) * mu


def _rwkv_transforms(r, k, v, e, w0, a0, k_k, k_a, ww2, wa2, ones):
    w = -_softplus(-(w0 + jnp.dot(jnp.tanh(e).astype(BF16), ww2, preferred_element_type=F32))) - 0.5
    lw = -jnp.exp(w)
    a = _sigmoid(a0 + jnp.dot(e.astype(BF16), wa2, preferred_element_type=F32))
    kk = k * k_k
    kk = kk / jnp.maximum(jnp.sqrt(_head_sum(kk * kk, ones)), 1e-12)
    k_mod = k * (1.0 + (a - 1.0) * k_a)
    return lw, a, kk, k_mod


def _rwkv_output(y, r, k_mod, v, r_k, ln_w, ln_b, ones):
    mean = _head_sum(y, ones) * (1.0 / RWKV_HEAD_DIM)
    d = y - mean
    var = _head_sum(d * d, ones) * (1.0 / RWKV_HEAD_DIM)
    yn = d * lax.rsqrt(var + GN_EPS) * ln_w + ln_b
    return yn + _head_sum(r * k_mod * r_k, ones) * v


def _wkv_prompt_kernel(r_ref, k_ref, v_ref, e_ref, mur_ref, muk_ref, muv_ref, mue_ref, w0_ref, a0_ref,
                       kk_ref, ka_ref, rk_ref, lnw_ref, lnb_ref, ww2_ref, wa2_ref,
                       o_ref, st_ref, pr_ref, pk_ref, pv_ref, pe_ref, s_ref, y_ref, *, tb):
    t = pl.program_id(2)

    @pl.when(t == 0)
    def _():
        for ref in (pr_ref, pk_ref, pv_ref, pe_ref, s_ref):
            ref[...] = jnp.zeros_like(ref)

    ones = _pair_ones()
    r_cur, k_cur, v_cur, e_cur = r_ref[0, 0], k_ref[0, 0], v_ref[0, 0], e_ref[...]
    r = _token_shift(r_cur, pr_ref[...], mur_ref[...])
    k = _token_shift(k_cur, pk_ref[...], muk_ref[...])
    v = _token_shift(v_cur, pv_ref[...], muv_ref[...])
    e = _token_shift(e_cur, pe_ref[...], mue_ref[...])
    for cur, prev in ((r_cur, pr_ref), (k_cur, pk_ref), (v_cur, pv_ref), (e_cur, pe_ref)):
        prev[...] = cur[tb - 1:tb, :]
    lw, a, kk, k_mod = _rwkv_transforms(r, k, v, e, w0_ref[...], a0_ref[...], kk_ref[...], ka_ref[...],
                                        ww2_ref[...], wa2_ref[...], ones)
    terms = _wkv_chunk_terms(r, k_mod, v, -kk, kk * a, lw)
    s = s_ref[...]
    for i, (r_new, y0, m_c, n_c) in enumerate(terms):
        ys = _bdot(r_new, s) + y0
        y_ref[i * CHUNK:(i + 1) * CHUNK, :] = ys[:CHUNK] + ys[CHUNK:]
        s = _bdot(m_c, s) + n_c
    s_ref[...] = s
    o_ref[...] = _rwkv_output(y_ref[...], r, k_mod, v, rk_ref[...], lnw_ref[...], lnb_ref[...], ones)

    @pl.when(t == pl.num_programs(2) - 1)
    def _():
        st_ref[0, 0] = s


def wkv_prompt(r, k, v, ext, mu_rkv, mu_e, w0, a0, k_k, k_a, r_k, ln_w, ln_b, ww2, wa2, *, tb):
    batch, _, seq, _ = r.shape
    nt = seq // tb
    rows = pl.BlockSpec((1, 1, tb, LANES), lambda b, p, t: (b, p, t, 0))
    vec = lambda off: pl.BlockSpec((1, LANES), lambda b, p, t: (0, off + p))
    lora = pl.BlockSpec((RWKV_EXTRA, LANES), lambda b, p, t: (0, p))
    return pl.pallas_call(
        functools.partial(_wkv_prompt_kernel, tb=tb),
        grid=(batch, N_PAIRS, nt),
        in_specs=[
            rows, rows, rows,
            pl.BlockSpec((tb, RWKV_EXTRA), lambda b, p, t: (b * nt + t, 0)),
            vec(0), vec(N_PAIRS), vec(2 * N_PAIRS),
            pl.BlockSpec((1, RWKV_EXTRA), lambda b, p, t: (0, 0)),
            vec(0), vec(0), vec(0), vec(0), vec(0), vec(0), vec(0), lora, lora,
        ],
        out_specs=[
            pl.BlockSpec((tb, LANES), lambda b, p, t: (b * nt + t, p)),
            pl.BlockSpec((1, 1, PAIR, PAIR), lambda b, p, t: (b, p, 0, 0)),
        ],
        out_shape=[
            jax.ShapeDtypeStruct((batch * seq, TOK_WIDTH), F32),
            jax.ShapeDtypeStruct((batch, N_PAIRS, PAIR, PAIR), F32),
        ],
        scratch_shapes=[
            pltpu.VMEM((1, LANES), F32), pltpu.VMEM((1, LANES), F32), pltpu.VMEM((1, LANES), F32),
            pltpu.VMEM((1, RWKV_EXTRA), F32), pltpu.VMEM((PAIR, PAIR), F32), pltpu.VMEM((tb, LANES), F32),
        ],
        compiler_params=_params("parallel", "parallel", "arbitrary"),
        name="wkv_prompt",
    )(r, k, v, ext, mu_rkv, mu_rkv, mu_rkv, mu_e, w0, a0, k_k, k_a, r_k, ln_w, ln_b, ww2, wa2)


def _wkv_step_kernel(cur_ref, prev_ref, e_ref, pe_ref, mu_ref, mue_ref, w0_ref, a0_ref, kk_ref, ka_ref, rk_ref,
                     lnw_ref, lnb_ref, ww2_ref, wa2_ref, st_ref, o_ref, stn_ref):
    hd = RWKV_HEAD_DIM
    e_cur = e_ref[0]
    e = e_cur + (pe_ref[0] - e_cur) * mue_ref[...]
    e_rows = jnp.broadcast_to(e, (SUBLANES, RWKV_EXTRA))
    e_tanh = jnp.tanh(e_rows).astype(BF16)
    e_lin = e_rows.astype(BF16)
    eye = lax.broadcasted_iota(jnp.int32, (hd, hd), 0) == lax.broadcasted_iota(jnp.int32, (hd, hd), 1)

    def mixed(i, h):
        cur = cur_ref[0, i, h:h + 1, :]
        return cur + (prev_ref[0, i, h:h + 1, :] - cur) * mu_ref[i, h:h + 1, :]

    for h in range(RWKV_HEADS):
        row = lambda ref: ref[h:h + 1, :]
        r, k, v = mixed(0, h), mixed(1, h), mixed(2, h)
        w = -_softplus(-(row(w0_ref) + jnp.dot(e_tanh, ww2_ref[h], preferred_element_type=F32)[0:1])) - 0.5
        decay = jnp.exp(-jnp.exp(w))
        a = _sigmoid(row(a0_ref) + jnp.dot(e_lin, wa2_ref[h], preferred_element_type=F32)[0:1])
        kk = k * row(kk_ref)
        kk = kk / jnp.maximum(jnp.sqrt(jnp.sum(kk * kk, axis=-1, keepdims=True)), 1e-12)
        k_mod = k * (1.0 + (a - 1.0) * row(ka_ref))
        s = st_ref[0, h]
        sa = jnp.sum(s * (-kk), axis=-1, keepdims=True)
        v_col = jnp.sum(jnp.where(eye, jnp.broadcast_to(v, (hd, hd)), 0.0), axis=-1, keepdims=True)
        s_new = s * decay + sa * (kk * a) + v_col * k_mod
        stn_ref[0, h] = s_new
        y_col = jnp.sum(s_new * r, axis=-1, keepdims=True)
        y = jnp.sum(jnp.where(eye, jnp.broadcast_to(y_col, (hd, hd)), 0.0), axis=0, keepdims=True)
        mean = jnp.mean(y, axis=-1, keepdims=True)
        d = y - mean
        var = jnp.mean(d * d, axis=-1, keepdims=True)
        yn = d * lax.rsqrt(var + GN_EPS) * row(lnw_ref) + row(lnb_ref)
        o_ref[0, h:h + 1, :] = yn + jnp.sum(r * k_mod * row(rk_ref), axis=-1, keepdims=True) * v


def wkv_step(cur, prev, e_cur, e_prev, mu, mu_e, w0, a0, k_k, k_a, r_k, ln_w, ln_b, ww2h, wa2h, state):
    batch = cur.shape[0]
    hshape = (RWKV_HEADS, RWKV_HEAD_DIM)
    per_row = lambda shape: pl.BlockSpec((1,) + shape, lambda b: (b,) + (0,) * len(shape))
    const = lambda shape: pl.BlockSpec(shape, lambda b: (0,) * len(shape))
    return pl.pallas_call(
        _wkv_step_kernel,
        grid=(batch,),
        in_specs=[per_row((3,) + hshape), per_row((3,) + hshape), per_row((1, RWKV_EXTRA)), per_row((1, RWKV_EXTRA)),
                  const((3,) + hshape), const((1, RWKV_EXTRA))] + [const(hshape)] * 7
                 + [const((RWKV_HEADS, RWKV_EXTRA, RWKV_HEAD_DIM))] * 2 + [per_row(hshape + (RWKV_HEAD_DIM,))],
        out_specs=[per_row(hshape), per_row(hshape + (RWKV_HEAD_DIM,))],
        out_shape=[jax.ShapeDtypeStruct((batch,) + hshape, F32),
                   jax.ShapeDtypeStruct((batch,) + hshape + (RWKV_HEAD_DIM,), F32)],
        compiler_params=_params("parallel"),
        name="wkv_step",
    )(cur, prev, e_cur, e_prev, mu, mu_e, w0, a0, k_k, k_a, r_k, ln_w, ln_b, ww2h, wa2h, state)


def _relayout_in_proj(w_in, tok_cols, extra_cols, extra_width):
    w_t = jnp.transpose(w_in)
    tok = w_t[:tok_cols]
    extra = w_t[tok_cols:tok_cols + extra_cols]
    q_mem = w_t[tok_cols + extra_cols:tok_cols + extra_cols + MEM_WIDTH]
    gate = w_t[tok_cols + extra_cols + MEM_WIDTH:]
    main = jnp.concatenate([gate, tok, q_mem], axis=0).astype(BF16)
    extra = jnp.pad(extra, ((0, extra_width - extra_cols), (0, 0))).astype(BF16)
    return main, extra


def _pad_lanes(v, width):
    return jnp.pad(v, (0, width - v.shape[0])).reshape(1, width)


def _split_w_o(w_o):
    return w_o[:TOK_WIDTH].astype(BF16), w_o[TOK_WIDTH:].astype(BF16)


def _memory_kv(mem, g_mem, w_mem):
    kv = norm_proj(mem, g_mem, jnp.transpose(w_mem).astype(BF16), tm=mem.shape[0])
    return kv[:, :MEM_WIDTH], kv[:, MEM_WIDTH:]


def kernel(x_prompt, x_sample, cache_fox_k, cache_fox_v, cache_fox_logf, cache_mem_k, cache_mem_v, state_rwkv_wkv, state_rwkv_shift, page_table, mem_prompt, g_pre, g_post, g_mem, w_mem, w_o, fox_w_in, fox_b_f, rwkv_w_in, rwkv_mu, rwkv_w0, rwkv_w_w2, rwkv_a0, rwkv_w_a2, rwkv_k_k, rwkv_k_a, rwkv_r_k, rwkv_ln_w, rwkv_ln_b):
    bp, seq, _ = x_prompt.shape
    bs = x_sample.shape[0]
    xp = x_prompt.reshape(bp * seq, D_MODEL)
    xs = x_sample.reshape(bs, D_MODEL)
    mem = mem_prompt.reshape(bp * N_MEM, D_MODEL)
    tm_p = 1024
    mem_shape = (bp, N_MEM, MEM_HEADS, MEM_HEAD_DIM)

    w_main, w_ext = _relayout_in_proj(fox_w_in[0], 3 * TOK_WIDTH, FOX_HEADS, FOX_EXTRA)
    w_tok, w_mo = _split_w_o(w_o[0])
    b_pad = _pad_lanes(fox_b_f[0], LANES)
    mk0, mv0 = _memory_kv(mem, g_mem[0], w_mem[0])

    gm_p, q_p, k_p, v_p, ext_p = norm_proj(xp, g_pre[0], w_main, w_ext, tm=tm_p, heads=(bp, seq))
    logf_p, ccol, crow = fox_gate(ext_p, b_pad, batch=bp, seq=seq, tb=256)
    o_tok = fox_flash(q_p, k_p, v_p, ccol, crow, tq=512)
    o_mem = mem_attn(gm_p.reshape(bp, seq, GM_WIDTH), mk0, mv0, tq=1024, q_block=D_MODEL // MEM_WIDTH)
    xp = out_proj(o_tok, o_mem.reshape(bp * seq, MEM_WIDTH), gm_p, xp, w_tok, w_mo, g_post[0], tm=256)

    main_s, ext_s = norm_proj(xs, g_pre[0], w_main, w_ext, tm=bs)
    q_s, k_s, v_s = (main_s[:, off:off + TOK_WIDTH] for off in (TOKA_OFF, TOKB_OFF, TOKC_OFF))
    logf_pool_t = jnp.pad(jnp.transpose(cache_fox_logf[0], (0, 2, 1)), ((0, 0), (0, HEAD_PAD - FOX_HEADS), (0, 0)))
    k_pool, v_pool = (jnp.transpose(c, (0, 1, 3, 2, 4)) for c in (cache_fox_k, cache_fox_v))
    o_tok_s, logf_s = fox_decode(page_table, q_s, k_s, v_s, ext_s, b_pad, k_pool, v_pool, logf_pool_t, layer=0)
    o_mem_s = mem_attn(main_s.reshape(bs, 1, MAIN_WIDTH), cache_mem_k[0].reshape(bs * N_MEM, MEM_WIDTH),
                       cache_mem_v[0].reshape(bs * N_MEM, MEM_WIDTH), tq=1, q_block=QMEM_OFF // MEM_WIDTH)
    xs = out_proj(o_tok_s, o_mem_s.reshape(bs, MEM_WIDTH), main_s, xs, w_tok, w_mo, g_post[0], tm=bs)

    head_shape = (FOX_HEADS, FOX_HEAD_DIM)
    fox_k_p = jnp.transpose(k_p, (0, 2, 1, 3))[None]
    fox_v_p = jnp.transpose(v_p, (0, 2, 1, 3))[None]
    fox_f_p = logf_p[:, :FOX_HEADS].reshape(1, bp, seq, FOX_HEADS)
    fox_k_s = k_s.reshape((1, bs, 1) + head_shape)
    fox_v_s = v_s.reshape((1, bs, 1) + head_shape)
    fox_f_s = logf_s[:, :FOX_HEADS].reshape(1, bs, 1, FOX_HEADS)

    w_main, w_ext = _relayout_in_proj(rwkv_w_in[0], 3 * TOK_WIDTH, 2 * LORA, RWKV_EXTRA)
    w_tok, w_mo = _split_w_o(w_o[1])
    mk1, mv1 = _memory_kv(mem, g_mem[1], w_mem[1])
    mu_rkv = rwkv_mu[0][:3 * TOK_WIDTH].reshape(1, 3 * TOK_WIDTH)
    mu_e = _pad_lanes(rwkv_mu[0][3 * TOK_WIDTH:], RWKV_EXTRA)
    ww2 = jnp.pad(rwkv_w_w2[0], ((0, RWKV_EXTRA - LORA), (0, 0))).astype(BF16)
    wa2 = jnp.pad(rwkv_w_a2[0], ((LORA, RWKV_EXTRA - 2 * LORA), (0, 0))).astype(BF16)
    vecs = [p[0] for p in (rwkv_w0, rwkv_a0, rwkv_k_k, rwkv_k_a, rwkv_r_k, rwkv_ln_w, rwkv_ln_b)]

    gm_p, r_p, k_p, v_p, ext_p = norm_proj(xp, g_pre[1], w_main, w_ext, tm=tm_p, heads=(bp, seq))
    o_tok, st_pairs = wkv_prompt(r_p, k_p, v_p, ext_p, mu_rkv, mu_e, *[v.reshape(1, TOK_WIDTH) for v in vecs], ww2, wa2,
                                 tb=512)
    o_mem = mem_attn(gm_p.reshape(bp, seq, GM_WIDTH), mk1, mv1, tq=1024, q_block=D_MODEL // MEM_WIDTH)
    xp = out_proj(o_tok, o_mem.reshape(bp * seq, MEM_WIDTH), gm_p, xp, w_tok, w_mo, g_post[1], tm=256)

    main_s, ext_s = norm_proj(xs, g_pre[1], w_main, w_ext, tm=bs)
    hshape = (RWKV_HEADS, RWKV_HEAD_DIM)
    shift_prev = state_rwkv_shift[0][:, 0, :]
    cur = main_s[:, TOKA_OFF:QMEM_OFF].reshape((bs, 3) + hshape)
    prev = shift_prev[:, :3 * TOK_WIDTH].reshape((bs, 3) + hshape)
    e_prev = jnp.pad(shift_prev[:, 3 * TOK_WIDTH:], ((0, 0), (0, RWKV_EXTRA - 2 * LORA))).reshape(bs, 1, RWKV_EXTRA)
    per_head = lambda w: jnp.transpose(w.reshape((RWKV_EXTRA,) + hshape), (1, 0, 2))
    o_tok_s, wkv_s = wkv_step(cur, prev, ext_s.reshape(bs, 1, RWKV_EXTRA), e_prev, mu_rkv.reshape((3,) + hshape), mu_e,
                              *[v.reshape(hshape) for v in vecs], per_head(ww2), per_head(wa2), state_rwkv_wkv[0])
    o_mem_s = mem_attn(main_s.reshape(bs, 1, MAIN_WIDTH), cache_mem_k[1].reshape(bs * N_MEM, MEM_WIDTH),
                       cache_mem_v[1].reshape(bs * N_MEM, MEM_WIDTH), tq=1, q_block=QMEM_OFF // MEM_WIDTH)
    xs = out_proj(o_tok_s.reshape(bs, TOK_WIDTH), o_mem_s.reshape(bs, MEM_WIDTH), main_s, xs, w_tok, w_mo, g_post[1], tm=bs)

    st6 = st_pairs.reshape(bp, N_PAIRS, 2, RWKV_HEAD_DIM, 2, RWKV_HEAD_DIM)
    st_heads = jnp.stack([st6[:, :, 0, :, 0, :], st6[:, :, 1, :, 1, :]], axis=2)
    wkv_p = jnp.swapaxes(st_heads.reshape((bp,) + hshape + (RWKV_HEAD_DIM,)), -1, -2)
    last_p = jnp.concatenate([a[:, :, seq - 1, :].reshape(bp, TOK_WIDTH) for a in (r_p, k_p, v_p)], axis=-1)
    last_e = ext_p.reshape(bp, seq, RWKV_EXTRA)[:, seq - 1, :2 * LORA]
    shift_p = jnp.concatenate([last_p, last_e], axis=-1).reshape(1, bp, 1, 3 * TOK_WIDTH + 2 * LORA)
    shift_s = jnp.concatenate([main_s[:, TOKA_OFF:QMEM_OFF], ext_s[:, :2 * LORA]], axis=-1).reshape(1, bs, 1, 3 * TOK_WIDTH + 2 * LORA)

    return (xp.reshape(bp, seq, D_MODEL), xs.reshape(bs, 1, D_MODEL),
            jnp.stack([mk0.reshape(mem_shape), mk1.reshape(mem_shape)]),
            jnp.stack([mv0.reshape(mem_shape), mv1.reshape(mem_shape)]),
            fox_k_p, fox_v_p, fox_f_p, fox_k_s, fox_v_s, fox_f_s,
            wkv_p[None], shift_p, wkv_s[None], shift_s)
```

```python
import functools
import math

import jax
import jax.numpy as jnp
from jax import lax
from jax.experimental import pallas as pl
from jax.experimental.pallas import tpu as pltpu

F32 = jnp.float32
BF16 = jnp.bfloat16

D_MODEL = 2048
N_MEM = 256
MEM_HEADS = 4
MEM_HEAD_DIM = 128
MEM_WIDTH = MEM_HEADS * MEM_HEAD_DIM
TOK_WIDTH = D_MODEL - MEM_WIDTH
FOX_HEAD_DIM = 128
FOX_HEADS = TOK_WIDTH // FOX_HEAD_DIM
RWKV_HEAD_DIM = 64
RWKV_HEADS = TOK_WIDTH // RWKV_HEAD_DIM
LORA = 96
PAGE_SIZE = 128
RMS_EPS = 1e-6
GN_EPS = 64e-5

LANES = 128
SUBLANES = 8
V7X_VMEM_BYTES = 64 * 1024 * 1024
VMEM_LIMIT_BYTES = V7X_VMEM_BYTES - 8 * 1024 * 1024

GATE_OFF = 0
TOKA_OFF = D_MODEL
TOKB_OFF = TOKA_OFF + TOK_WIDTH
TOKC_OFF = TOKB_OFF + TOK_WIDTH
QMEM_OFF = TOKC_OFF + TOK_WIDTH
MAIN_WIDTH = QMEM_OFF + MEM_WIDTH
FOX_EXTRA = LANES
RWKV_EXTRA = 2 * LANES

HEAD_PAD = 16
CHUNK = 64


def _params(*sem):
    return pltpu.CompilerParams(dimension_semantics=sem, vmem_limit_bytes=VMEM_LIMIT_BYTES)


def _rms_scale(x):
    return lax.rsqrt(jnp.mean(x * x, axis=-1, keepdims=True) + RMS_EPS)


def _sigmoid(x):
    return 1.0 / (1.0 + jnp.exp(-x))


def _softplus(x):
    return jnp.maximum(x, 0.0) + jnp.log(1.0 + jnp.exp(-jnp.abs(x)))


def _split(x, pieces):
    out = []
    for _ in range(pieces - 1):
        p = x.astype(BF16)
        out.append(p)
        x = x - p.astype(F32)
    return out + [x.astype(BF16)]


def _dot_exact_rhs(x, rhs, pieces=3):
    rows = x.shape[0]
    parts = jnp.dot(jnp.concatenate(_split(x, pieces), axis=0), rhs.astype(BF16), preferred_element_type=F32)
    return sum(parts[i * rows:(i + 1) * rows] for i in range(pieces))


def _dot_exact_lhs(lhs, x, pieces=3):
    cols = x.shape[1]
    parts = jnp.dot(lhs.astype(BF16), jnp.concatenate(_split(x, pieces), axis=1), preferred_element_type=F32)
    return sum(parts[:, i * cols:(i + 1) * cols] for i in range(pieces))


TN = 4 * LANES
GATE_BLOCKS = D_MODEL // TN
TOK_BLOCKS = TOK_WIDTH // TN
QMEM_BLOCK = GATE_BLOCKS + 3 * TOK_BLOCKS
GM_WIDTH = D_MODEL + MEM_WIDTH
N_GROUPS128 = TOK_WIDTH // LANES
assert MAIN_WIDTH == (QMEM_BLOCK + 1) * TN and MEM_WIDTH == TN


def _nt_dot(a, b):
    return lax.dot_general(a, b, (((1,), (1,)), ((), ())), preferred_element_type=F32)


def _norm_proj_kernel(x_ref, g_ref, w_ref, *rest, has_extra, head_mode):
    rest = list(rest)
    we_ref = rest.pop(0) if has_extra else None
    h_ref = rest.pop()
    oe_ref = rest.pop() if has_extra else None
    j = pl.program_id(1)

    @pl.when(j == 0)
    def _():
        x = x_ref[...]
        h = (x * _rms_scale(x) * g_ref[...]).astype(BF16)
        h_ref[...] = h
        if has_extra:
            oe_ref[...] = _nt_dot(h, we_ref[...])

    if not head_mode:
        rest[0][...] = _nt_dot(h_ref[...], w_ref[...])
        return
    gm_ref, groups = rest[0], rest[1:]

    @pl.when((j < GATE_BLOCKS) | (j == QMEM_BLOCK))
    def _():
        gm_ref[...] = _nt_dot(h_ref[...], w_ref[...])

    for gi, ref in enumerate(groups):
        first = GATE_BLOCKS + gi * TOK_BLOCKS

        @pl.when((j >= first) & (j < first + TOK_BLOCKS))
        def _(ref=ref):
            res = _nt_dot(h_ref[...], w_ref[...])
            for hh in range(TN // LANES):
                ref[0, hh] = res[:, hh * LANES:(hh + 1) * LANES]


def norm_proj(x, g, w_t, w_extra_t=None, *, tm, heads=None):
    m, d = x.shape
    n = w_t.shape[0]
    assert m % tm == 0 and n % TN == 0
    has_extra = w_extra_t is not None
    in_specs = [
        pl.BlockSpec((tm, d), lambda i, j: (i, 0)),
        pl.BlockSpec((1, d), lambda i, j: (0, 0)),
        pl.BlockSpec((TN, d), lambda i, j: (j, 0)),
    ]
    args = [x, g.reshape(1, d), w_t]
    if has_extra:
        in_specs.append(pl.BlockSpec(w_extra_t.shape, lambda i, j: (0, 0)))
        args.append(w_extra_t)
    if heads is None:
        out_specs = [pl.BlockSpec((tm, TN), lambda i, j: (i, j))]
        out_shape = [jax.ShapeDtypeStruct((m, n), F32)]
    else:
        batch, seq = heads
        assert n == MAIN_WIDTH and seq % tm == 0
        per_seq = seq // tm
        gm_col = lambda j: jnp.where(j == QMEM_BLOCK, GATE_BLOCKS, jnp.minimum(j, GATE_BLOCKS - 1))
        out_specs = [pl.BlockSpec((tm, TN), lambda i, j: (i, gm_col(j)))]
        out_shape = [jax.ShapeDtypeStruct((m, GM_WIDTH), F32)]
        for gi in range(3):
            first = GATE_BLOCKS + gi * TOK_BLOCKS
            out_specs.append(pl.BlockSpec(
                (1, TN // LANES, tm, LANES),
                lambda i, j, first=first: (i // per_seq, jnp.clip(j - first, 0, TOK_BLOCKS - 1), i % per_seq, 0)))
            out_shape.append(jax.ShapeDtypeStruct((batch, N_GROUPS128, seq, LANES), F32))
    if has_extra:
        e = w_extra_t.shape[0]
        out_specs.append(pl.BlockSpec((tm, e), lambda i, j: (i, 0)))
        out_shape.append(jax.ShapeDtypeStruct((m, e), F32))
    outs = pl.pallas_call(
        functools.partial(_norm_proj_kernel, has_extra=has_extra, head_mode=heads is not None),
        grid=(m // tm, n // TN),
        in_specs=in_specs,
        out_specs=out_specs,
        out_shape=out_shape,
        scratch_shapes=[pltpu.VMEM((tm, d), BF16)],
        compiler_params=_params("parallel", "arbitrary"),
        name="norm_proj",
    )(*args)
    return outs if len(outs) > 1 else outs[0]


def _fox_gate_kernel(f_ref, b_ref, logf_ref, ccol_ref, crow_ref, carry_ref, *, tb):
    @pl.when(pl.program_id(1) == 0)
    def _():
        carry_ref[...] = jnp.zeros_like(carry_ref)

    logf = -_softplus(-(f_ref[...] + b_ref[...]))
    logf_ref[...] = logf
    row = lax.broadcasted_iota(jnp.int32, (tb, tb), 0)
    col = lax.broadcasted_iota(jnp.int32, (tb, tb), 1)
    c = _dot_exact_lhs(jnp.where(col <= row, 1.0, 0.0), logf) + carry_ref[...]
    ccol_ref[...] = c
    crow_ref[0] = c.T[:HEAD_PAD]
    carry_ref[...] = c[tb - 1:tb, :]


def fox_gate(f_ext, b_pad, *, batch, seq, tb):
    nt = seq // tb
    row_spec = pl.BlockSpec((tb, LANES), lambda b, t: (b * nt + t, 0))
    return pl.pallas_call(
        functools.partial(_fox_gate_kernel, tb=tb),
        grid=(batch, nt),
        in_specs=[row_spec, pl.BlockSpec((1, LANES), lambda b, t: (0, 0))],
        out_specs=[row_spec, row_spec, pl.BlockSpec((1, HEAD_PAD, tb), lambda b, t: (b, 0, t))],
        out_shape=[
            jax.ShapeDtypeStruct((batch * seq, LANES), F32),
            jax.ShapeDtypeStruct((batch * seq, LANES), F32),
            jax.ShapeDtypeStruct((batch, HEAD_PAD, seq), F32),
        ],
        scratch_shapes=[pltpu.VMEM((1, LANES), F32)],
        compiler_params=_params("parallel", "arbitrary"),
        name="fox_gate",
    )(f_ext, b_pad)


_MASKED = -1e30
LOG2E = 1.4426950408889634


def _fox_flash_kernel(q_ref, k_ref, v_ref, ccol_ref, crow_ref, o_ref, kb_ref, vb_ref, *, tq):
    h = pl.program_id(1)
    qi = pl.program_id(2)

    @pl.when(qi == 0)
    def _():
        kb_ref[...] = k_ref[0, 0].astype(BF16)
        vb_ref[...] = v_ref[0, 0].astype(BF16)

    q = (q_ref[0, 0] * (FOX_HEAD_DIM ** -0.5 * LOG2E)).astype(BF16)
    lane = lax.broadcasted_iota(jnp.int32, (tq, LANES), 1)
    cq = jnp.sum(jnp.where(lane == h, ccol_ref[...], 0.0), axis=-1, keepdims=True) * LOG2E

    def step(c, carry, diagonal):
        m, l, acc = carry
        start = pl.multiple_of(c * tq, tq)
        kc = kb_ref[pl.ds(start, tq), :]
        vc = vb_ref[pl.ds(start, tq), :]
        ck = crow_ref[0, 0, :, pl.ds(start, tq)] * LOG2E
        s = lax.dot_general(q, kc, (((1,), (1,)), ((), ())), preferred_element_type=F32) - ck
        if diagonal:
            row = lax.broadcasted_iota(jnp.int32, (tq, tq), 0)
            col = lax.broadcasted_iota(jnp.int32, (tq, tq), 1)
            s = jnp.where(col <= row, s, _MASKED)
        m_new = jnp.maximum(m, jnp.max(s, axis=-1, keepdims=True) + cq)
        p = jnp.exp2(s - (m_new - cq))
        alpha = jnp.exp2(m - m_new)
        l = alpha * l + jnp.sum(p, axis=-1, keepdims=True)
        acc = alpha * acc + jnp.dot(p.astype(BF16), vc, preferred_element_type=F32)
        return m_new, l, acc

    init = (jnp.full((tq, 1), _MASKED, F32), jnp.zeros((tq, 1), F32), jnp.zeros((tq, FOX_HEAD_DIM), F32))
    carry = lax.fori_loop(0, qi, lambda c, cr: step(c, cr, False), init)
    _, l, acc = step(qi, carry, True)
    o_ref[...] = acc / l


def fox_flash(q, k, v, ccol, crow, *, tq):
    batch, _, seq, _ = q.shape
    nq = seq // tq
    crow4 = crow.reshape(batch, HEAD_PAD, 1, seq)
    kv_spec = pl.BlockSpec((1, 1, seq, FOX_HEAD_DIM), lambda b, h, i: (b, h, 0, 0))
    return pl.pallas_call(
        functools.partial(_fox_flash_kernel, tq=tq),
        grid=(batch, FOX_HEADS, nq),
        in_specs=[
            pl.BlockSpec((1, 1, tq, FOX_HEAD_DIM), lambda b, h, i: (b, h, i, 0)),
            kv_spec, kv_spec,
            pl.BlockSpec((tq, LANES), lambda b, h, i: (b * nq + i, 0)),
            pl.BlockSpec((1, 1, 1, seq), lambda b, h, i: (b, h, 0, 0)),
        ],
        out_specs=pl.BlockSpec((tq, LANES), lambda b, h, i: (b * nq + i, h)),
        out_shape=jax.ShapeDtypeStruct((batch * seq, TOK_WIDTH), F32),
        scratch_shapes=[pltpu.VMEM((seq, LANES), BF16), pltpu.VMEM((seq, LANES), BF16)],
        compiler_params=_params("parallel", "parallel", "arbitrary"),
        name="fox_flash",
    )(q, k, v, ccol, crow4)


PAGES_PER_STEP = 4


def _head_mask():
    row = lax.broadcasted_iota(jnp.int32, (HEAD_PAD, TOK_WIDTH), 0)
    col = lax.broadcasted_iota(jnp.int32, (HEAD_PAD, TOK_WIDTH), 1)
    lo = row * FOX_HEAD_DIM
    return (col >= lo) & (col < lo + FOX_HEAD_DIM)


def _fox_decode_kernel(pt_ref, q_ref, kn_ref, vn_ref, f_ref, bf_ref, *rest):
    del pt_ref
    pps = PAGES_PER_STEP
    lf_refs, k_refs, v_refs = rest[:pps], rest[pps:2 * pps], rest[2 * pps:3 * pps]
    o_ref, lfn_ref, qbd_ref, m_ref, l_ref, acc_ref, carry_ref = rest[3 * pps:]
    b = pl.program_id(0)
    j = pl.program_id(1)
    scale = FOX_HEAD_DIM ** -0.5
    hmask = _head_mask()

    @pl.when(j == 0)
    def _():
        q = q_ref[pl.ds(b, 1), :]
        q_bd = jnp.where(hmask, jnp.broadcast_to(q, (HEAD_PAD, TOK_WIDTH)), 0.0)
        qbd_ref[...] = q_bd.astype(BF16)
        logf_new = -_softplus(-(f_ref[pl.ds(b, 1), :] + bf_ref[...]))
        lfn_ref[pl.ds(b, 1), :] = logf_new
        eye = lax.broadcasted_iota(jnp.int32, (HEAD_PAD, LANES), 0) == lax.broadcasted_iota(jnp.int32, (HEAD_PAD, LANES), 1)
        carry_ref[...] = jnp.sum(jnp.where(eye, jnp.broadcast_to(logf_new, (HEAD_PAD, LANES)), 0.0), axis=-1, keepdims=True)
        m_ref[...] = jnp.sum(q_bd * kn_ref[pl.ds(b, 1), :], axis=-1, keepdims=True) * scale
        l_ref[...] = jnp.ones_like(l_ref)
        acc_ref[...] = jnp.broadcast_to(vn_ref[pl.ds(b, 1), :], (HEAD_PAD, TOK_WIDTH))

    def page_rows(ref):
        return jnp.concatenate([ref[0, 0, h].astype(BF16) for h in range(FOX_HEADS)], axis=1)

    lfs = [ref[0] for ref in lf_refs]
    src = lax.broadcasted_iota(jnp.int32, (PAGE_SIZE, PAGE_SIZE), 0)
    dst = lax.broadcasted_iota(jnp.int32, (PAGE_SIZE, PAGE_SIZE), 1)
    inside = _dot_exact_rhs(jnp.concatenate(lfs, axis=0), jnp.where(src > dst, 1.0, 0.0))
    carries = [carry_ref[...]]
    for lf in lfs:
        carries.append(carries[-1] + jnp.sum(lf, axis=-1, keepdims=True))
    carry_ref[...] = carries[-1]

    q_bd = qbd_ref[...]
    scores = []
    for i in range(pps):
        s = lax.dot_general(q_bd, page_rows(k_refs[i]), (((1,), (1,)), ((), ())), preferred_element_type=F32)
        scores.append(s * scale + (inside[i * HEAD_PAD:(i + 1) * HEAD_PAD] + carries[i]))
    m_old = m_ref[...]
    m_new = m_old
    for s in scores:
        m_new = jnp.maximum(m_new, jnp.max(s, axis=-1, keepdims=True))
    alpha = jnp.exp(m_old - m_new)
    l_new = alpha * l_ref[...]
    acc = alpha * acc_ref[...]
    for i, s in enumerate(scores):
        p = jnp.exp(s - m_new)
        l_new = l_new + jnp.sum(p, axis=-1, keepdims=True)
        acc = acc + jnp.dot(p.astype(BF16), page_rows(v_refs[i]), preferred_element_type=F32)
    m_ref[...] = m_new
    l_ref[...] = l_new
    acc_ref[...] = acc

    @pl.when(j == pl.num_programs(1) - 1)
    def _():
        out = jnp.where(hmask, acc / l_new, 0.0)
        o_ref[pl.ds(b, 1), :] = jnp.sum(out, axis=0, keepdims=True)


def fox_decode(page_table, q, k_new, v_new, f_ext, b_pad, k_pool, v_pool, logf_pool_t, *, layer):
    batch, n_pages = page_table.shape
    pps = PAGES_PER_STEP
    assert n_pages % pps == 0
    whole = lambda shape: pl.BlockSpec(shape, lambda b, j, pt: (0,) * len(shape))
    page_id = lambda i: (lambda b, j, pt: pt[b, n_pages - 1 - j * pps - i])
    lf_spec = lambda i: pl.BlockSpec((1, HEAD_PAD, PAGE_SIZE), lambda b, j, pt: (page_id(i)(b, j, pt), 0, 0))
    kv_spec = lambda i: pl.BlockSpec((1, 1, FOX_HEADS, PAGE_SIZE, FOX_HEAD_DIM),
                                     lambda b, j, pt: (layer, page_id(i)(b, j, pt), 0, 0, 0))
    grid_spec = pltpu.PrefetchScalarGridSpec(
        num_scalar_prefetch=1,
        grid=(batch, n_pages // pps),
        in_specs=[whole((batch, TOK_WIDTH))] * 3 + [whole((batch, LANES)), whole((1, LANES))]
                 + [lf_spec(i) for i in range(pps)] + [kv_spec(i) for i in range(pps)] * 2,
        out_specs=[whole((batch, TOK_WIDTH)), whole((batch, LANES))],
        scratch_shapes=[
            pltpu.VMEM((HEAD_PAD, TOK_WIDTH), BF16),
            pltpu.VMEM((HEAD_PAD, 1), F32),
            pltpu.VMEM((HEAD_PAD, 1), F32),
            pltpu.VMEM((HEAD_PAD, TOK_WIDTH), F32),
            pltpu.VMEM((HEAD_PAD, 1), F32),
        ],
    )
    return pl.pallas_call(
        _fox_decode_kernel,
        grid_spec=grid_spec,
        out_shape=[jax.ShapeDtypeStruct((batch, TOK_WIDTH), F32), jax.ShapeDtypeStruct((batch, LANES), F32)],
        compiler_params=_params("arbitrary", "arbitrary"),
        name="fox_decode",
    )(page_table, q, k_new, v_new, f_ext, b_pad, *([logf_pool_t] * pps), *([k_pool] * pps), *([v_pool] * pps))


def _mem_attn_kernel(q_ref, k_ref, v_ref, o_ref, *, tq):
    rows = max(tq, SUBLANES)
    scale = MEM_HEAD_DIM ** -0.5
    for h in range(MEM_HEADS):
        sl = slice(h * MEM_HEAD_DIM, (h + 1) * MEM_HEAD_DIM)
        qh = jnp.broadcast_to(q_ref[0, :, sl], (rows, MEM_HEAD_DIM)).astype(BF16)
        kh = k_ref[:, sl].astype(BF16)
        vh = v_ref[:, sl].astype(BF16)
        s = lax.dot_general(qh, kh, (((1,), (1,)), ((), ())), preferred_element_type=F32) * scale
        e = jnp.exp(s - jnp.max(s, axis=-1, keepdims=True))
        oh = jnp.dot(e.astype(BF16), vh, preferred_element_type=F32) / jnp.sum(e, axis=-1, keepdims=True)
        o_ref[0, :, sl] = oh[:tq]


def mem_attn(main3, mem_k, mem_v, *, tq, q_block):
    batch, seq, _ = main3.shape
    kv_spec = pl.BlockSpec((N_MEM, MEM_WIDTH), lambda b, i: (b, 0))
    return pl.pallas_call(
        functools.partial(_mem_attn_kernel, tq=tq),
        grid=(batch, seq // tq),
        in_specs=[pl.BlockSpec((1, tq, MEM_WIDTH), lambda b, i: (b, i, q_block)), kv_spec, kv_spec],
        out_specs=pl.BlockSpec((1, tq, MEM_WIDTH), lambda b, i: (b, i, 0)),
        out_shape=jax.ShapeDtypeStruct((batch, seq, MEM_WIDTH), F32),
        compiler_params=_params("parallel", "parallel"),
        name="mem_attn",
    )(main3, mem_k, mem_v)


def _out_proj_kernel(ot_ref, om_ref, gate_ref, x_ref, wt_ref, wm_ref, gp_ref, y_ref):
    gate = gate_ref[...]
    sg = gate * _sigmoid(gate)
    a = (ot_ref[...] * sg[:, :TOK_WIDTH]).astype(BF16)
    b = (om_ref[...] * sg[:, TOK_WIDTH:]).astype(BF16)
    y = jnp.dot(a, wt_ref[...], preferred_element_type=F32) + jnp.dot(b, wm_ref[...], preferred_element_type=F32)
    y_ref[...] = x_ref[...] + y * _rms_scale(y) * gp_ref[...]


def out_proj(o_tok, o_mem, main, x, w_tok, w_mem, g_post, *, tm):
    m = x.shape[0]
    row = lambda width: pl.BlockSpec((tm, width), lambda i: (i, 0))
    const = lambda shape: pl.BlockSpec(shape, lambda i: (0, 0))
    return pl.pallas_call(
        _out_proj_kernel,
        grid=(m // tm,),
        in_specs=[row(TOK_WIDTH), row(MEM_WIDTH), row(D_MODEL), row(D_MODEL),
                  const((TOK_WIDTH, D_MODEL)), const((MEM_WIDTH, D_MODEL)), const((1, D_MODEL))],
        out_specs=row(D_MODEL),
        out_shape=jax.ShapeDtypeStruct((m, D_MODEL), F32),
        compiler_params=_params("parallel"),
        name="out_proj",
    )(o_tok, o_mem, main, x, w_tok, w_mem, g_post.reshape(1, D_MODEL))


PAIR = 2 * RWKV_HEAD_DIM
assert PAIR == LANES
N_PAIRS = RWKV_HEADS // 2


def _pair_ones():
    row = lax.broadcasted_iota(jnp.int32, (PAIR, PAIR), 0) // RWKV_HEAD_DIM
    col = lax.broadcasted_iota(jnp.int32, (PAIR, PAIR), 1) // RWKV_HEAD_DIM
    return jnp.where(row == col, 1.0, 0.0).astype(F32)


def _head_sum(x, ones):
    return _dot_exact_rhs(x, ones, pieces=2)


def _stack(x):
    lane = lax.broadcasted_iota(jnp.int32, x.shape, 1)
    first = lane < RWKV_HEAD_DIM
    return jnp.concatenate([jnp.where(first, x, 0.0), jnp.where(first, 0.0, x)], axis=0)


def _bdot(a, b):
    return jnp.dot(a.astype(BF16), b.astype(BF16), preferred_element_type=F32)


def _wkv_chunk_terms(r, k, v, a_vec, b_vec, lw, between_stages=lambda: None):
    c = CHUNK
    chunks = [slice(c0, c0 + c) for c0 in range(0, r.shape[0], c)]
    row = lax.broadcasted_iota(jnp.int32, (c, c), 0)
    col = lax.broadcasted_iota(jnp.int32, (c, c), 1)
    tril = jnp.where(col <= row, 1.0, 0.0)
    gi = lax.broadcasted_iota(jnp.int32, (4 * c, 4 * c), 0)
    gj = lax.broadcasted_iota(jnp.int32, (4 * c, 4 * c), 1)
    keep = (gj & (c - 1)) < (gi & (c - 1)) + gi // (2 * c)
    eye = jnp.where(lax.broadcasted_iota(jnp.int32, (2 * c, 2 * c), 0) == lax.broadcasted_iota(jnp.int32, (2 * c, 2 * c), 1),
                    1.0, 0.0).astype(F32)

    cum = [_dot_exact_lhs(tril, lw[s]) for s in chunks]
    between_stages()
    last = [x[c - 1:c, :] for x in cum]
    a_t = [_stack(a_vec[s] * jnp.exp(cu - lw[s])) for s, cu in zip(chunks, cum)]
    r_t = [_stack(r[s] * jnp.exp(cu)) for s, cu in zip(chunks, cum)]
    v_s = [_stack(v[s]) for s in chunks]
    xs = [jnp.concatenate([a, rr], axis=0).astype(BF16) for a, rr in zip(a_t, r_t)]
    ys = [jnp.concatenate([_stack(b_vec[s] * jnp.exp(-cu)), _stack(k[s] * jnp.exp(-cu))], axis=0).astype(BF16)
          for s, cu in zip(chunks, cum)]
    g = [jnp.where(keep, lax.dot_general(x, y, (((1,), (1,)), ((), ())), preferred_element_type=F32), 0.0)
         for x, y in zip(xs, ys)]
    between_stages()
    l_ab = [x[:2 * c, :2 * c] for x in g]
    l_ak = [x[:2 * c, 2 * c:] for x in g]
    l_rb = [x[2 * c:, :2 * c] for x in g]
    l_rk = [x[2 * c:, 2 * c:] for x in g]
    t_inv = [eye + x for x in l_ab]
    l_pow = l_ab
    for _ in range(int(math.log2(c)) - 1):
        l_pow = [_bdot(x, x) for x in l_pow]
        between_stages()
        t_inv = [t + _bdot(p, t) for p, t in zip(l_pow, t_inv)]
        between_stages()
    akv = [_bdot(x, y) for x, y in zip(l_ak, v_s)]
    ta = [_bdot(t, jnp.concatenate([a, u], axis=1)) for t, a, u in zip(t_inv, a_t, akv)]
    between_stages()
    rbx = [_bdot(x, y) for x, y in zip(l_rb, ta)]
    rkv = [_bdot(x, y) for x, y in zip(l_rk, v_s)]
    between_stages()
    r_new = [x + y[:, :PAIR] for x, y in zip(r_t, rbx)]
    y0 = [x[:, PAIR:] + y for x, y in zip(rbx, rkv)]
    e_rem = [jnp.exp(la - cu) for la, cu in zip(last, cum)]
    mn = [_bdot(_stack(b_vec[s] * e).T, t) for s, e, t in zip(chunks, e_rem, ta)]
    kv = [_bdot(_stack(k[s] * e).T, vv) for s, e, vv in zip(chunks, e_rem, v_s)]
    m_c = [x[:, :PAIR] + eye * jnp.exp(la) for x, la in zip(mn, last)]
    n_c = [x[:, PAIR:] + y for x, y in zip(mn, kv)]
    return list(zip(r_new, y0, m_c, n_c))


def _token_shift(cur, prev_row, mu):
    rolled = pltpu.roll(cur, 1, 0)
    first = lax.broadcasted_iota(jnp.int32, cur.shape, 0) == 0
    prev = jnp.where(first, prev_row, rolled)
    return cur + (prev - cur) * mu


def _rwkv_transforms(r, k, v, e, w0, a0, k_k, k_a, ww2, wa2, ones):
    w = -_softplus(-(w0 + jnp.dot(jnp.tanh(e).astype(BF16), ww2, preferred_element_type=F32))) - 0.5
    lw = -jnp.exp(w)
    a = _sigmoid(a0 + jnp.dot(e.astype(BF16), wa2, preferred_element_type=F32))
    kk = k * k_k
    kk = kk / jnp.maximum(jnp.sqrt(_head_sum(kk * kk, ones)), 1e-12)
    k_mod = k * (1.0 + (a - 1.0) * k_a)
    return lw, a, kk, k_mod


def _group_norm(y, ln_w, ln_b, ones):
    mean = _head_sum(y, ones) * (1.0 / RWKV_HEAD_DIM)
    d = y - mean
    var = _head_sum(d * d, ones) * (1.0 / RWKV_HEAD_DIM)
    return d * lax.rsqrt(var + GN_EPS) * ln_w + ln_b


def _wkv_prompt_kernel(r_ref, k_ref, v_ref, e_ref, mur_ref, muk_ref, muv_ref, mue_ref, w0_ref, a0_ref,
                       kk_ref, ka_ref, rk_ref, ww2_ref, wa2_ref, lnw_ref, lnb_ref,
                       o_ref, st_ref, pr_ref, pk_ref, pv_ref, pe_ref, s_ref, y_ref,
                       rn_ref, y0_ref, mc_ref, nc_ref, bonus_ref, *, tb, nt):
    n = pl.program_id(0)
    nc = tb // CHUNK
    ones = _pair_ones()

    @pl.when(n == 0)
    def _():
        for ref in (rn_ref, y0_ref, mc_ref, nc_ref, bonus_ref, s_ref):
            ref[...] = jnp.zeros_like(ref)

    @pl.when(lax.rem(n, nt) == 0)
    def _():
        for ref in (pr_ref, pk_ref, pv_ref, pe_ref):
            ref[...] = jnp.zeros_like(ref)

    prev_starts_sequence = lax.rem(n - 1, nt) == 0
    state = [jnp.where(prev_starts_sequence, 0.0, s_ref[...])]
    pending = list(range(nc))

    def recurrence_step():
        if not pending:
            return
        i = pending.pop(0)
        s = state[0]
        ys = _bdot(rn_ref[i], s) + y0_ref[i]
        y_ref[i * CHUNK:(i + 1) * CHUNK, :] = ys[:CHUNK] + ys[CHUNK:]
        state[0] = _bdot(mc_ref[i], s) + nc_ref[i]

    r_cur, k_cur, v_cur, e_cur = r_ref[0, 0], k_ref[0, 0], v_ref[0, 0], e_ref[...]
    r = _token_shift(r_cur, pr_ref[...], mur_ref[...])
    k = _token_shift(k_cur, pk_ref[...], muk_ref[...])
    v = _token_shift(v_cur, pv_ref[...], muv_ref[...])
    e = _token_shift(e_cur, pe_ref[...], mue_ref[...])
    for cur, prev in ((r_cur, pr_ref), (k_cur, pk_ref), (v_cur, pv_ref), (e_cur, pe_ref)):
        prev[...] = cur[tb - 1:tb, :]
    recurrence_step()
    lw, a, kk, k_mod = _rwkv_transforms(r, k, v, e, w0_ref[...], a0_ref[...], kk_ref[...], ka_ref[...],
                                        ww2_ref[...], wa2_ref[...], ones)
    recurrence_step()
    bonus_new = _head_sum(r * k_mod * rk_ref[...], ones) * v
    terms = _wkv_chunk_terms(r, k_mod, v, -kk, kk * a, lw, between_stages=recurrence_step)
    while pending:
        recurrence_step()
    s_ref[...] = state[0]
    st_ref[0, 0] = state[0]
    o_ref[...] = _group_norm(y_ref[...], lnw_ref[...], lnb_ref[...], ones) + bonus_ref[...]

    bonus_ref[...] = bonus_new
    for i, (r_new, y0, m_c, n_c) in enumerate(terms):
        rn_ref[i] = r_new
        y0_ref[i] = y0
        mc_ref[i] = m_c
        nc_ref[i] = n_c


def wkv_prompt(r, k, v, ext, mu_rkv, mu_e, w0, a0, k_k, k_a, r_k, ln_w, ln_b, ww2, wa2, *, tb):
    batch, _, seq, _ = r.shape
    nt = seq // tb
    n_blocks = batch * N_PAIRS * nt

    def where(n):
        return n // (N_PAIRS * nt), (n // nt) % N_PAIRS, n % nt

    cur = lambda n: where(jnp.minimum(n, n_blocks - 1))
    prev = lambda n: where(jnp.maximum(n - 1, 0))
    rows = pl.BlockSpec((1, 1, tb, LANES), lambda n: (cur(n)[0], cur(n)[1], cur(n)[2], 0))
    vec = lambda off: pl.BlockSpec((1, LANES), lambda n: (0, off + cur(n)[1]))
    vec_prev = pl.BlockSpec((1, LANES), lambda n: (0, prev(n)[1]))
    lora = pl.BlockSpec((RWKV_EXTRA, LANES), lambda n: (0, cur(n)[1]))
    terms = pltpu.VMEM((tb // CHUNK, PAIR, PAIR), F32)
    return pl.pallas_call(
        functools.partial(_wkv_prompt_kernel, tb=tb, nt=nt),
        grid=(n_blocks + 1,),
        in_specs=[
            rows, rows, rows,
            pl.BlockSpec((tb, RWKV_EXTRA), lambda n: (cur(n)[0] * nt + cur(n)[2], 0)),
            vec(0), vec(N_PAIRS), vec(2 * N_PAIRS),
            pl.BlockSpec((1, RWKV_EXTRA), lambda n: (0, 0)),
            vec(0), vec(0), vec(0), vec(0), vec(0), lora, lora, vec_prev, vec_prev,
        ],
        out_specs=[
            pl.BlockSpec((tb, LANES), lambda n: (prev(n)[0] * nt + prev(n)[2], prev(n)[1])),
            pl.BlockSpec((1, 1, PAIR, PAIR), lambda n: (prev(n)[0], prev(n)[1], 0, 0)),
        ],
        out_shape=[
            jax.ShapeDtypeStruct((batch * seq, TOK_WIDTH), F32),
            jax.ShapeDtypeStruct((batch, N_PAIRS, PAIR, PAIR), F32),
        ],
        scratch_shapes=[
            pltpu.VMEM((1, LANES), F32), pltpu.VMEM((1, LANES), F32), pltpu.VMEM((1, LANES), F32),
            pltpu.VMEM((1, RWKV_EXTRA), F32), pltpu.VMEM((PAIR, PAIR), F32), pltpu.VMEM((tb, LANES), F32),
            terms, terms, terms, terms, pltpu.VMEM((tb, LANES), F32),
        ],
        compiler_params=_params("arbitrary"),
        name="wkv_prompt",
    )(r, k, v, ext, mu_rkv, mu_rkv, mu_rkv, mu_e, w0, a0, k_k, k_a, r_k, ww2, wa2, ln_w, ln_b)


def _wkv_step_kernel(cur_ref, prev_ref, e_ref, pe_ref, mu_ref, mue_ref, w0_ref, a0_ref, kk_ref, ka_ref, rk_ref,
                     lnw_ref, lnb_ref, ww2_ref, wa2_ref, st_ref, o_ref, stn_ref):
    hd = RWKV_HEAD_DIM
    e_cur = e_ref[0]
    e = e_cur + (pe_ref[0] - e_cur) * mue_ref[...]
    e_rows = jnp.broadcast_to(e, (SUBLANES, RWKV_EXTRA))
    e_tanh = jnp.tanh(e_rows).astype(BF16)
    e_lin = e_rows.astype(BF16)
    eye = lax.broadcasted_iota(jnp.int32, (hd, hd), 0) == lax.broadcasted_iota(jnp.int32, (hd, hd), 1)

    def mixed(i, h):
        cur = cur_ref[0, i, h:h + 1, :]
        return cur + (prev_ref[0, i, h:h + 1, :] - cur) * mu_ref[i, h:h + 1, :]

    for h in range(RWKV_HEADS):
        row = lambda ref: ref[h:h + 1, :]
        r, k, v = mixed(0, h), mixed(1, h), mixed(2, h)
        w = -_softplus(-(row(w0_ref) + jnp.dot(e_tanh, ww2_ref[h], preferred_element_type=F32)[0:1])) - 0.5
        decay = jnp.exp(-jnp.exp(w))
        a = _sigmoid(row(a0_ref) + jnp.dot(e_lin, wa2_ref[h], preferred_element_type=F32)[0:1])
        kk = k * row(kk_ref)
        kk = kk / jnp.maximum(jnp.sqrt(jnp.sum(kk * kk, axis=-1, keepdims=True)), 1e-12)
        k_mod = k * (1.0 + (a - 1.0) * row(ka_ref))
        s = st_ref[0, h]
        sa = jnp.sum(s * (-kk), axis=-1, keepdims=True)
        v_col = jnp.sum(jnp.where(eye, jnp.broadcast_to(v, (hd, hd)), 0.0), axis=-1, keepdims=True)
        s_new = s * decay + sa * (kk * a) + v_col * k_mod
        stn_ref[0, h] = s_new
        y_col = jnp.sum(s_new * r, axis=-1, keepdims=True)
        y = jnp.sum(jnp.where(eye, jnp.broadcast_to(y_col, (hd, hd)), 0.0), axis=0, keepdims=True)
        mean = jnp.mean(y, axis=-1, keepdims=True)
        d = y - mean
        var = jnp.mean(d * d, axis=-1, keepdims=True)
        yn = d * lax.rsqrt(var + GN_EPS) * row(lnw_ref) + row(lnb_ref)
        o_ref[0, h:h + 1, :] = yn + jnp.sum(r * k_mod * row(rk_ref), axis=-1, keepdims=True) * v


def wkv_step(cur, prev, e_cur, e_prev, mu, mu_e, w0, a0, k_k, k_a, r_k, ln_w, ln_b, ww2h, wa2h, state):
    batch = cur.shape[0]
    hshape = (RWKV_HEADS, RWKV_HEAD_DIM)
    per_row = lambda shape: pl.BlockSpec((1,) + shape, lambda b: (b,) + (0,) * len(shape))
    const = lambda shape: pl.BlockSpec(shape, lambda b: (0,) * len(shape))
    return pl.pallas_call(
        _wkv_step_kernel,
        grid=(batch,),
        in_specs=[per_row((3,) + hshape), per_row((3,) + hshape), per_row((1, RWKV_EXTRA)), per_row((1, RWKV_EXTRA)),
                  const((3,) + hshape), const((1, RWKV_EXTRA))] + [const(hshape)] * 7
                 + [const((RWKV_HEADS, RWKV_EXTRA, RWKV_HEAD_DIM))] * 2 + [per_row(hshape + (RWKV_HEAD_DIM,))],
        out_specs=[per_row(hshape), per_row(hshape + (RWKV_HEAD_DIM,))],
        out_shape=[jax.ShapeDtypeStruct((batch,) + hshape, F32),
                   jax.ShapeDtypeStruct((batch,) + hshape + (RWKV_HEAD_DIM,), F32)],
        compiler_params=_params("parallel"),
        name="wkv_step",
    )(cur, prev, e_cur, e_prev, mu, mu_e, w0, a0, k_k, k_a, r_k, ln_w, ln_b, ww2h, wa2h, state)


def _relayout_in_proj(w_in, tok_cols, extra_cols, extra_width):
    w_t = jnp.transpose(w_in)
    tok = w_t[:tok_cols]
    extra = w_t[tok_cols:tok_cols + extra_cols]
    q_mem = w_t[tok_cols + extra_cols:tok_cols + extra_cols + MEM_WIDTH]
    gate = w_t[tok_cols + extra_cols + MEM_WIDTH:]
    main = jnp.concatenate([gate, tok, q_mem], axis=0).astype(BF16)
    extra = jnp.pad(extra, ((0, extra_width - extra_cols), (0, 0))).astype(BF16)
    return main, extra


def _pad_lanes(v, width):
    return jnp.pad(v, (0, width - v.shape[0])).reshape(1, width)


def _split_w_o(w_o):
    return w_o[:TOK_WIDTH].astype(BF16), w_o[TOK_WIDTH:].astype(BF16)


def _memory_kv(mem, g_mem, w_mem):
    kv = norm_proj(mem, g_mem, jnp.transpose(w_mem).astype(BF16), tm=mem.shape[0])
    return kv[:, :MEM_WIDTH], kv[:, MEM_WIDTH:]


def kernel(x_prompt, x_sample, cache_fox_k, cache_fox_v, cache_fox_logf, cache_mem_k, cache_mem_v, state_rwkv_wkv, state_rwkv_shift, page_table, mem_prompt, g_pre, g_post, g_mem, w_mem, w_o, fox_w_in, fox_b_f, rwkv_w_in, rwkv_mu, rwkv_w0, rwkv_w_w2, rwkv_a0, rwkv_w_a2, rwkv_k_k, rwkv_k_a, rwkv_r_k, rwkv_ln_w, rwkv_ln_b):
    bp, seq, _ = x_prompt.shape
    bs = x_sample.shape[0]
    xp = x_prompt.reshape(bp * seq, D_MODEL)
    xs = x_sample.reshape(bs, D_MODEL)
    mem = mem_prompt.reshape(bp * N_MEM, D_MODEL)
    tm_p = 1024
    mem_shape = (bp, N_MEM, MEM_HEADS, MEM_HEAD_DIM)

    w_main, w_ext = _relayout_in_proj(fox_w_in[0], 3 * TOK_WIDTH, FOX_HEADS, FOX_EXTRA)
    w_tok, w_mo = _split_w_o(w_o[0])
    b_pad = _pad_lanes(fox_b_f[0], LANES)
    mk0, mv0 = _memory_kv(mem, g_mem[0], w_mem[0])

    gm_p, q_p, k_p, v_p, ext_p = norm_proj(xp, g_pre[0], w_main, w_ext, tm=tm_p, heads=(bp, seq))
    logf_p, ccol, crow = fox_gate(ext_p, b_pad, batch=bp, seq=seq, tb=256)
    o_tok = fox_flash(q_p, k_p, v_p, ccol, crow, tq=512)
    o_mem = mem_attn(gm_p.reshape(bp, seq, GM_WIDTH), mk0, mv0, tq=1024, q_block=D_MODEL // MEM_WIDTH)
    xp = out_proj(o_tok, o_mem.reshape(bp * seq, MEM_WIDTH), gm_p, xp, w_tok, w_mo, g_post[0], tm=256)

    main_s, ext_s = norm_proj(xs, g_pre[0], w_main, w_ext, tm=bs)
    q_s, k_s, v_s = (main_s[:, off:off + TOK_WIDTH] for off in (TOKA_OFF, TOKB_OFF, TOKC_OFF))
    logf_pool_t = jnp.pad(jnp.transpose(cache_fox_logf[0], (0, 2, 1)), ((0, 0), (0, HEAD_PAD - FOX_HEADS), (0, 0)))
    k_pool, v_pool = (jnp.transpose(c, (0, 1, 3, 2, 4)) for c in (cache_fox_k, cache_fox_v))
    o_tok_s, logf_s = fox_decode(page_table, q_s, k_s, v_s, ext_s, b_pad, k_pool, v_pool, logf_pool_t, layer=0)
    o_mem_s = mem_attn(main_s.reshape(bs, 1, MAIN_WIDTH), cache_mem_k[0].reshape(bs * N_MEM, MEM_WIDTH),
                       cache_mem_v[0].reshape(bs * N_MEM, MEM_WIDTH), tq=1, q_block=QMEM_OFF // MEM_WIDTH)
    xs = out_proj(o_tok_s, o_mem_s.reshape(bs, MEM_WIDTH), main_s, xs, w_tok, w_mo, g_post[0], tm=bs)

    head_shape = (FOX_HEADS, FOX_HEAD_DIM)
    fox_k_p = jnp.transpose(k_p, (0, 2, 1, 3))[None]
    fox_v_p = jnp.transpose(v_p, (0, 2, 1, 3))[None]
    fox_f_p = logf_p[:, :FOX_HEADS].reshape(1, bp, seq, FOX_HEADS)
    fox_k_s = k_s.reshape((1, bs, 1) + head_shape)
    fox_v_s = v_s.reshape((1, bs, 1) + head_shape)
    fox_f_s = logf_s[:, :FOX_HEADS].reshape(1, bs, 1, FOX_HEADS)

    w_main, w_ext = _relayout_in_proj(rwkv_w_in[0], 3 * TOK_WIDTH, 2 * LORA, RWKV_EXTRA)
    w_tok, w_mo = _split_w_o(w_o[1])
    mk1, mv1 = _memory_kv(mem, g_mem[1], w_mem[1])
    mu_rkv = rwkv_mu[0][:3 * TOK_WIDTH].reshape(1, 3 * TOK_WIDTH)
    mu_e = _pad_lanes(rwkv_mu[0][3 * TOK_WIDTH:], RWKV_EXTRA)
    ww2 = jnp.pad(rwkv_w_w2[0], ((0, RWKV_EXTRA - LORA), (0, 0))).astype(BF16)
    wa2 = jnp.pad(rwkv_w_a2[0], ((LORA, RWKV_EXTRA - 2 * LORA), (0, 0))).astype(BF16)
    vecs = [p[0] for p in (rwkv_w0, rwkv_a0, rwkv_k_k, rwkv_k_a, rwkv_r_k, rwkv_ln_w, rwkv_ln_b)]

    gm_p, r_p, k_p, v_p, ext_p = norm_proj(xp, g_pre[1], w_main, w_ext, tm=tm_p, heads=(bp, seq))
    o_tok, st_pairs = wkv_prompt(r_p, k_p, v_p, ext_p, mu_rkv, mu_e, *[v.reshape(1, TOK_WIDTH) for v in vecs], ww2, wa2,
                                 tb=512)
    o_mem = mem_attn(gm_p.reshape(bp, seq, GM_WIDTH), mk1, mv1, tq=1024, q_block=D_MODEL // MEM_WIDTH)
    xp = out_proj(o_tok, o_mem.reshape(bp * seq, MEM_WIDTH), gm_p, xp, w_tok, w_mo, g_post[1], tm=256)

    main_s, ext_s = norm_proj(xs, g_pre[1], w_main, w_ext, tm=bs)
    hshape = (RWKV_HEADS, RWKV_HEAD_DIM)
    shift_prev = state_rwkv_shift[0][:, 0, :]
    cur = main_s[:, TOKA_OFF:QMEM_OFF].reshape((bs, 3) + hshape)
    prev = shift_prev[:, :3 * TOK_WIDTH].reshape((bs, 3) + hshape)
    e_prev = jnp.pad(shift_prev[:, 3 * TOK_WIDTH:], ((0, 0), (0, RWKV_EXTRA - 2 * LORA))).reshape(bs, 1, RWKV_EXTRA)
    per_head = lambda w: jnp.transpose(w.reshape((RWKV_EXTRA,) + hshape), (1, 0, 2))
    o_tok_s, wkv_s = wkv_step(cur, prev, ext_s.reshape(bs, 1, RWKV_EXTRA), e_prev, mu_rkv.reshape((3,) + hshape), mu_e,
                              *[v.reshape(hshape) for v in vecs], per_head(ww2), per_head(wa2), state_rwkv_wkv[0])
    o_mem_s = mem_attn(main_s.reshape(bs, 1, MAIN_WIDTH), cache_mem_k[1].reshape(bs * N_MEM, MEM_WIDTH),
                       cache_mem_v[1].reshape(bs * N_MEM, MEM_WIDTH), tq=1, q_block=QMEM_OFF // MEM_WIDTH)
    xs = out_proj(o_tok_s.reshape(bs, TOK_WIDTH), o_mem_s.reshape(bs, MEM_WIDTH), main_s, xs, w_tok, w_mo, g_post[1], tm=bs)

    st6 = st_pairs.reshape(bp, N_PAIRS, 2, RWKV_HEAD_DIM, 2, RWKV_HEAD_DIM)
    st_heads = jnp.stack([st6[:, :, 0, :, 0, :], st6[:, :, 1, :, 1, :]], axis=2)
    wkv_p = jnp.swapaxes(st_heads.reshape((bp,) + hshape + (RWKV_HEAD_DIM,)), -1, -2)
    last_p = jnp.concatenate([a[:, :, seq - 1, :].reshape(bp, TOK_WIDTH) for a in (r_p, k_p, v_p)], axis=-1)
    last_e = ext_p.reshape(bp, seq, RWKV_EXTRA)[:, seq - 1, :2 * LORA]
    shift_p = jnp.concatenate([last_p, last_e], axis=-1).reshape(1, bp, 1, 3 * TOK_WIDTH + 2 * LORA)
    shift_s = jnp.concatenate([main_s[:, TOKA_OFF:QMEM_OFF], ext_s[:, :2 * LORA]], axis=-1).reshape(1, bs, 1, 3 * TOK_WIDTH + 2 * LORA)

    return (xp.reshape(bp, seq, D_MODEL), xs.reshape(bs, 1, D_MODEL),
            jnp.stack([mk0.reshape(mem_shape), mk1.reshape(mem_shape)]),
            jnp.stack([mv0.reshape(mem_shape), mv1.reshape(mem_shape)]),
            fox_k_p, fox_v_p, fox_f_p, fox_k_s, fox_v_s, fox_f_s,
            wkv_p[None], shift_p, wkv_s[None], shift_s)
```

```python
import functools
import math

import jax
import jax.numpy as jnp
from jax import lax
from jax.experimental import pallas as pl
from jax.experimental.pallas import tpu as pltpu

F32 = jnp.float32
BF16 = jnp.bfloat16

D_MODEL = 2048
N_MEM = 256
MEM_HEADS = 4
MEM_HEAD_DIM = 128
MEM_WIDTH = MEM_HEADS * MEM_HEAD_DIM
TOK_WIDTH = D_MODEL - MEM_WIDTH
FOX_HEAD_DIM = 128
FOX_HEADS = TOK_WIDTH // FOX_HEAD_DIM
RWKV_HEAD_DIM = 64
RWKV_HEADS = TOK_WIDTH // RWKV_HEAD_DIM
LORA = 96
PAGE_SIZE = 128
RMS_EPS = 1e-6
GN_EPS = 64e-5

LANES = 128
SUBLANES = 8
V7X_VMEM_BYTES = 64 * 1024 * 1024
VMEM_LIMIT_BYTES = V7X_VMEM_BYTES - 8 * 1024 * 1024

GATE_OFF = 0
TOKA_OFF = D_MODEL
TOKB_OFF = TOKA_OFF + TOK_WIDTH
TOKC_OFF = TOKB_OFF + TOK_WIDTH
QMEM_OFF = TOKC_OFF + TOK_WIDTH
MAIN_WIDTH = QMEM_OFF + MEM_WIDTH
FOX_EXTRA = LANES
RWKV_EXTRA = 2 * LANES

HEAD_PAD = 16
CHUNK = 64


def _params(*sem):
    return pltpu.CompilerParams(dimension_semantics=sem, vmem_limit_bytes=VMEM_LIMIT_BYTES)


def _rms_scale(x):
    return lax.rsqrt(jnp.mean(x * x, axis=-1, keepdims=True) + RMS_EPS)


def _sigmoid(x):
    return 1.0 / (1.0 + jnp.exp(-x))


def _softplus(x):
    return jnp.maximum(x, 0.0) + jnp.log(1.0 + jnp.exp(-jnp.abs(x)))


def _split(x, pieces):
    out = []
    for _ in range(pieces - 1):
        p = x.astype(BF16)
        out.append(p)
        x = x - p.astype(F32)
    return out + [x.astype(BF16)]


def _dot_exact_rhs(x, rhs, pieces=3):
    rows = x.shape[0]
    parts = jnp.dot(jnp.concatenate(_split(x, pieces), axis=0), rhs.astype(BF16), preferred_element_type=F32)
    return sum(parts[i * rows:(i + 1) * rows] for i in range(pieces))


def _dot_exact_lhs(lhs, x, pieces=3):
    cols = x.shape[1]
    parts = jnp.dot(lhs.astype(BF16), jnp.concatenate(_split(x, pieces), axis=1), preferred_element_type=F32)
    return sum(parts[:, i * cols:(i + 1) * cols] for i in range(pieces))


TN = 4 * LANES
GATE_BLOCKS = D_MODEL // TN
TOK_BLOCKS = TOK_WIDTH // TN
QMEM_BLOCK = GATE_BLOCKS + 3 * TOK_BLOCKS
GM_WIDTH = D_MODEL + MEM_WIDTH
N_GROUPS128 = TOK_WIDTH // LANES
assert MAIN_WIDTH == (QMEM_BLOCK + 1) * TN and MEM_WIDTH == TN


def _nt_dot(a, b):
    return lax.dot_general(a, b, (((1,), (1,)), ((), ())), preferred_element_type=F32)


def _norm_proj_kernel(x_ref, g_ref, wt_ref, *rest, has_gm, has_extra, head_mode):
    rest = list(rest)
    wg_ref = rest.pop(0) if has_gm else None
    we_ref = rest.pop(0) if has_extra else None
    h_ref = rest.pop()
    oe_ref = rest.pop() if has_extra else None
    j = pl.program_id(1)

    @pl.when(j == 0)
    def _():
        x = x_ref[...]
        h = (x * _rms_scale(x) * g_ref[...]).astype(BF16)
        h_ref[...] = h
        if has_extra:
            oe_ref[...] = _nt_dot(h, we_ref[...])

    if not has_gm:
        rest[0][...] = _nt_dot(h_ref[...], wt_ref[...])
        return
    in_gm = (j < GATE_BLOCKS) | (j == QMEM_BLOCK)

    @pl.when(in_gm)
    def _():
        rest[0][...] = _nt_dot(h_ref[...], wg_ref[...])

    if not head_mode:
        @pl.when(jnp.logical_not(in_gm))
        def _():
            rest[0][...] = _nt_dot(h_ref[...], wt_ref[...])
        return

    for gi, ref in enumerate(rest[1:]):
        first = GATE_BLOCKS + gi * TOK_BLOCKS

        @pl.when((j >= first) & (j < first + TOK_BLOCKS))
        def _(ref=ref):
            res = _nt_dot(h_ref[...], wt_ref[...])
            for hh in range(TN // LANES):
                ref[0, hh] = res[:, hh * LANES:(hh + 1) * LANES]


def norm_proj(x, g, w_tok_t, w_gm_t=None, w_extra_t=None, *, tm, heads=None):
    m, d = x.shape
    has_gm = w_gm_t is not None
    has_extra = w_extra_t is not None
    n = MAIN_WIDTH if has_gm else w_tok_t.shape[0]
    assert m % tm == 0 and n % TN == 0 and (has_gm or heads is None)
    gm_col = lambda j: jnp.where(j == QMEM_BLOCK, GATE_BLOCKS, jnp.minimum(j, GATE_BLOCKS - 1))
    tok_row = (lambda j: jnp.clip(j - GATE_BLOCKS, 0, 3 * TOK_BLOCKS - 1)) if has_gm else (lambda j: j)
    in_specs = [
        pl.BlockSpec((tm, d), lambda i, j: (i, 0)),
        pl.BlockSpec((1, d), lambda i, j: (0, 0)),
        pl.BlockSpec((TN, d), lambda i, j: (tok_row(j), 0)),
    ]
    args = [x, g.reshape(1, d), w_tok_t]
    if has_gm:
        in_specs.append(pl.BlockSpec((TN, d), lambda i, j: (gm_col(j), 0)))
        args.append(w_gm_t)
    if has_extra:
        in_specs.append(pl.BlockSpec(w_extra_t.shape, lambda i, j: (0, 0)))
        args.append(w_extra_t)
    if heads is None:
        out_specs = [pl.BlockSpec((tm, TN), lambda i, j: (i, j))]
        out_shape = [jax.ShapeDtypeStruct((m, n), F32)]
    else:
        batch, seq = heads
        assert seq % tm == 0
        per_seq = seq // tm
        out_specs = [pl.BlockSpec((tm, TN), lambda i, j: (i, gm_col(j)))]
        out_shape = [jax.ShapeDtypeStruct((m, GM_WIDTH), F32)]
        for gi in range(3):
            first = GATE_BLOCKS + gi * TOK_BLOCKS
            out_specs.append(pl.BlockSpec(
                (1, TN // LANES, tm, LANES),
                lambda i, j, first=first: (i // per_seq, jnp.clip(j - first, 0, TOK_BLOCKS - 1), i % per_seq, 0)))
            out_shape.append(jax.ShapeDtypeStruct((batch, N_GROUPS128, seq, LANES), F32))
    if has_extra:
        e = w_extra_t.shape[0]
        out_specs.append(pl.BlockSpec((tm, e), lambda i, j: (i, 0)))
        out_shape.append(jax.ShapeDtypeStruct((m, e), F32))
    outs = pl.pallas_call(
        functools.partial(_norm_proj_kernel, has_gm=has_gm, has_extra=has_extra, head_mode=heads is not None),
        grid=(m // tm, n // TN),
        in_specs=in_specs,
        out_specs=out_specs,
        out_shape=out_shape,
        scratch_shapes=[pltpu.VMEM((tm, d), BF16)],
        compiler_params=_params("parallel", "arbitrary"),
        name="norm_proj",
    )(*args)
    return outs if len(outs) > 1 else outs[0]


def _fox_gate_kernel(f_ref, b_ref, logf_ref, ccol_ref, crow_ref, carry_ref, *, tb):
    @pl.when(pl.program_id(1) == 0)
    def _():
        carry_ref[...] = jnp.zeros_like(carry_ref)

    logf = -_softplus(-(f_ref[...] + b_ref[...]))
    logf_ref[...] = logf
    row = lax.broadcasted_iota(jnp.int32, (tb, tb), 0)
    col = lax.broadcasted_iota(jnp.int32, (tb, tb), 1)
    c = _dot_exact_lhs(jnp.where(col <= row, 1.0, 0.0), logf) + carry_ref[...]
    ccol_ref[...] = c
    crow_ref[0] = c.T[:HEAD_PAD]
    carry_ref[...] = c[tb - 1:tb, :]


def fox_gate(f_ext, b_pad, *, batch, seq, tb):
    nt = seq // tb
    row_spec = pl.BlockSpec((tb, LANES), lambda b, t: (b * nt + t, 0))
    return pl.pallas_call(
        functools.partial(_fox_gate_kernel, tb=tb),
        grid=(batch, nt),
        in_specs=[row_spec, pl.BlockSpec((1, LANES), lambda b, t: (0, 0))],
        out_specs=[row_spec, row_spec, pl.BlockSpec((1, HEAD_PAD, tb), lambda b, t: (b, 0, t))],
        out_shape=[
            jax.ShapeDtypeStruct((batch * seq, LANES), F32),
            jax.ShapeDtypeStruct((batch * seq, LANES), F32),
            jax.ShapeDtypeStruct((batch, HEAD_PAD, seq), F32),
        ],
        scratch_shapes=[pltpu.VMEM((1, LANES), F32)],
        compiler_params=_params("parallel", "arbitrary"),
        name="fox_gate",
    )(f_ext, b_pad)


_MASKED = -1e30
LOG2E = 1.4426950408889634


def _fox_flash_kernel(q_ref, k_ref, v_ref, ccol_ref, crow_ref, o_ref, kb_ref, vb_ref, *, tq):
    h = pl.program_id(1)
    qi = pl.program_id(2)

    @pl.when(qi == 0)
    def _():
        kb_ref[...] = k_ref[0, 0].astype(BF16)
        vb_ref[...] = v_ref[0, 0].astype(BF16)

    q = (q_ref[0, 0] * (FOX_HEAD_DIM ** -0.5 * LOG2E)).astype(BF16)
    lane = lax.broadcasted_iota(jnp.int32, (tq, LANES), 1)
    cq = jnp.sum(jnp.where(lane == h, ccol_ref[...], 0.0), axis=-1, keepdims=True) * LOG2E

    def step(c, carry, diagonal):
        m, l, acc = carry
        start = pl.multiple_of(c * tq, tq)
        kc = kb_ref[pl.ds(start, tq), :]
        vc = vb_ref[pl.ds(start, tq), :]
        ck = crow_ref[0, 0, :, pl.ds(start, tq)] * LOG2E
        s = lax.dot_general(q, kc, (((1,), (1,)), ((), ())), preferred_element_type=F32) - ck
        if diagonal:
            row = lax.broadcasted_iota(jnp.int32, (tq, tq), 0)
            col = lax.broadcasted_iota(jnp.int32, (tq, tq), 1)
            s = jnp.where(col <= row, s, _MASKED)
        m_new = jnp.maximum(m, jnp.max(s, axis=-1, keepdims=True) + cq)
        p = jnp.exp2(s - (m_new - cq))
        alpha = jnp.exp2(m - m_new)
        l = alpha * l + jnp.sum(p, axis=-1, keepdims=True)
        acc = alpha * acc + jnp.dot(p.astype(BF16), vc, preferred_element_type=F32)
        return m_new, l, acc

    init = (jnp.full((tq, 1), _MASKED, F32), jnp.zeros((tq, 1), F32), jnp.zeros((tq, FOX_HEAD_DIM), F32))
    carry = lax.fori_loop(0, qi, lambda c, cr: step(c, cr, False), init)
    _, l, acc = step(qi, carry, True)
    o_ref[...] = acc / l


def fox_flash(q, k, v, ccol, crow, *, tq):
    batch, _, seq, _ = q.shape
    nq = seq // tq
    crow4 = crow.reshape(batch, HEAD_PAD, 1, seq)
    kv_spec = pl.BlockSpec((1, 1, seq, FOX_HEAD_DIM), lambda b, h, i: (b, h, 0, 0))
    return pl.pallas_call(
        functools.partial(_fox_flash_kernel, tq=tq),
        grid=(batch, FOX_HEADS, nq),
        in_specs=[
            pl.BlockSpec((1, 1, tq, FOX_HEAD_DIM), lambda b, h, i: (b, h, i, 0)),
            kv_spec, kv_spec,
            pl.BlockSpec((tq, LANES), lambda b, h, i: (b * nq + i, 0)),
            pl.BlockSpec((1, 1, 1, seq), lambda b, h, i: (b, h, 0, 0)),
        ],
        out_specs=pl.BlockSpec((tq, LANES), lambda b, h, i: (b * nq + i, h)),
        out_shape=jax.ShapeDtypeStruct((batch * seq, TOK_WIDTH), F32),
        scratch_shapes=[pltpu.VMEM((seq, LANES), BF16), pltpu.VMEM((seq, LANES), BF16)],
        compiler_params=_params("parallel", "parallel", "arbitrary"),
        name="fox_flash",
    )(q, k, v, ccol, crow4)


PAGES_PER_STEP = 8


def _head_mask():
    row = lax.broadcasted_iota(jnp.int32, (HEAD_PAD, TOK_WIDTH), 0)
    col = lax.broadcasted_iota(jnp.int32, (HEAD_PAD, TOK_WIDTH), 1)
    lo = row * FOX_HEAD_DIM
    return (col >= lo) & (col < lo + FOX_HEAD_DIM)


def _fox_decode_kernel(pt_ref, q_ref, kn_ref, vn_ref, f_ref, bf_ref, *rest):
    del pt_ref
    pps = PAGES_PER_STEP
    lf_refs, k_refs, v_refs = rest[:pps], rest[pps:2 * pps], rest[2 * pps:3 * pps]
    o_ref, lfn_ref, qbd_ref, m_ref, l_ref, acc_ref, carry_ref = rest[3 * pps:]
    b = pl.program_id(0)
    j = pl.program_id(1)
    scale = FOX_HEAD_DIM ** -0.5
    hmask = _head_mask()

    @pl.when(j == 0)
    def _():
        q = q_ref[pl.ds(b, 1), :]
        q_bd = jnp.where(hmask, jnp.broadcast_to(q, (HEAD_PAD, TOK_WIDTH)), 0.0)
        qbd_ref[...] = q_bd.astype(BF16)
        logf_new = -_softplus(-(f_ref[pl.ds(b, 1), :] + bf_ref[...]))
        lfn_ref[pl.ds(b, 1), :] = logf_new
        eye = lax.broadcasted_iota(jnp.int32, (HEAD_PAD, LANES), 0) == lax.broadcasted_iota(jnp.int32, (HEAD_PAD, LANES), 1)
        carry_ref[...] = jnp.sum(jnp.where(eye, jnp.broadcast_to(logf_new, (HEAD_PAD, LANES)), 0.0), axis=-1, keepdims=True)
        m_ref[...] = jnp.sum(q_bd * kn_ref[pl.ds(b, 1), :], axis=-1, keepdims=True) * scale
        l_ref[...] = jnp.ones_like(l_ref)
        acc_ref[...] = jnp.broadcast_to(vn_ref[pl.ds(b, 1), :], (HEAD_PAD, TOK_WIDTH))

    def page_rows(ref):
        return jnp.concatenate([ref[0, 0, h].astype(BF16) for h in range(FOX_HEADS)], axis=1)

    lfs = [ref[0] for ref in lf_refs]
    src = lax.broadcasted_iota(jnp.int32, (PAGE_SIZE, PAGE_SIZE), 0)
    dst = lax.broadcasted_iota(jnp.int32, (PAGE_SIZE, PAGE_SIZE), 1)
    inside = _dot_exact_rhs(jnp.concatenate(lfs, axis=0), jnp.where(src > dst, 1.0, 0.0))
    carries = [carry_ref[...]]
    for lf in lfs:
        carries.append(carries[-1] + jnp.sum(lf, axis=-1, keepdims=True))
    carry_ref[...] = carries[-1]

    q_bd = qbd_ref[...]
    scores = []
    for i in range(pps):
        s = lax.dot_general(q_bd, page_rows(k_refs[i]), (((1,), (1,)), ((), ())), preferred_element_type=F32)
        scores.append(s * scale + (inside[i * HEAD_PAD:(i + 1) * HEAD_PAD] + carries[i]))
    m_old = m_ref[...]
    m_new = m_old
    for s in scores:
        m_new = jnp.maximum(m_new, jnp.max(s, axis=-1, keepdims=True))
    alpha = jnp.exp(m_old - m_new)
    l_new = alpha * l_ref[...]
    acc = alpha * acc_ref[...]
    for i, s in enumerate(scores):
        p = jnp.exp(s - m_new)
        l_new = l_new + jnp.sum(p, axis=-1, keepdims=True)
        acc = acc + jnp.dot(p.astype(BF16), page_rows(v_refs[i]), preferred_element_type=F32)
    m_ref[...] = m_new
    l_ref[...] = l_new
    acc_ref[...] = acc

    @pl.when(j == pl.num_programs(1) - 1)
    def _():
        out = jnp.where(hmask, acc / l_new, 0.0)
        o_ref[pl.ds(b, 1), :] = jnp.sum(out, axis=0, keepdims=True)


def fox_decode(page_table, q, k_new, v_new, f_ext, b_pad, k_pool, v_pool, logf_pool_t, *, layer):
    batch, n_pages = page_table.shape
    pps = PAGES_PER_STEP
    assert n_pages % pps == 0
    whole = lambda shape: pl.BlockSpec(shape, lambda b, j, pt: (0,) * len(shape))
    page_id = lambda i: (lambda b, j, pt: pt[b, n_pages - 1 - j * pps - i])
    lf_spec = lambda i: pl.BlockSpec((1, HEAD_PAD, PAGE_SIZE), lambda b, j, pt: (page_id(i)(b, j, pt), 0, 0))
    kv_spec = lambda i: pl.BlockSpec((1, 1, FOX_HEADS, PAGE_SIZE, FOX_HEAD_DIM),
                                     lambda b, j, pt: (layer, page_id(i)(b, j, pt), 0, 0, 0))
    grid_spec = pltpu.PrefetchScalarGridSpec(
        num_scalar_prefetch=1,
        grid=(batch, n_pages // pps),
        in_specs=[whole((batch, TOK_WIDTH))] * 3 + [whole((batch, LANES)), whole((1, LANES))]
                 + [lf_spec(i) for i in range(pps)] + [kv_spec(i) for i in range(pps)] * 2,
        out_specs=[whole((batch, TOK_WIDTH)), whole((batch, LANES))],
        scratch_shapes=[
            pltpu.VMEM((HEAD_PAD, TOK_WIDTH), BF16),
            pltpu.VMEM((HEAD_PAD, 1), F32),
            pltpu.VMEM((HEAD_PAD, 1), F32),
            pltpu.VMEM((HEAD_PAD, TOK_WIDTH), F32),
            pltpu.VMEM((HEAD_PAD, 1), F32),
        ],
    )
    return pl.pallas_call(
        _fox_decode_kernel,
        grid_spec=grid_spec,
        out_shape=[jax.ShapeDtypeStruct((batch, TOK_WIDTH), F32), jax.ShapeDtypeStruct((batch, LANES), F32)],
        compiler_params=_params("arbitrary", "arbitrary"),
        name="fox_decode",
    )(page_table, q, k_new, v_new, f_ext, b_pad, *([logf_pool_t] * pps), *([k_pool] * pps), *([v_pool] * pps))


def _mem_attn_kernel(q_ref, k_ref, v_ref, o_ref, *, tq):
    rows = max(tq, SUBLANES)
    scale = MEM_HEAD_DIM ** -0.5
    for h in range(MEM_HEADS):
        sl = slice(h * MEM_HEAD_DIM, (h + 1) * MEM_HEAD_DIM)
        qh = jnp.broadcast_to(q_ref[0, :, sl], (rows, MEM_HEAD_DIM)).astype(BF16)
        kh = k_ref[:, sl].astype(BF16)
        vh = v_ref[:, sl].astype(BF16)
        s = lax.dot_general(qh, kh, (((1,), (1,)), ((), ())), preferred_element_type=F32) * scale
        e = jnp.exp(s - jnp.max(s, axis=-1, keepdims=True))
        oh = jnp.dot(e.astype(BF16), vh, preferred_element_type=F32) / jnp.sum(e, axis=-1, keepdims=True)
        o_ref[0, :, sl] = oh[:tq]


def mem_attn(main3, mem_k, mem_v, *, tq, q_block):
    batch, seq, _ = main3.shape
    kv_spec = pl.BlockSpec((N_MEM, MEM_WIDTH), lambda b, i: (b, 0))
    return pl.pallas_call(
        functools.partial(_mem_attn_kernel, tq=tq),
        grid=(batch, seq // tq),
        in_specs=[pl.BlockSpec((1, tq, MEM_WIDTH), lambda b, i: (b, i, q_block)), kv_spec, kv_spec],
        out_specs=pl.BlockSpec((1, tq, MEM_WIDTH), lambda b, i: (b, i, 0)),
        out_shape=jax.ShapeDtypeStruct((batch, seq, MEM_WIDTH), F32),
        compiler_params=_params("parallel", "parallel"),
        name="mem_attn",
    )(main3, mem_k, mem_v)


def _out_proj_kernel(ot_ref, om_ref, gate_ref, x_ref, wt_ref, wm_ref, gp_ref, y_ref):
    gate = gate_ref[...]
    sg = gate * _sigmoid(gate)
    a = (ot_ref[...] * sg[:, :TOK_WIDTH]).astype(BF16)
    b = (om_ref[...] * sg[:, TOK_WIDTH:]).astype(BF16)
    y = jnp.dot(a, wt_ref[...], preferred_element_type=F32) + jnp.dot(b, wm_ref[...], preferred_element_type=F32)
    y_ref[...] = x_ref[...] + y * _rms_scale(y) * gp_ref[...]


def out_proj(o_tok, o_mem, main, x, w_tok, w_mem, g_post, *, tm):
    m = x.shape[0]
    row = lambda width: pl.BlockSpec((tm, width), lambda i: (i, 0))
    const = lambda shape: pl.BlockSpec(shape, lambda i: (0, 0))
    return pl.pallas_call(
        _out_proj_kernel,
        grid=(m // tm,),
        in_specs=[row(TOK_WIDTH), row(MEM_WIDTH), row(D_MODEL), row(D_MODEL),
                  const((TOK_WIDTH, D_MODEL)), const((MEM_WIDTH, D_MODEL)), const((1, D_MODEL))],
        out_specs=row(D_MODEL),
        out_shape=jax.ShapeDtypeStruct((m, D_MODEL), F32),
        compiler_params=_params("parallel"),
        name="out_proj",
    )(o_tok, o_mem, main, x, w_tok, w_mem, g_post.reshape(1, D_MODEL))


PAIR = 2 * RWKV_HEAD_DIM
assert PAIR == LANES
N_PAIRS = RWKV_HEADS // 2


def _pair_ones():
    row = lax.broadcasted_iota(jnp.int32, (PAIR, PAIR), 0) // RWKV_HEAD_DIM
    col = lax.broadcasted_iota(jnp.int32, (PAIR, PAIR), 1) // RWKV_HEAD_DIM
    return jnp.where(row == col, 1.0, 0.0).astype(F32)


def _head_sum(x, ones):
    return _dot_exact_rhs(x, ones, pieces=2)


def _stack(x):
    lane = lax.broadcasted_iota(jnp.int32, x.shape, 1)
    first = lane < RWKV_HEAD_DIM
    return jnp.concatenate([jnp.where(first, x, 0.0), jnp.where(first, 0.0, x)], axis=0)


def _bdot(a, b):
    return jnp.dot(a.astype(BF16), b.astype(BF16), preferred_element_type=F32)


def _wkv_chunk_terms(r, k, v, a_vec, b_vec, lw, between_stages=lambda: None):
    c = CHUNK
    chunks = [slice(c0, c0 + c) for c0 in range(0, r.shape[0], c)]
    row = lax.broadcasted_iota(jnp.int32, (c, c), 0)
    col = lax.broadcasted_iota(jnp.int32, (c, c), 1)
    tril = jnp.where(col <= row, 1.0, 0.0)
    gi = lax.broadcasted_iota(jnp.int32, (4 * c, 4 * c), 0)
    gj = lax.broadcasted_iota(jnp.int32, (4 * c, 4 * c), 1)
    keep = (gj & (c - 1)) < (gi & (c - 1)) + gi // (2 * c)
    eye = jnp.where(lax.broadcasted_iota(jnp.int32, (2 * c, 2 * c), 0) == lax.broadcasted_iota(jnp.int32, (2 * c, 2 * c), 1),
                    1.0, 0.0).astype(F32)

    cum = [_dot_exact_lhs(tril, lw[s]) for s in chunks]
    between_stages()
    last = [x[c - 1:c, :] for x in cum]
    a_t = [_stack(a_vec[s] * jnp.exp(cu - lw[s])) for s, cu in zip(chunks, cum)]
    r_t = [_stack(r[s] * jnp.exp(cu)) for s, cu in zip(chunks, cum)]
    v_s = [_stack(v[s]) for s in chunks]
    xs = [jnp.concatenate([a, rr], axis=0).astype(BF16) for a, rr in zip(a_t, r_t)]
    ys = [jnp.concatenate([_stack(b_vec[s] * jnp.exp(-cu)), _stack(k[s] * jnp.exp(-cu))], axis=0).astype(BF16)
          for s, cu in zip(chunks, cum)]
    g = [jnp.where(keep, lax.dot_general(x, y, (((1,), (1,)), ((), ())), preferred_element_type=F32), 0.0)
         for x, y in zip(xs, ys)]
    between_stages()
    l_ab = [x[:2 * c, :2 * c] for x in g]
    l_ak = [x[:2 * c, 2 * c:] for x in g]
    l_rb = [x[2 * c:, :2 * c] for x in g]
    l_rk = [x[2 * c:, 2 * c:] for x in g]
    t_inv = [eye + x for x in l_ab]
    l_pow = l_ab
    for _ in range(int(math.log2(c)) - 1):
        l_pow = [_bdot(x, x) for x in l_pow]
        between_stages()
        t_inv = [t + _bdot(p, t) for p, t in zip(l_pow, t_inv)]
        between_stages()
    akv = [_bdot(x, y) for x, y in zip(l_ak, v_s)]
    ta = [_bdot(t, jnp.concatenate([a, u], axis=1)) for t, a, u in zip(t_inv, a_t, akv)]
    between_stages()
    rbx = [_bdot(x, y) for x, y in zip(l_rb, ta)]
    rkv = [_bdot(x, y) for x, y in zip(l_rk, v_s)]
    between_stages()
    r_new = [x + y[:, :PAIR] for x, y in zip(r_t, rbx)]
    y0 = [x[:, PAIR:] + y for x, y in zip(rbx, rkv)]
    e_rem = [jnp.exp(la - cu) for la, cu in zip(last, cum)]
    mn = [_bdot(_stack(b_vec[s] * e).T, t) for s, e, t in zip(chunks, e_rem, ta)]
    kv = [_bdot(_stack(k[s] * e).T, vv) for s, e, vv in zip(chunks, e_rem, v_s)]
    m_c = [x[:, :PAIR] + eye * jnp.exp(la) for x, la in zip(mn, last)]
    n_c = [x[:, PAIR:] + y for x, y in zip(mn, kv)]
    return list(zip(r_new, y0, m_c, n_c))


def _token_shift(cur, prev_row, mu):
    rolled = pltpu.roll(cur, 1, 0)
    first = lax.broadcasted_iota(jnp.int32, cur.shape, 0) == 0
    prev = jnp.where(first, prev_row, rolled)
    return cur + (prev - cur) * mu


def _rwkv_transforms(r, k, v, e, w0, a0, k_k, k_a, ww2, wa2, ones):
    w = -_softplus(-(w0 + jnp.dot(jnp.tanh(e).astype(BF16), ww2, preferred_element_type=F32))) - 0.5
    lw = -jnp.exp(w)
    a = _sigmoid(a0 + jnp.dot(e.astype(BF16), wa2, preferred_element_type=F32))
    kk = k * k_k
    kk = kk / jnp.maximum(jnp.sqrt(_head_sum(kk * kk, ones)), 1e-12)
    k_mod = k * (1.0 + (a - 1.0) * k_a)
    return lw, a, kk, k_mod


def _group_norm(y, ln_w, ln_b, ones):
    mean = _head_sum(y, ones) * (1.0 / RWKV_HEAD_DIM)
    d = y - mean
    var = _head_sum(d * d, ones) * (1.0 / RWKV_HEAD_DIM)
    return d * lax.rsqrt(var + GN_EPS) * ln_w + ln_b


def _wkv_prompt_kernel(r_ref, k_ref, v_ref, e_ref, mur_ref, muk_ref, muv_ref, mue_ref, w0_ref, a0_ref,
                       kk_ref, ka_ref, rk_ref, ww2_ref, wa2_ref, lnw_ref, lnb_ref,
                       o_ref, st_ref, pr_ref, pk_ref, pv_ref, pe_ref, s_ref, y_ref,
                       rn_ref, y0_ref, mc_ref, nc_ref, bonus_ref, *, tb, nt):
    n = pl.program_id(0)
    nc = tb // CHUNK
    ones = _pair_ones()

    @pl.when(n == 0)
    def _():
        for ref in (rn_ref, y0_ref, mc_ref, nc_ref, bonus_ref, s_ref):
            ref[...] = jnp.zeros_like(ref)

    @pl.when(lax.rem(n, nt) == 0)
    def _():
        for ref in (pr_ref, pk_ref, pv_ref, pe_ref):
            ref[...] = jnp.zeros_like(ref)

    prev_starts_sequence = lax.rem(n - 1, nt) == 0
    state = [jnp.where(prev_starts_sequence, 0.0, s_ref[...])]
    pending = list(range(nc))

    def recurrence_step():
        if not pending:
            return
        i = pending.pop(0)
        s = state[0]
        ys = _bdot(rn_ref[i], s) + y0_ref[i]
        y_ref[i * CHUNK:(i + 1) * CHUNK, :] = ys[:CHUNK] + ys[CHUNK:]
        state[0] = _bdot(mc_ref[i], s) + nc_ref[i]

    r_cur, k_cur, v_cur, e_cur = r_ref[0, 0], k_ref[0, 0], v_ref[0, 0], e_ref[...]
    r = _token_shift(r_cur, pr_ref[...], mur_ref[...])
    k = _token_shift(k_cur, pk_ref[...], muk_ref[...])
    v = _token_shift(v_cur, pv_ref[...], muv_ref[...])
    e = _token_shift(e_cur, pe_ref[...], mue_ref[...])
    for cur, prev in ((r_cur, pr_ref), (k_cur, pk_ref), (v_cur, pv_ref), (e_cur, pe_ref)):
        prev[...] = cur[tb - 1:tb, :]
    recurrence_step()
    lw, a, kk, k_mod = _rwkv_transforms(r, k, v, e, w0_ref[...], a0_ref[...], kk_ref[...], ka_ref[...],
                                        ww2_ref[...], wa2_ref[...], ones)
    recurrence_step()
    bonus_new = _head_sum(r * k_mod * rk_ref[...], ones) * v
    terms = _wkv_chunk_terms(r, k_mod, v, -kk, kk * a, lw, between_stages=recurrence_step)
    while pending:
        recurrence_step()
    s_ref[...] = state[0]
    st_ref[0, 0] = state[0]
    o_ref[...] = _group_norm(y_ref[...], lnw_ref[...], lnb_ref[...], ones) + bonus_ref[...]

    bonus_ref[...] = bonus_new
    for i, (r_new, y0, m_c, n_c) in enumerate(terms):
        rn_ref[i] = r_new
        y0_ref[i] = y0
        mc_ref[i] = m_c
        nc_ref[i] = n_c


def wkv_prompt(r, k, v, ext, mu_rkv, mu_e, w0, a0, k_k, k_a, r_k, ln_w, ln_b, ww2, wa2, *, tb):
    batch, _, seq, _ = r.shape
    nt = seq // tb
    n_blocks = batch * N_PAIRS * nt

    def where(n):
        return n // (N_PAIRS * nt), (n // nt) % N_PAIRS, n % nt

    cur = lambda n: where(jnp.minimum(n, n_blocks - 1))
    prev = lambda n: where(jnp.maximum(n - 1, 0))
    rows = pl.BlockSpec((1, 1, tb, LANES), lambda n: (cur(n)[0], cur(n)[1], cur(n)[2], 0))
    vec = lambda off: pl.BlockSpec((1, LANES), lambda n: (0, off + cur(n)[1]))
    vec_prev = pl.BlockSpec((1, LANES), lambda n: (0, prev(n)[1]))
    lora = pl.BlockSpec((RWKV_EXTRA, LANES), lambda n: (0, cur(n)[1]))
    terms = pltpu.VMEM((tb // CHUNK, PAIR, PAIR), F32)
    return pl.pallas_call(
        functools.partial(_wkv_prompt_kernel, tb=tb, nt=nt),
        grid=(n_blocks + 1,),
        in_specs=[
            rows, rows, rows,
            pl.BlockSpec((tb, RWKV_EXTRA), lambda n: (cur(n)[0] * nt + cur(n)[2], 0)),
            vec(0), vec(N_PAIRS), vec(2 * N_PAIRS),
            pl.BlockSpec((1, RWKV_EXTRA), lambda n: (0, 0)),
            vec(0), vec(0), vec(0), vec(0), vec(0), lora, lora, vec_prev, vec_prev,
        ],
        out_specs=[
            pl.BlockSpec((tb, LANES), lambda n: (prev(n)[0] * nt + prev(n)[2], prev(n)[1])),
            pl.BlockSpec((1, 1, PAIR, PAIR), lambda n: (prev(n)[0], prev(n)[1], 0, 0)),
        ],
        out_shape=[
            jax.ShapeDtypeStruct((batch * seq, TOK_WIDTH), F32),
            jax.ShapeDtypeStruct((batch, N_PAIRS, PAIR, PAIR), F32),
        ],
        scratch_shapes=[
            pltpu.VMEM((1, LANES), F32), pltpu.VMEM((1, LANES), F32), pltpu.VMEM((1, LANES), F32),
            pltpu.VMEM((1, RWKV_EXTRA), F32), pltpu.VMEM((PAIR, PAIR), F32), pltpu.VMEM((tb, LANES), F32),
            terms, terms, terms, terms, pltpu.VMEM((tb, LANES), F32),
        ],
        compiler_params=_params("arbitrary"),
        name="wkv_prompt",
    )(r, k, v, ext, mu_rkv, mu_rkv, mu_rkv, mu_e, w0, a0, k_k, k_a, r_k, ww2, wa2, ln_w, ln_b)


def _wkv_step_kernel(cur_ref, prev_ref, e_ref, pe_ref, mu_ref, mue_ref, w0_ref, a0_ref, kk_ref, ka_ref, rk_ref,
                     lnw_ref, lnb_ref, ww2_ref, wa2_ref, st_ref, o_ref, stn_ref):
    hd = RWKV_HEAD_DIM
    e_cur = e_ref[0]
    e = e_cur + (pe_ref[0] - e_cur) * mue_ref[...]
    e_rows = jnp.broadcast_to(e, (SUBLANES, RWKV_EXTRA))
    e_tanh = jnp.tanh(e_rows).astype(BF16)
    e_lin = e_rows.astype(BF16)
    eye = lax.broadcasted_iota(jnp.int32, (hd, hd), 0) == lax.broadcasted_iota(jnp.int32, (hd, hd), 1)

    def mixed(i, h):
        cur = cur_ref[0, i, h:h + 1, :]
        return cur + (prev_ref[0, i, h:h + 1, :] - cur) * mu_ref[i, h:h + 1, :]

    for h in range(RWKV_HEADS):
        row = lambda ref: ref[h:h + 1, :]
        r, k, v = mixed(0, h), mixed(1, h), mixed(2, h)
        w = -_softplus(-(row(w0_ref) + jnp.dot(e_tanh, ww2_ref[h], preferred_element_type=F32)[0:1])) - 0.5
        decay = jnp.exp(-jnp.exp(w))
        a = _sigmoid(row(a0_ref) + jnp.dot(e_lin, wa2_ref[h], preferred_element_type=F32)[0:1])
        kk = k * row(kk_ref)
        kk = kk / jnp.maximum(jnp.sqrt(jnp.sum(kk * kk, axis=-1, keepdims=True)), 1e-12)
        k_mod = k * (1.0 + (a - 1.0) * row(ka_ref))
        s = st_ref[0, h]
        sa = jnp.sum(s * (-kk), axis=-1, keepdims=True)
        v_col = jnp.sum(jnp.where(eye, jnp.broadcast_to(v, (hd, hd)), 0.0), axis=-1, keepdims=True)
        s_new = s * decay + sa * (kk * a) + v_col * k_mod
        stn_ref[0, h] = s_new
        y_col = jnp.sum(s_new * r, axis=-1, keepdims=True)
        y = jnp.sum(jnp.where(eye, jnp.broadcast_to(y_col, (hd, hd)), 0.0), axis=0, keepdims=True)
        mean = jnp.mean(y, axis=-1, keepdims=True)
        d = y - mean
        var = jnp.mean(d * d, axis=-1, keepdims=True)
        yn = d * lax.rsqrt(var + GN_EPS) * row(lnw_ref) + row(lnb_ref)
        o_ref[0, h:h + 1, :] = yn + jnp.sum(r * k_mod * row(rk_ref), axis=-1, keepdims=True) * v


def wkv_step(cur, prev, e_cur, e_prev, mu, mu_e, w0, a0, k_k, k_a, r_k, ln_w, ln_b, ww2h, wa2h, state):
    batch = cur.shape[0]
    hshape = (RWKV_HEADS, RWKV_HEAD_DIM)
    per_row = lambda shape: pl.BlockSpec((1,) + shape, lambda b: (b,) + (0,) * len(shape))
    const = lambda shape: pl.BlockSpec(shape, lambda b: (0,) * len(shape))
    return pl.pallas_call(
        _wkv_step_kernel,
        grid=(batch,),
        in_specs=[per_row((3,) + hshape), per_row((3,) + hshape), per_row((1, RWKV_EXTRA)), per_row((1, RWKV_EXTRA)),
                  const((3,) + hshape), const((1, RWKV_EXTRA))] + [const(hshape)] * 7
                 + [const((RWKV_HEADS, RWKV_EXTRA, RWKV_HEAD_DIM))] * 2 + [per_row(hshape + (RWKV_HEAD_DIM,))],
        out_specs=[per_row(hshape), per_row(hshape + (RWKV_HEAD_DIM,))],
        out_shape=[jax.ShapeDtypeStruct((batch,) + hshape, F32),
                   jax.ShapeDtypeStruct((batch,) + hshape + (RWKV_HEAD_DIM,), F32)],
        compiler_params=_params("parallel"),
        name="wkv_step",
    )(cur, prev, e_cur, e_prev, mu, mu_e, w0, a0, k_k, k_a, r_k, ln_w, ln_b, ww2h, wa2h, state)


def _relayout_in_proj(w_in, tok_cols, extra_cols, extra_width):
    w_t = jnp.transpose(w_in).astype(BF16)
    extra = w_t[tok_cols:tok_cols + extra_cols]
    q_mem = w_t[tok_cols + extra_cols:tok_cols + extra_cols + MEM_WIDTH]
    gate = w_t[tok_cols + extra_cols + MEM_WIDTH:]
    return w_t, jnp.concatenate([gate, q_mem], axis=0), jnp.pad(extra, ((0, extra_width - extra_cols), (0, 0)))


def _pad_lanes(v, width):
    return jnp.pad(v, (0, width - v.shape[0])).reshape(1, width)


def _split_w_o(w_o):
    return w_o[:TOK_WIDTH].astype(BF16), w_o[TOK_WIDTH:].astype(BF16)


def _memory_kv(mem, g_mem, w_mem):
    kv = norm_proj(mem, g_mem, jnp.transpose(w_mem).astype(BF16), tm=mem.shape[0])
    return kv[:, :MEM_WIDTH], kv[:, MEM_WIDTH:]


def kernel(x_prompt, x_sample, cache_fox_k, cache_fox_v, cache_fox_logf, cache_mem_k, cache_mem_v, state_rwkv_wkv, state_rwkv_shift, page_table, mem_prompt, g_pre, g_post, g_mem, w_mem, w_o, fox_w_in, fox_b_f, rwkv_w_in, rwkv_mu, rwkv_w0, rwkv_w_w2, rwkv_a0, rwkv_w_a2, rwkv_k_k, rwkv_k_a, rwkv_r_k, rwkv_ln_w, rwkv_ln_b):
    bp, seq, _ = x_prompt.shape
    bs = x_sample.shape[0]
    xp = x_prompt.reshape(bp * seq, D_MODEL)
    xs = x_sample.reshape(bs, D_MODEL)
    mem = mem_prompt.reshape(bp * N_MEM, D_MODEL)
    tm_p = 1024
    mem_shape = (bp, N_MEM, MEM_HEADS, MEM_HEAD_DIM)

    w_tok, w_gm, w_ext = _relayout_in_proj(fox_w_in[0], 3 * TOK_WIDTH, FOX_HEADS, FOX_EXTRA)
    wo_tok, wo_mem = _split_w_o(w_o[0])
    b_pad = _pad_lanes(fox_b_f[0], LANES)
    mk0, mv0 = _memory_kv(mem, g_mem[0], w_mem[0])

    gm_p, q_p, k_p, v_p, ext_p = norm_proj(xp, g_pre[0], w_tok, w_gm, w_ext, tm=tm_p, heads=(bp, seq))
    logf_p, ccol, crow = fox_gate(ext_p, b_pad, batch=bp, seq=seq, tb=256)
    o_tok = fox_flash(q_p, k_p, v_p, ccol, crow, tq=512)
    o_mem = mem_attn(gm_p.reshape(bp, seq, GM_WIDTH), mk0, mv0, tq=1024, q_block=D_MODEL // MEM_WIDTH)
    xp = out_proj(o_tok, o_mem.reshape(bp * seq, MEM_WIDTH), gm_p, xp, wo_tok, wo_mem, g_post[0], tm=256)

    main_s, ext_s = norm_proj(xs, g_pre[0], w_tok, w_gm, w_ext, tm=bs)
    q_s, k_s, v_s = (main_s[:, off:off + TOK_WIDTH] for off in (TOKA_OFF, TOKB_OFF, TOKC_OFF))
    logf_pool_t = jnp.pad(jnp.transpose(cache_fox_logf[0], (0, 2, 1)), ((0, 0), (0, HEAD_PAD - FOX_HEADS), (0, 0)))
    k_pool, v_pool = (jnp.transpose(c, (0, 1, 3, 2, 4)) for c in (cache_fox_k, cache_fox_v))
    o_tok_s, logf_s = fox_decode(page_table, q_s, k_s, v_s, ext_s, b_pad, k_pool, v_pool, logf_pool_t, layer=0)
    o_mem_s = mem_attn(main_s.reshape(bs, 1, MAIN_WIDTH), cache_mem_k[0].reshape(bs * N_MEM, MEM_WIDTH),
                       cache_mem_v[0].reshape(bs * N_MEM, MEM_WIDTH), tq=1, q_block=QMEM_OFF // MEM_WIDTH)
    xs = out_proj(o_tok_s, o_mem_s.reshape(bs, MEM_WIDTH), main_s, xs, wo_tok, wo_mem, g_post[0], tm=bs)

    head_shape = (FOX_HEADS, FOX_HEAD_DIM)
    fox_k_p = jnp.transpose(k_p, (0, 2, 1, 3))[None]
    fox_v_p = jnp.transpose(v_p, (0, 2, 1, 3))[None]
    fox_f_p = logf_p[:, :FOX_HEADS].reshape(1, bp, seq, FOX_HEADS)
    fox_k_s = k_s.reshape((1, bs, 1) + head_shape)
    fox_v_s = v_s.reshape((1, bs, 1) + head_shape)
    fox_f_s = logf_s[:, :FOX_HEADS].reshape(1, bs, 1, FOX_HEADS)

    w_tok, w_gm, w_ext = _relayout_in_proj(rwkv_w_in[0], 3 * TOK_WIDTH, 2 * LORA, RWKV_EXTRA)
    wo_tok, wo_mem = _split_w_o(w_o[1])
    mk1, mv1 = _memory_kv(mem, g_mem[1], w_mem[1])
    mu_rkv = rwkv_mu[0][:3 * TOK_WIDTH].reshape(1, 3 * TOK_WIDTH)
    mu_e = _pad_lanes(rwkv_mu[0][3 * TOK_WIDTH:], RWKV_EXTRA)
    ww2 = jnp.pad(rwkv_w_w2[0], ((0, RWKV_EXTRA - LORA), (0, 0))).astype(BF16)
    wa2 = jnp.pad(rwkv_w_a2[0], ((LORA, RWKV_EXTRA - 2 * LORA), (0, 0))).astype(BF16)
    vecs = [p[0] for p in (rwkv_w0, rwkv_a0, rwkv_k_k, rwkv_k_a, rwkv_r_k, rwkv_ln_w, rwkv_ln_b)]

    gm_p, r_p, k_p, v_p, ext_p = norm_proj(xp, g_pre[1], w_tok, w_gm, w_ext, tm=tm_p, heads=(bp, seq))
    o_tok, st_pairs = wkv_prompt(r_p, k_p, v_p, ext_p, mu_rkv, mu_e, *[v.reshape(1, TOK_WIDTH) for v in vecs], ww2, wa2,
                                 tb=512)
    o_mem = mem_attn(gm_p.reshape(bp, seq, GM_WIDTH), mk1, mv1, tq=1024, q_block=D_MODEL // MEM_WIDTH)
    xp = out_proj(o_tok, o_mem.reshape(bp * seq, MEM_WIDTH), gm_p, xp, wo_tok, wo_mem, g_post[1], tm=256)

    main_s, ext_s = norm_proj(xs, g_pre[1], w_tok, w_gm, w_ext, tm=bs)
    hshape = (RWKV_HEADS, RWKV_HEAD_DIM)
    shift_prev = state_rwkv_shift[0][:, 0, :]
    cur = main_s[:, TOKA_OFF:QMEM_OFF].reshape((bs, 3) + hshape)
    prev = shift_prev[:, :3 * TOK_WIDTH].reshape((bs, 3) + hshape)
    e_prev = jnp.pad(shift_prev[:, 3 * TOK_WIDTH:], ((0, 0), (0, RWKV_EXTRA - 2 * LORA))).reshape(bs, 1, RWKV_EXTRA)
    per_head = lambda w: jnp.transpose(w.reshape((RWKV_EXTRA,) + hshape), (1, 0, 2))
    o_tok_s, wkv_s = wkv_step(cur, prev, ext_s.reshape(bs, 1, RWKV_EXTRA), e_prev, mu_rkv.reshape((3,) + hshape), mu_e,
                              *[v.reshape(hshape) for v in vecs], per_head(ww2), per_head(wa2), state_rwkv_wkv[0])
    o_mem_s = mem_attn(main_s.reshape(bs, 1, MAIN_WIDTH), cache_mem_k[1].reshape(bs * N_MEM, MEM_WIDTH),
                       cache_mem_v[1].reshape(bs * N_MEM, MEM_WIDTH), tq=1, q_block=QMEM_OFF // MEM_WIDTH)
    xs = out_proj(o_tok_s.reshape(bs, TOK_WIDTH), o_mem_s.reshape(bs, MEM_WIDTH), main_s, xs, wo_tok, wo_mem, g_post[1], tm=bs)

    st6 = st_pairs.reshape(bp, N_PAIRS, 2, RWKV_HEAD_DIM, 2, RWKV_HEAD_DIM)
    st_heads = jnp.stack([st6[:, :, 0, :, 0, :], st6[:, :, 1, :, 1, :]], axis=2)
    wkv_p = jnp.swapaxes(st_heads.reshape((bp,) + hshape + (RWKV_HEAD_DIM,)), -1, -2)
    last_p = jnp.concatenate([a[:, :, seq - 1, :].reshape(bp, TOK_WIDTH) for a in (r_p, k_p, v_p)], axis=-1)
    last_e = ext_p.reshape(bp, seq, RWKV_EXTRA)[:, seq - 1, :2 * LORA]
    shift_p = jnp.concatenate([last_p, last_e], axis=-1).reshape(1, bp, 1, 3 * TOK_WIDTH + 2 * LORA)
    shift_s = jnp.concatenate([main_s[:, TOKA_OFF:QMEM_OFF], ext_s[:, :2 * LORA]], axis=-1).reshape(1, bs, 1, 3 * TOK_WIDTH + 2 * LORA)

    return (xp.reshape(bp, seq, D_MODEL), xs.reshape(bs, 1, D_MODEL),
            jnp.stack([mk0.reshape(mem_shape), mk1.reshape(mem_shape)]),
            jnp.stack([mv0.reshape(mem_shape), mv1.reshape(mem_shape)]),
            fox_k_p, fox_v_p, fox_f_p, fox_k_s, fox_v_s, fox_f_s,
            wkv_p[None], shift_p, wkv_s[None], shift_s)
```

```python
import functools
import math

import jax
import jax.numpy as jnp
from jax import lax
from jax.experimental import pallas as pl
from jax.experimental.pallas import tpu as pltpu

F32 = jnp.float32
BF16 = jnp.bfloat16

D_MODEL = 2048
N_MEM = 256
MEM_HEADS = 4
MEM_HEAD_DIM = 128
MEM_WIDTH = MEM_HEADS * MEM_HEAD_DIM
TOK_WIDTH = D_MODEL - MEM_WIDTH
FOX_HEAD_DIM = 128
FOX_HEADS = TOK_WIDTH // FOX_HEAD_DIM
RWKV_HEAD_DIM = 64
RWKV_HEADS = TOK_WIDTH // RWKV_HEAD_DIM
LORA = 96
PAGE_SIZE = 128
RMS_EPS = 1e-6
GN_EPS = 64e-5

LANES = 128
SUBLANES = 8
V7X_VMEM_BYTES = 64 * 1024 * 1024
VMEM_LIMIT_BYTES = V7X_VMEM_BYTES - 8 * 1024 * 1024

GATE_OFF = 0
TOKA_OFF = D_MODEL
TOKB_OFF = TOKA_OFF + TOK_WIDTH
TOKC_OFF = TOKB_OFF + TOK_WIDTH
QMEM_OFF = TOKC_OFF + TOK_WIDTH
MAIN_WIDTH = QMEM_OFF + MEM_WIDTH
FOX_EXTRA = LANES
RWKV_EXTRA = 2 * LANES

HEAD_PAD = 16
CHUNK = 64


def _params(*sem):
    return pltpu.CompilerParams(dimension_semantics=sem, vmem_limit_bytes=VMEM_LIMIT_BYTES)


def _rms_scale(x):
    return lax.rsqrt(jnp.mean(x * x, axis=-1, keepdims=True) + RMS_EPS)


def _sigmoid(x):
    return 1.0 / (1.0 + jnp.exp(-x))


def _softplus(x):
    return jnp.maximum(x, 0.0) + jnp.log(1.0 + jnp.exp(-jnp.abs(x)))


def _split(x, pieces):
    out = []
    for _ in range(pieces - 1):
        p = x.astype(BF16)
        out.append(p)
        x = x - p.astype(F32)
    return out + [x.astype(BF16)]


def _dot_exact_rhs(x, rhs, pieces=3):
    rows = x.shape[0]
    parts = jnp.dot(jnp.concatenate(_split(x, pieces), axis=0), rhs.astype(BF16), preferred_element_type=F32)
    return sum(parts[i * rows:(i + 1) * rows] for i in range(pieces))


def _dot_exact_lhs(lhs, x, pieces=3):
    cols = x.shape[1]
    parts = jnp.dot(lhs.astype(BF16), jnp.concatenate(_split(x, pieces), axis=1), preferred_element_type=F32)
    return sum(parts[:, i * cols:(i + 1) * cols] for i in range(pieces))


TN = 4 * LANES
GATE_BLOCKS = D_MODEL // TN
TOK_BLOCKS = TOK_WIDTH // TN
QMEM_BLOCK = GATE_BLOCKS + 3 * TOK_BLOCKS
GM_WIDTH = D_MODEL + MEM_WIDTH
N_GROUPS128 = TOK_WIDTH // LANES
assert MAIN_WIDTH == (QMEM_BLOCK + 1) * TN and MEM_WIDTH == TN


def _nt_dot(a, b):
    return lax.dot_general(a, b, (((1,), (1,)), ((), ())), preferred_element_type=F32)


def _norm_proj_kernel(x_ref, g_ref, wt_ref, *rest, has_gm, has_extra, head_mode):
    rest = list(rest)
    wg_ref = rest.pop(0) if has_gm else None
    we_ref = rest.pop(0) if has_extra else None
    h_ref = rest.pop()
    oe_ref = rest.pop() if has_extra else None
    j = pl.program_id(1)

    @pl.when(j == 0)
    def _():
        x = x_ref[...]
        h = (x * _rms_scale(x) * g_ref[...]).astype(BF16)
        h_ref[...] = h
        if has_extra:
            oe_ref[...] = _nt_dot(h, we_ref[...])

    if not has_gm:
        rest[0][...] = _nt_dot(h_ref[...], wt_ref[...])
        return
    in_gm = (j < GATE_BLOCKS) | (j == QMEM_BLOCK)

    @pl.when(in_gm)
    def _():
        rest[0][...] = _nt_dot(h_ref[...], wg_ref[...])

    if not head_mode:
        @pl.when(jnp.logical_not(in_gm))
        def _():
            rest[0][...] = _nt_dot(h_ref[...], wt_ref[...])
        return

    for gi, ref in enumerate(rest[1:]):
        first = GATE_BLOCKS + gi * TOK_BLOCKS

        @pl.when((j >= first) & (j < first + TOK_BLOCKS))
        def _(ref=ref):
            res = _nt_dot(h_ref[...], wt_ref[...])
            for hh in range(TN // LANES):
                ref[0, hh] = res[:, hh * LANES:(hh + 1) * LANES]


def norm_proj(x, g, w_tok_t, w_gm_t=None, w_extra_t=None, *, tm, heads=None):
    m, d = x.shape
    has_gm = w_gm_t is not None
    has_extra = w_extra_t is not None
    n = MAIN_WIDTH if has_gm else w_tok_t.shape[0]
    assert m % tm == 0 and n % TN == 0 and (has_gm or heads is None)
    gm_col = lambda j: jnp.where(j == QMEM_BLOCK, GATE_BLOCKS, jnp.minimum(j, GATE_BLOCKS - 1))
    tok_row = (lambda j: jnp.clip(j - GATE_BLOCKS, 0, 3 * TOK_BLOCKS - 1)) if has_gm else (lambda j: j)
    in_specs = [
        pl.BlockSpec((tm, d), lambda i, j: (i, 0)),
        pl.BlockSpec((1, d), lambda i, j: (0, 0)),
        pl.BlockSpec((TN, d), lambda i, j: (tok_row(j), 0)),
    ]
    args = [x, g.reshape(1, d), w_tok_t]
    if has_gm:
        in_specs.append(pl.BlockSpec((TN, d), lambda i, j: (gm_col(j), 0)))
        args.append(w_gm_t)
    if has_extra:
        in_specs.append(pl.BlockSpec(w_extra_t.shape, lambda i, j: (0, 0)))
        args.append(w_extra_t)
    if heads is None:
        out_specs = [pl.BlockSpec((tm, TN), lambda i, j: (i, j))]
        out_shape = [jax.ShapeDtypeStruct((m, n), F32)]
    else:
        batch, seq = heads
        assert seq % tm == 0
        per_seq = seq // tm
        out_specs = [pl.BlockSpec((tm, TN), lambda i, j: (i, gm_col(j)))]
        out_shape = [jax.ShapeDtypeStruct((m, GM_WIDTH), F32)]
        for gi in range(3):
            first = GATE_BLOCKS + gi * TOK_BLOCKS
            out_specs.append(pl.BlockSpec(
                (1, TN // LANES, tm, LANES),
                lambda i, j, first=first: (i // per_seq, jnp.clip(j - first, 0, TOK_BLOCKS - 1), i % per_seq, 0)))
            out_shape.append(jax.ShapeDtypeStruct((batch, N_GROUPS128, seq, LANES), F32))
    if has_extra:
        e = w_extra_t.shape[0]
        out_specs.append(pl.BlockSpec((tm, e), lambda i, j: (i, 0)))
        out_shape.append(jax.ShapeDtypeStruct((m, e), F32))
    outs = pl.pallas_call(
        functools.partial(_norm_proj_kernel, has_gm=has_gm, has_extra=has_extra, head_mode=heads is not None),
        grid=(m // tm, n // TN),
        in_specs=in_specs,
        out_specs=out_specs,
        out_shape=out_shape,
        scratch_shapes=[pltpu.VMEM((tm, d), BF16)],
        compiler_params=_params("parallel", "arbitrary"),
        name="norm_proj",
    )(*args)
    return outs if len(outs) > 1 else outs[0]


def _fox_gate_kernel(f_ref, b_ref, logf_ref, ccol_ref, crow_ref, carry_ref, *, tb):
    @pl.when(pl.program_id(1) == 0)
    def _():
        carry_ref[...] = jnp.zeros_like(carry_ref)

    logf = -_softplus(-(f_ref[...] + b_ref[...]))
    logf_ref[...] = logf
    row = lax.broadcasted_iota(jnp.int32, (tb, tb), 0)
    col = lax.broadcasted_iota(jnp.int32, (tb, tb), 1)
    c = _dot_exact_lhs(jnp.where(col <= row, 1.0, 0.0), logf) + carry_ref[...]
    ccol_ref[...] = c
    crow_ref[0] = c.T[:HEAD_PAD]
    carry_ref[...] = c[tb - 1:tb, :]


def fox_gate(f_ext, b_pad, *, batch, seq, tb):
    nt = seq // tb
    row_spec = pl.BlockSpec((tb, LANES), lambda b, t: (b * nt + t, 0))
    return pl.pallas_call(
        functools.partial(_fox_gate_kernel, tb=tb),
        grid=(batch, nt),
        in_specs=[row_spec, pl.BlockSpec((1, LANES), lambda b, t: (0, 0))],
        out_specs=[row_spec, row_spec, pl.BlockSpec((1, HEAD_PAD, tb), lambda b, t: (b, 0, t))],
        out_shape=[
            jax.ShapeDtypeStruct((batch * seq, LANES), F32),
            jax.ShapeDtypeStruct((batch * seq, LANES), F32),
            jax.ShapeDtypeStruct((batch, HEAD_PAD, seq), F32),
        ],
        scratch_shapes=[pltpu.VMEM((1, LANES), F32)],
        compiler_params=_params("parallel", "arbitrary"),
        name="fox_gate",
    )(f_ext, b_pad)


_MASKED = -1e30
LOG2E = 1.4426950408889634


def _flash_tile(h, qi, q_ref, k_ref, v_ref, ccol_ref, crow_ref, o_ref, kb_ref, vb_ref, *, tq):
    @pl.when(qi == 0)
    def _():
        kb_ref[...] = k_ref[0, 0].astype(BF16)
        vb_ref[...] = v_ref[0, 0].astype(BF16)

    q = (q_ref[0, 0] * (FOX_HEAD_DIM ** -0.5 * LOG2E)).astype(BF16)
    lane = lax.broadcasted_iota(jnp.int32, (tq, LANES), 1)
    cq = jnp.sum(jnp.where(lane == h, ccol_ref[...], 0.0), axis=-1, keepdims=True) * LOG2E

    def step(c, carry, diagonal):
        m, l, acc = carry
        start = pl.multiple_of(c * tq, tq)
        kc = kb_ref[pl.ds(start, tq), :]
        vc = vb_ref[pl.ds(start, tq), :]
        ck = crow_ref[0, 0, :, pl.ds(start, tq)] * LOG2E
        s = lax.dot_general(q, kc, (((1,), (1,)), ((), ())), preferred_element_type=F32) - ck
        if diagonal:
            row = lax.broadcasted_iota(jnp.int32, (tq, tq), 0)
            col = lax.broadcasted_iota(jnp.int32, (tq, tq), 1)
            s = jnp.where(col <= row, s, _MASKED)
        m_new = jnp.maximum(m, jnp.max(s, axis=-1, keepdims=True) + cq)
        p = jnp.exp2(s - (m_new - cq))
        alpha = jnp.exp2(m - m_new)
        l = alpha * l + jnp.sum(p, axis=-1, keepdims=True)
        acc = alpha * acc + jnp.dot(p.astype(BF16), vc, preferred_element_type=F32)
        return m_new, l, acc

    init = (jnp.full((tq, 1), _MASKED, F32), jnp.zeros((tq, 1), F32), jnp.zeros((tq, FOX_HEAD_DIM), F32))
    carry = lax.fori_loop(0, qi, lambda c, cr: step(c, cr, False), init)
    _, l, acc = step(qi, carry, True)
    o_ref[...] = acc / l


PAGES_PER_STEP = 8


def _head_mask():
    row = lax.broadcasted_iota(jnp.int32, (HEAD_PAD, TOK_WIDTH), 0)
    col = lax.broadcasted_iota(jnp.int32, (HEAD_PAD, TOK_WIDTH), 1)
    lo = row * FOX_HEAD_DIM
    return (col >= lo) & (col < lo + FOX_HEAD_DIM)


def _decode_pages(b, j, last, q_ref, kn_ref, vn_ref, f_ref, bf_ref, lf_refs, k_refs, v_refs,
                  o_ref, lfn_ref, qbd_ref, m_ref, l_ref, acc_ref, carry_ref):
    pps = PAGES_PER_STEP
    scale = FOX_HEAD_DIM ** -0.5
    hmask = _head_mask()

    @pl.when(j == 0)
    def _():
        q = q_ref[pl.ds(b, 1), :]
        q_bd = jnp.where(hmask, jnp.broadcast_to(q, (HEAD_PAD, TOK_WIDTH)), 0.0)
        qbd_ref[...] = q_bd.astype(BF16)
        logf_new = -_softplus(-(f_ref[pl.ds(b, 1), :] + bf_ref[...]))
        lfn_ref[pl.ds(b, 1), :] = logf_new
        eye = lax.broadcasted_iota(jnp.int32, (HEAD_PAD, LANES), 0) == lax.broadcasted_iota(jnp.int32, (HEAD_PAD, LANES), 1)
        carry_ref[...] = jnp.sum(jnp.where(eye, jnp.broadcast_to(logf_new, (HEAD_PAD, LANES)), 0.0), axis=-1, keepdims=True)
        m_ref[...] = jnp.sum(q_bd * kn_ref[pl.ds(b, 1), :], axis=-1, keepdims=True) * scale
        l_ref[...] = jnp.ones_like(l_ref)
        acc_ref[...] = jnp.broadcast_to(vn_ref[pl.ds(b, 1), :], (HEAD_PAD, TOK_WIDTH))

    def page_rows(ref):
        return jnp.concatenate([ref[0, 0, h].astype(BF16) for h in range(FOX_HEADS)], axis=1)

    lfs = [ref[0] for ref in lf_refs]
    src = lax.broadcasted_iota(jnp.int32, (PAGE_SIZE, PAGE_SIZE), 0)
    dst = lax.broadcasted_iota(jnp.int32, (PAGE_SIZE, PAGE_SIZE), 1)
    inside = _dot_exact_rhs(jnp.concatenate(lfs, axis=0), jnp.where(src > dst, 1.0, 0.0))
    carries = [carry_ref[...]]
    for lf in lfs:
        carries.append(carries[-1] + jnp.sum(lf, axis=-1, keepdims=True))
    carry_ref[...] = carries[-1]

    q_bd = qbd_ref[...]
    scores = []
    for i in range(pps):
        s = lax.dot_general(q_bd, page_rows(k_refs[i]), (((1,), (1,)), ((), ())), preferred_element_type=F32)
        scores.append(s * scale + (inside[i * HEAD_PAD:(i + 1) * HEAD_PAD] + carries[i]))
    m_old = m_ref[...]
    m_new = m_old
    for s in scores:
        m_new = jnp.maximum(m_new, jnp.max(s, axis=-1, keepdims=True))
    alpha = jnp.exp(m_old - m_new)
    l_new = alpha * l_ref[...]
    acc = alpha * acc_ref[...]
    for i, s in enumerate(scores):
        p = jnp.exp(s - m_new)
        l_new = l_new + jnp.sum(p, axis=-1, keepdims=True)
        acc = acc + jnp.dot(p.astype(BF16), page_rows(v_refs[i]), preferred_element_type=F32)
    m_ref[...] = m_new
    l_ref[...] = l_new
    acc_ref[...] = acc

    @pl.when(last)
    def _():
        out = jnp.where(hmask, acc / l_new, 0.0)
        o_ref[pl.ds(b, 1), :] = jnp.sum(out, axis=0, keepdims=True)


def _fox_attention_kernel(pt_ref, q_ref, k_ref, v_ref, ccol_ref, crow_ref, qs_ref, kn_ref, vn_ref, f_ref, bf_ref, *rest,
                          tq, steps_per_row, n_decode_steps):
    del pt_ref
    pps = PAGES_PER_STEP
    lf_refs, k_refs, v_refs = rest[:pps], rest[pps:2 * pps], rest[2 * pps:3 * pps]
    o_ref, os_ref, lfn_ref, kb_ref, vb_ref, qbd_ref, m_ref, l_ref, acc_ref, carry_ref = rest[3 * pps:]
    h = pl.program_id(1)
    qi = pl.program_id(2)
    n = (pl.program_id(0) * pl.num_programs(1) + h) * pl.num_programs(2) + qi
    _flash_tile(h, qi, q_ref, k_ref, v_ref, ccol_ref, crow_ref, o_ref, kb_ref, vb_ref, tq=tq)

    @pl.when(n < n_decode_steps)
    def _():
        j = lax.rem(n, steps_per_row)
        _decode_pages(lax.div(n, steps_per_row), j, j == steps_per_row - 1, qs_ref, kn_ref, vn_ref, f_ref, bf_ref,
                      lf_refs, k_refs, v_refs, os_ref, lfn_ref, qbd_ref, m_ref, l_ref, acc_ref, carry_ref)


def fox_attention(q, k, v, ccol, crow, page_table, q_s, k_new, v_new, f_ext, b_pad, k_pool, v_pool, logf_pool_t,
                  *, tq, layer):
    batch, heads, seq, _ = q.shape
    rows, n_pages = page_table.shape
    pps = PAGES_PER_STEP
    nq = seq // tq
    steps_per_row = n_pages // pps
    n_decode_steps = rows * steps_per_row
    assert n_pages % pps == 0 and n_decode_steps <= batch * heads * nq
    crow4 = crow.reshape(batch, HEAD_PAD, 1, seq)

    def page_id(i):
        def index(b, h, qi, pt):
            d = jnp.minimum((b * heads + h) * nq + qi, n_decode_steps - 1)
            return pt[d // steps_per_row, n_pages - 1 - (d % steps_per_row) * pps - i]
        return index

    whole = lambda shape: pl.BlockSpec(shape, lambda b, h, qi, pt: (0,) * len(shape))
    kv_spec = pl.BlockSpec((1, 1, seq, FOX_HEAD_DIM), lambda b, h, qi, pt: (b, h, 0, 0))
    lf_spec = lambda i: pl.BlockSpec((1, HEAD_PAD, PAGE_SIZE), lambda b, h, qi, pt: (page_id(i)(b, h, qi, pt), 0, 0))
    page_spec = lambda i: pl.BlockSpec((1, 1, FOX_HEADS, PAGE_SIZE, FOX_HEAD_DIM),
                                       lambda b, h, qi, pt: (layer, page_id(i)(b, h, qi, pt), 0, 0, 0))
    grid_spec = pltpu.PrefetchScalarGridSpec(
        num_scalar_prefetch=1,
        grid=(batch, heads, nq),
        in_specs=[
            pl.BlockSpec((1, 1, tq, FOX_HEAD_DIM), lambda b, h, qi, pt: (b, h, qi, 0)),
            kv_spec, kv_spec,
            pl.BlockSpec((tq, LANES), lambda b, h, qi, pt: (b * nq + qi, 0)),
            pl.BlockSpec((1, 1, 1, seq), lambda b, h, qi, pt: (b, h, 0, 0)),
        ] + [whole((rows, TOK_WIDTH))] * 3 + [whole((rows, LANES)), whole((1, LANES))]
          + [lf_spec(i) for i in range(pps)] + [page_spec(i) for i in range(pps)] * 2,
        out_specs=[
            pl.BlockSpec((tq, LANES), lambda b, h, qi, pt: (b * nq + qi, h)),
            whole((rows, TOK_WIDTH)), whole((rows, LANES)),
        ],
        scratch_shapes=[
            pltpu.VMEM((seq, LANES), BF16), pltpu.VMEM((seq, LANES), BF16),
            pltpu.VMEM((HEAD_PAD, TOK_WIDTH), BF16),
            pltpu.VMEM((HEAD_PAD, 1), F32),
            pltpu.VMEM((HEAD_PAD, 1), F32),
            pltpu.VMEM((HEAD_PAD, TOK_WIDTH), F32),
            pltpu.VMEM((HEAD_PAD, 1), F32),
        ],
    )
    return pl.pallas_call(
        functools.partial(_fox_attention_kernel, tq=tq, steps_per_row=steps_per_row, n_decode_steps=n_decode_steps),
        grid_spec=grid_spec,
        out_shape=[jax.ShapeDtypeStruct((batch * seq, TOK_WIDTH), F32),
                   jax.ShapeDtypeStruct((rows, TOK_WIDTH), F32), jax.ShapeDtypeStruct((rows, LANES), F32)],
        compiler_params=_params("arbitrary", "arbitrary", "arbitrary"),
        name="fox_attention",
    )(page_table, q, k, v, ccol, crow4, q_s, k_new, v_new, f_ext, b_pad,
      *([logf_pool_t] * pps), *([k_pool] * pps), *([v_pool] * pps))


def _mem_attn_kernel(q_ref, k_ref, v_ref, o_ref, *, tq):
    rows = max(tq, SUBLANES)
    scale = MEM_HEAD_DIM ** -0.5
    for h in range(MEM_HEADS):
        sl = slice(h * MEM_HEAD_DIM, (h + 1) * MEM_HEAD_DIM)
        qh = jnp.broadcast_to(q_ref[0, :, sl], (rows, MEM_HEAD_DIM)).astype(BF16)
        kh = k_ref[:, sl].astype(BF16)
        vh = v_ref[:, sl].astype(BF16)
        s = lax.dot_general(qh, kh, (((1,), (1,)), ((), ())), preferred_element_type=F32) * scale
        e = jnp.exp(s - jnp.max(s, axis=-1, keepdims=True))
        oh = jnp.dot(e.astype(BF16), vh, preferred_element_type=F32) / jnp.sum(e, axis=-1, keepdims=True)
        o_ref[0, :, sl] = oh[:tq]


def mem_attn(main3, mem_k, mem_v, *, tq, q_block):
    batch, seq, _ = main3.shape
    kv_spec = pl.BlockSpec((N_MEM, MEM_WIDTH), lambda b, i: (b, 0))
    return pl.pallas_call(
        functools.partial(_mem_attn_kernel, tq=tq),
        grid=(batch, seq // tq),
        in_specs=[pl.BlockSpec((1, tq, MEM_WIDTH), lambda b, i: (b, i, q_block)), kv_spec, kv_spec],
        out_specs=pl.BlockSpec((1, tq, MEM_WIDTH), lambda b, i: (b, i, 0)),
        out_shape=jax.ShapeDtypeStruct((batch, seq, MEM_WIDTH), F32),
        compiler_params=_params("parallel", "parallel"),
        name="mem_attn",
    )(main3, mem_k, mem_v)


def _out_proj_kernel(ot_ref, om_ref, gate_ref, x_ref, wt_ref, wm_ref, gp_ref, y_ref):
    gate = gate_ref[...]
    sg = gate * _sigmoid(gate)
    a = (ot_ref[...] * sg[:, :TOK_WIDTH]).astype(BF16)
    b = (om_ref[...] * sg[:, TOK_WIDTH:]).astype(BF16)
    y = jnp.dot(a, wt_ref[...], preferred_element_type=F32) + jnp.dot(b, wm_ref[...], preferred_element_type=F32)
    y_ref[...] = x_ref[...] + y * _rms_scale(y) * gp_ref[...]


def out_proj(o_tok, o_mem, main, x, w_tok, w_mem, g_post, *, tm):
    m = x.shape[0]
    row = lambda width: pl.BlockSpec((tm, width), lambda i: (i, 0))
    const = lambda shape: pl.BlockSpec(shape, lambda i: (0, 0))
    return pl.pallas_call(
        _out_proj_kernel,
        grid=(m // tm,),
        in_specs=[row(TOK_WIDTH), row(MEM_WIDTH), row(D_MODEL), row(D_MODEL),
                  const((TOK_WIDTH, D_MODEL)), const((MEM_WIDTH, D_MODEL)), const((1, D_MODEL))],
        out_specs=row(D_MODEL),
        out_shape=jax.ShapeDtypeStruct((m, D_MODEL), F32),
        compiler_params=_params("parallel"),
        name="out_proj",
    )(o_tok, o_mem, main, x, w_tok, w_mem, g_post.reshape(1, D_MODEL))


PAIR = 2 * RWKV_HEAD_DIM
assert PAIR == LANES
N_PAIRS = RWKV_HEADS // 2


def _pair_ones():
    row = lax.broadcasted_iota(jnp.int32, (PAIR, PAIR), 0) // RWKV_HEAD_DIM
    col = lax.broadcasted_iota(jnp.int32, (PAIR, PAIR), 1) // RWKV_HEAD_DIM
    return jnp.where(row == col, 1.0, 0.0).astype(F32)


def _head_sum(x, ones):
    return _dot_exact_rhs(x, ones, pieces=2)


def _stack(x):
    lane = lax.broadcasted_iota(jnp.int32, x.shape, 1)
    first = lane < RWKV_HEAD_DIM
    return jnp.concatenate([jnp.where(first, x, 0.0), jnp.where(first, 0.0, x)], axis=0)


def _bdot(a, b):
    return jnp.dot(a.astype(BF16), b.astype(BF16), preferred_element_type=F32)


def _wkv_chunk_terms(r, k, v, a_vec, b_vec, lw, between_stages=lambda: None):
    c = CHUNK
    chunks = [slice(c0, c0 + c) for c0 in range(0, r.shape[0], c)]
    row = lax.broadcasted_iota(jnp.int32, (c, c), 0)
    col = lax.broadcasted_iota(jnp.int32, (c, c), 1)
    tril = jnp.where(col <= row, 1.0, 0.0)
    gi = lax.broadcasted_iota(jnp.int32, (4 * c, 4 * c), 0)
    gj = lax.broadcasted_iota(jnp.int32, (4 * c, 4 * c), 1)
    keep = (gj & (c - 1)) < (gi & (c - 1)) + gi // (2 * c)
    eye = jnp.where(lax.broadcasted_iota(jnp.int32, (2 * c, 2 * c), 0) == lax.broadcasted_iota(jnp.int32, (2 * c, 2 * c), 1),
                    1.0, 0.0).astype(F32)

    cum = [_dot_exact_lhs(tril, lw[s]) for s in chunks]
    between_stages()
    last = [x[c - 1:c, :] for x in cum]
    a_t = [_stack(a_vec[s] * jnp.exp(cu - lw[s])) for s, cu in zip(chunks, cum)]
    r_t = [_stack(r[s] * jnp.exp(cu)) for s, cu in zip(chunks, cum)]
    v_s = [_stack(v[s]) for s in chunks]
    xs = [jnp.concatenate([a, rr], axis=0).astype(BF16) for a, rr in zip(a_t, r_t)]
    ys = [jnp.concatenate([_stack(b_vec[s] * jnp.exp(-cu)), _stack(k[s] * jnp.exp(-cu))], axis=0).astype(BF16)
          for s, cu in zip(chunks, cum)]
    g = [jnp.where(keep, lax.dot_general(x, y, (((1,), (1,)), ((), ())), preferred_element_type=F32), 0.0)
         for x, y in zip(xs, ys)]
    between_stages()
    l_ab = [x[:2 * c, :2 * c] for x in g]
    l_ak = [x[:2 * c, 2 * c:] for x in g]
    l_rb = [x[2 * c:, :2 * c] for x in g]
    l_rk = [x[2 * c:, 2 * c:] for x in g]
    t_inv = [eye + x for x in l_ab]
    l_pow = l_ab
    for _ in range(int(math.log2(c)) - 1):
        l_pow = [_bdot(x, x) for x in l_pow]
        between_stages()
        t_inv = [t + _bdot(p, t) for p, t in zip(l_pow, t_inv)]
        between_stages()
    akv = [_bdot(x, y) for x, y in zip(l_ak, v_s)]
    ta = [_bdot(t, jnp.concatenate([a, u], axis=1)) for t, a, u in zip(t_inv, a_t, akv)]
    between_stages()
    rbx = [_bdot(x, y) for x, y in zip(l_rb, ta)]
    rkv = [_bdot(x, y) for x, y in zip(l_rk, v_s)]
    between_stages()
    r_new = [x + y[:, :PAIR] for x, y in zip(r_t, rbx)]
    y0 = [x[:, PAIR:] + y for x, y in zip(rbx, rkv)]
    e_rem = [jnp.exp(la - cu) for la, cu in zip(last, cum)]
    mn = [_bdot(_stack(b_vec[s] * e).T, t) for s, e, t in zip(chunks, e_rem, ta)]
    kv = [_bdot(_stack(k[s] * e).T, vv) for s, e, vv in zip(chunks, e_rem, v_s)]
    m_c = [x[:, :PAIR] + eye * jnp.exp(la) for x, la in zip(mn, last)]
    n_c = [x[:, PAIR:] + y for x, y in zip(mn, kv)]
    return list(zip(r_new, y0, m_c, n_c))


def _token_shift(cur, prev_row, mu):
    rolled = pltpu.roll(cur, 1, 0)
    first = lax.broadcasted_iota(jnp.int32, cur.shape, 0) == 0
    prev = jnp.where(first, prev_row, rolled)
    return cur + (prev - cur) * mu


def _rwkv_transforms(r, k, v, e, w0, a0, k_k, k_a, ww2, wa2, ones):
    w = -_softplus(-(w0 + jnp.dot(jnp.tanh(e).astype(BF16), ww2, preferred_element_type=F32))) - 0.5
    lw = -jnp.exp(w)
    a = _sigmoid(a0 + jnp.dot(e.astype(BF16), wa2, preferred_element_type=F32))
    kk = k * k_k
    kk = kk / jnp.maximum(jnp.sqrt(_head_sum(kk * kk, ones)), 1e-12)
    k_mod = k * (1.0 + (a - 1.0) * k_a)
    return lw, a, kk, k_mod


def _group_norm(y, ln_w, ln_b, ones):
    mean = _head_sum(y, ones) * (1.0 / RWKV_HEAD_DIM)
    d = y - mean
    var = _head_sum(d * d, ones) * (1.0 / RWKV_HEAD_DIM)
    return d * lax.rsqrt(var + GN_EPS) * ln_w + ln_b


def _wkv_prompt_kernel(r_ref, k_ref, v_ref, e_ref, mur_ref, muk_ref, muv_ref, mue_ref, w0_ref, a0_ref,
                       kk_ref, ka_ref, rk_ref, ww2_ref, wa2_ref, lnw_ref, lnb_ref,
                       o_ref, st_ref, pr_ref, pk_ref, pv_ref, pe_ref, s_ref, y_ref,
                       rn_ref, y0_ref, mc_ref, nc_ref, bonus_ref, *, tb, nt):
    n = pl.program_id(0)
    nc = tb // CHUNK
    ones = _pair_ones()

    @pl.when(n == 0)
    def _():
        for ref in (rn_ref, y0_ref, mc_ref, nc_ref, bonus_ref, s_ref):
            ref[...] = jnp.zeros_like(ref)

    @pl.when(lax.rem(n, nt) == 0)
    def _():
        for ref in (pr_ref, pk_ref, pv_ref, pe_ref):
            ref[...] = jnp.zeros_like(ref)

    prev_starts_sequence = lax.rem(n - 1, nt) == 0
    state = [jnp.where(prev_starts_sequence, 0.0, s_ref[...])]
    pending = list(range(nc))

    def recurrence_step():
        if not pending:
            return
        i = pending.pop(0)
        s = state[0]
        ys = _bdot(rn_ref[i], s) + y0_ref[i]
        y_ref[i * CHUNK:(i + 1) * CHUNK, :] = ys[:CHUNK] + ys[CHUNK:]
        state[0] = _bdot(mc_ref[i], s) + nc_ref[i]

    r_cur, k_cur, v_cur, e_cur = r_ref[0, 0], k_ref[0, 0], v_ref[0, 0], e_ref[...]
    r = _token_shift(r_cur, pr_ref[...], mur_ref[...])
    k = _token_shift(k_cur, pk_ref[...], muk_ref[...])
    v = _token_shift(v_cur, pv_ref[...], muv_ref[...])
    e = _token_shift(e_cur, pe_ref[...], mue_ref[...])
    for cur, prev in ((r_cur, pr_ref), (k_cur, pk_ref), (v_cur, pv_ref), (e_cur, pe_ref)):
        prev[...] = cur[tb - 1:tb, :]
    recurrence_step()
    lw, a, kk, k_mod = _rwkv_transforms(r, k, v, e, w0_ref[...], a0_ref[...], kk_ref[...], ka_ref[...],
                                        ww2_ref[...], wa2_ref[...], ones)
    recurrence_step()
    bonus_new = _head_sum(r * k_mod * rk_ref[...], ones) * v
    terms = _wkv_chunk_terms(r, k_mod, v, -kk, kk * a, lw, between_stages=recurrence_step)
    while pending:
        recurrence_step()
    s_ref[...] = state[0]
    st_ref[0, 0] = state[0]
    o_ref[...] = _group_norm(y_ref[...], lnw_ref[...], lnb_ref[...], ones) + bonus_ref[...]

    bonus_ref[...] = bonus_new
    for i, (r_new, y0, m_c, n_c) in enumerate(terms):
        rn_ref[i] = r_new
        y0_ref[i] = y0
        mc_ref[i] = m_c
        nc_ref[i] = n_c


def wkv_prompt(r, k, v, ext, mu_rkv, mu_e, w0, a0, k_k, k_a, r_k, ln_w, ln_b, ww2, wa2, *, tb):
    batch, _, seq, _ = r.shape
    nt = seq // tb
    n_blocks = batch * N_PAIRS * nt

    def where(n):
        return n // (N_PAIRS * nt), (n // nt) % N_PAIRS, n % nt

    cur = lambda n: where(jnp.minimum(n, n_blocks - 1))
    prev = lambda n: where(jnp.maximum(n - 1, 0))
    rows = pl.BlockSpec((1, 1, tb, LANES), lambda n: (cur(n)[0], cur(n)[1], cur(n)[2], 0))
    vec = lambda off: pl.BlockSpec((1, LANES), lambda n: (0, off + cur(n)[1]))
    vec_prev = pl.BlockSpec((1, LANES), lambda n: (0, prev(n)[1]))
    lora = pl.BlockSpec((RWKV_EXTRA, LANES), lambda n: (0, cur(n)[1]))
    terms = pltpu.VMEM((tb // CHUNK, PAIR, PAIR), F32)
    return pl.pallas_call(
        functools.partial(_wkv_prompt_kernel, tb=tb, nt=nt),
        grid=(n_blocks + 1,),
        in_specs=[
            rows, rows, rows,
            pl.BlockSpec((tb, RWKV_EXTRA), lambda n: (cur(n)[0] * nt + cur(n)[2], 0)),
            vec(0), vec(N_PAIRS), vec(2 * N_PAIRS),
            pl.BlockSpec((1, RWKV_EXTRA), lambda n: (0, 0)),
            vec(0), vec(0), vec(0), vec(0), vec(0), lora, lora, vec_prev, vec_prev,
        ],
        out_specs=[
            pl.BlockSpec((tb, LANES), lambda n: (prev(n)[0] * nt + prev(n)[2], prev(n)[1])),
            pl.BlockSpec((1, 1, PAIR, PAIR), lambda n: (prev(n)[0], prev(n)[1], 0, 0)),
        ],
        out_shape=[
            jax.ShapeDtypeStruct((batch * seq, TOK_WIDTH), F32),
            jax.ShapeDtypeStruct((batch, N_PAIRS, PAIR, PAIR), F32),
        ],
        scratch_shapes=[
            pltpu.VMEM((1, LANES), F32), pltpu.VMEM((1, LANES), F32), pltpu.VMEM((1, LANES), F32),
            pltpu.VMEM((1, RWKV_EXTRA), F32), pltpu.VMEM((PAIR, PAIR), F32), pltpu.VMEM((tb, LANES), F32),
            terms, terms, terms, terms, pltpu.VMEM((tb, LANES), F32),
        ],
        compiler_params=_params("arbitrary"),
        name="wkv_prompt",
    )(r, k, v, ext, mu_rkv, mu_rkv, mu_rkv, mu_e, w0, a0, k_k, k_a, r_k, ww2, wa2, ln_w, ln_b)


def _wkv_step_kernel(cur_ref, prev_ref, e_ref, pe_ref, mu_ref, mue_ref, w0_ref, a0_ref, kk_ref, ka_ref, rk_ref,
                     lnw_ref, lnb_ref, ww2_ref, wa2_ref, st_ref, o_ref, stn_ref):
    hd = RWKV_HEAD_DIM
    e_cur = e_ref[0]
    e = e_cur + (pe_ref[0] - e_cur) * mue_ref[...]
    e_rows = jnp.broadcast_to(e, (SUBLANES, RWKV_EXTRA))
    e_tanh = jnp.tanh(e_rows).astype(BF16)
    e_lin = e_rows.astype(BF16)
    eye = lax.broadcasted_iota(jnp.int32, (hd, hd), 0) == lax.broadcasted_iota(jnp.int32, (hd, hd), 1)

    def mixed(i, h):
        cur = cur_ref[0, i, h:h + 1, :]
        return cur + (prev_ref[0, i, h:h + 1, :] - cur) * mu_ref[i, h:h + 1, :]

    for h in range(RWKV_HEADS):
        row = lambda ref: ref[h:h + 1, :]
        r, k, v = mixed(0, h), mixed(1, h), mixed(2, h)
        w = -_softplus(-(row(w0_ref) + jnp.dot(e_tanh, ww2_ref[h], preferred_element_type=F32)[0:1])) - 0.5
        decay = jnp.exp(-jnp.exp(w))
        a = _sigmoid(row(a0_ref) + jnp.dot(e_lin, wa2_ref[h], preferred_element_type=F32)[0:1])
        kk = k * row(kk_ref)
        kk = kk / jnp.maximum(jnp.sqrt(jnp.sum(kk * kk, axis=-1, keepdims=True)), 1e-12)
        k_mod = k * (1.0 + (a - 1.0) * row(ka_ref))
        s = st_ref[0, h]
        sa = jnp.sum(s * (-kk), axis=-1, keepdims=True)
        v_col = jnp.sum(jnp.where(eye, jnp.broadcast_to(v, (hd, hd)), 0.0), axis=-1, keepdims=True)
        s_new = s * decay + sa * (kk * a) + v_col * k_mod
        stn_ref[0, h] = s_new
        y_col = jnp.sum(s_new * r, axis=-1, keepdims=True)
        y = jnp.sum(jnp.where(eye, jnp.broadcast_to(y_col, (hd, hd)), 0.0), axis=0, keepdims=True)
        mean = jnp.mean(y, axis=-1, keepdims=True)
        d = y - mean
        var = jnp.mean(d * d, axis=-1, keepdims=True)
        yn = d * lax.rsqrt(var + GN_EPS) * row(lnw_ref) + row(lnb_ref)
        o_ref[0, h:h + 1, :] = yn + jnp.sum(r * k_mod * row(rk_ref), axis=-1, keepdims=True) * v


def wkv_step(cur, prev, e_cur, e_prev, mu, mu_e, w0, a0, k_k, k_a, r_k, ln_w, ln_b, ww2h, wa2h, state):
    batch = cur.shape[0]
    hshape = (RWKV_HEADS, RWKV_HEAD_DIM)
    per_row = lambda shape: pl.BlockSpec((1,) + shape, lambda b: (b,) + (0,) * len(shape))
    const = lambda shape: pl.BlockSpec(shape, lambda b: (0,) * len(shape))
    return pl.pallas_call(
        _wkv_step_kernel,
        grid=(batch,),
        in_specs=[per_row((3,) + hshape), per_row((3,) + hshape), per_row((1, RWKV_EXTRA)), per_row((1, RWKV_EXTRA)),
                  const((3,) + hshape), const((1, RWKV_EXTRA))] + [const(hshape)] * 7
                 + [const((RWKV_HEADS, RWKV_EXTRA, RWKV_HEAD_DIM))] * 2 + [per_row(hshape + (RWKV_HEAD_DIM,))],
        out_specs=[per_row(hshape), per_row(hshape + (RWKV_HEAD_DIM,))],
        out_shape=[jax.ShapeDtypeStruct((batch,) + hshape, F32),
                   jax.ShapeDtypeStruct((batch,) + hshape + (RWKV_HEAD_DIM,), F32)],
        compiler_params=_params("parallel"),
        name="wkv_step",
    )(cur, prev, e_cur, e_prev, mu, mu_e, w0, a0, k_k, k_a, r_k, ln_w, ln_b, ww2h, wa2h, state)


def _relayout_in_proj(w_in, tok_cols, extra_cols, extra_width):
    w_t = jnp.transpose(w_in).astype(BF16)
    extra = w_t[tok_cols:tok_cols + extra_cols]
    q_mem = w_t[tok_cols + extra_cols:tok_cols + extra_cols + MEM_WIDTH]
    gate = w_t[tok_cols + extra_cols + MEM_WIDTH:]
    return w_t, jnp.concatenate([gate, q_mem], axis=0), jnp.pad(extra, ((0, extra_width - extra_cols), (0, 0)))


def _pad_lanes(v, width):
    return jnp.pad(v, (0, width - v.shape[0])).reshape(1, width)


def _split_w_o(w_o):
    return w_o[:TOK_WIDTH].astype(BF16), w_o[TOK_WIDTH:].astype(BF16)


def _memory_kv(mem, g_mem, w_mem):
    kv = norm_proj(mem, g_mem, jnp.transpose(w_mem).astype(BF16), tm=mem.shape[0])
    return kv[:, :MEM_WIDTH], kv[:, MEM_WIDTH:]


def kernel(x_prompt, x_sample, cache_fox_k, cache_fox_v, cache_fox_logf, cache_mem_k, cache_mem_v, state_rwkv_wkv, state_rwkv_shift, page_table, mem_prompt, g_pre, g_post, g_mem, w_mem, w_o, fox_w_in, fox_b_f, rwkv_w_in, rwkv_mu, rwkv_w0, rwkv_w_w2, rwkv_a0, rwkv_w_a2, rwkv_k_k, rwkv_k_a, rwkv_r_k, rwkv_ln_w, rwkv_ln_b):
    bp, seq, _ = x_prompt.shape
    bs = x_sample.shape[0]
    xp = x_prompt.reshape(bp * seq, D_MODEL)
    xs = x_sample.reshape(bs, D_MODEL)
    mem = mem_prompt.reshape(bp * N_MEM, D_MODEL)
    tm_p = 1024
    mem_shape = (bp, N_MEM, MEM_HEADS, MEM_HEAD_DIM)

    w_tok, w_gm, w_ext = _relayout_in_proj(fox_w_in[0], 3 * TOK_WIDTH, FOX_HEADS, FOX_EXTRA)
    wo_tok, wo_mem = _split_w_o(w_o[0])
    b_pad = _pad_lanes(fox_b_f[0], LANES)
    mk0, mv0 = _memory_kv(mem, g_mem[0], w_mem[0])

    gm_p, q_p, k_p, v_p, ext_p = norm_proj(xp, g_pre[0], w_tok, w_gm, w_ext, tm=tm_p, heads=(bp, seq))
    logf_p, ccol, crow = fox_gate(ext_p, b_pad, batch=bp, seq=seq, tb=256)
    main_s, ext_s = norm_proj(xs, g_pre[0], w_tok, w_gm, w_ext, tm=bs)
    q_s, k_s, v_s = (main_s[:, off:off + TOK_WIDTH] for off in (TOKA_OFF, TOKB_OFF, TOKC_OFF))
    logf_pool_t = jnp.pad(jnp.transpose(cache_fox_logf[0], (0, 2, 1)), ((0, 0), (0, HEAD_PAD - FOX_HEADS), (0, 0)))
    k_pool, v_pool = (jnp.transpose(c, (0, 1, 3, 2, 4)) for c in (cache_fox_k, cache_fox_v))
    o_tok, o_tok_s, logf_s = fox_attention(q_p, k_p, v_p, ccol, crow, page_table, q_s, k_s, v_s, ext_s, b_pad,
                                           k_pool, v_pool, logf_pool_t, tq=512, layer=0)
    o_mem = mem_attn(gm_p.reshape(bp, seq, GM_WIDTH), mk0, mv0, tq=1024, q_block=D_MODEL // MEM_WIDTH)
    xp = out_proj(o_tok, o_mem.reshape(bp * seq, MEM_WIDTH), gm_p, xp, wo_tok, wo_mem, g_post[0], tm=256)
    o_mem_s = mem_attn(main_s.reshape(bs, 1, MAIN_WIDTH), cache_mem_k[0].reshape(bs * N_MEM, MEM_WIDTH),
                       cache_mem_v[0].reshape(bs * N_MEM, MEM_WIDTH), tq=1, q_block=QMEM_OFF // MEM_WIDTH)
    xs = out_proj(o_tok_s, o_mem_s.reshape(bs, MEM_WIDTH), main_s, xs, wo_tok, wo_mem, g_post[0], tm=bs)

    head_shape = (FOX_HEADS, FOX_HEAD_DIM)
    fox_k_p = jnp.transpose(k_p, (0, 2, 1, 3))[None]
    fox_v_p = jnp.transpose(v_p, (0, 2, 1, 3))[None]
    fox_f_p = logf_p[:, :FOX_HEADS].reshape(1, bp, seq, FOX_HEADS)
    fox_k_s = k_s.reshape((1, bs, 1) + head_shape)
    fox_v_s = v_s.reshape((1, bs, 1) + head_shape)
    fox_f_s = logf_s[:, :FOX_HEADS].reshape(1, bs, 1, FOX_HEADS)

    w_tok, w_gm, w_ext = _relayout_in_proj(rwkv_w_in[0], 3 * TOK_WIDTH, 2 * LORA, RWKV_EXTRA)
    wo_tok, wo_mem = _split_w_o(w_o[1])
    mk1, mv1 = _memory_kv(mem, g_mem[1], w_mem[1])
    mu_rkv = rwkv_mu[0][:3 * TOK_WIDTH].reshape(1, 3 * TOK_WIDTH)
    mu_e = _pad_lanes(rwkv_mu[0][3 * TOK_WIDTH:], RWKV_EXTRA)
    ww2 = jnp.pad(rwkv_w_w2[0], ((0, RWKV_EXTRA - LORA), (0, 0))).astype(BF16)
    wa2 = jnp.pad(rwkv_w_a2[0], ((LORA, RWKV_EXTRA - 2 * LORA), (0, 0))).astype(BF16)
    vecs = [p[0] for p in (rwkv_w0, rwkv_a0, rwkv_k_k, rwkv_k_a, rwkv_r_k, rwkv_ln_w, rwkv_ln_b)]

    gm_p, r_p, k_p, v_p, ext_p = norm_proj(xp, g_pre[1], w_tok, w_gm, w_ext, tm=tm_p, heads=(bp, seq))
    o_tok, st_pairs = wkv_prompt(r_p, k_p, v_p, ext_p, mu_rkv, mu_e, *[v.reshape(1, TOK_WIDTH) for v in vecs], ww2, wa2,
                                 tb=512)
    o_mem = mem_attn(gm_p.reshape(bp, seq, GM_WIDTH), mk1, mv1, tq=1024, q_block=D_MODEL // MEM_WIDTH)
    xp = out_proj(o_tok, o_mem.reshape(bp * seq, MEM_WIDTH), gm_p, xp, wo_tok, wo_mem, g_post[1], tm=256)

    main_s, ext_s = norm_proj(xs, g_pre[1], w_tok, w_gm, w_ext, tm=bs)
    hshape = (RWKV_HEADS, RWKV_HEAD_DIM)
    shift_prev = state_rwkv_shift[0][:, 0, :]
    cur = main_s[:, TOKA_OFF:QMEM_OFF].reshape((bs, 3) + hshape)
    prev = shift_prev[:, :3 * TOK_WIDTH].reshape((bs, 3) + hshape)
    e_prev = jnp.pad(shift_prev[:, 3 * TOK_WIDTH:], ((0, 0), (0, RWKV_EXTRA - 2 * LORA))).reshape(bs, 1, RWKV_EXTRA)
    per_head = lambda w: jnp.transpose(w.reshape((RWKV_EXTRA,) + hshape), (1, 0, 2))
    o_tok_s, wkv_s = wkv_step(cur, prev, ext_s.reshape(bs, 1, RWKV_EXTRA), e_prev, mu_rkv.reshape((3,) + hshape), mu_e,
                              *[v.reshape(hshape) for v in vecs], per_head(ww2), per_head(wa2), state_rwkv_wkv[0])
    o_mem_s = mem_attn(main_s.reshape(bs, 1, MAIN_WIDTH), cache_mem_k[1].reshape(bs * N_MEM, MEM_WIDTH),
                       cache_mem_v[1].reshape(bs * N_MEM, MEM_WIDTH), tq=1, q_block=QMEM_OFF // MEM_WIDTH)
    xs = out_proj(o_tok_s.reshape(bs, TOK_WIDTH), o_mem_s.reshape(bs, MEM_WIDTH), main_s, xs, wo_tok, wo_mem, g_post[1], tm=bs)

    st6 = st_pairs.reshape(bp, N_PAIRS, 2, RWKV_HEAD_DIM, 2, RWKV_HEAD_DIM)
    st_heads = jnp.stack([st6[:, :, 0, :, 0, :], st6[:, :, 1, :, 1, :]], axis=2)
    wkv_p = jnp.swapaxes(st_heads.reshape((bp,) + hshape + (RWKV_HEAD_DIM,)), -1, -2)
    last_p = jnp.concatenate([a[:, :, seq - 1, :].reshape(bp, TOK_WIDTH) for a in (r_p, k_p, v_p)], axis=-1)
    last_e = ext_p.reshape(bp, seq, RWKV_EXTRA)[:, seq - 1, :2 * LORA]
    shift_p = jnp.concatenate([last_p, last_e], axis=-1).reshape(1, bp, 1, 3 * TOK_WIDTH + 2 * LORA)
    shift_s = jnp.concatenate([main_s[:, TOKA_OFF:QMEM_OFF], ext_s[:, :2 * LORA]], axis=-1).reshape(1, bs, 1, 3 * TOK_WIDTH + 2 * LORA)

    return (xp.reshape(bp, seq, D_MODEL), xs.reshape(bs, 1, D_MODEL),
            jnp.stack([mk0.reshape(mem_shape), mk1.reshape(mem_shape)]),
            jnp.stack([mv0.reshape(mem_shape), mv1.reshape(mem_shape)]),
            fox_k_p, fox_v_p, fox_f_p, fox_k_s, fox_v_s, fox_f_s,
            wkv_p[None], shift_p, wkv_s[None], shift_s)
```

```python
import functools
import math

import jax
import jax.numpy as jnp
from jax import lax
from jax.experimental import pallas as pl
from jax.experimental.pallas import tpu as pltpu

F32 = jnp.float32
BF16 = jnp.bfloat16

D_MODEL = 2048
N_MEM = 256
MEM_HEADS = 4
MEM_HEAD_DIM = 128
MEM_WIDTH = MEM_HEADS * MEM_HEAD_DIM
TOK_WIDTH = D_MODEL - MEM_WIDTH
FOX_HEAD_DIM = 128
FOX_HEADS = TOK_WIDTH // FOX_HEAD_DIM
RWKV_HEAD_DIM = 64
RWKV_HEADS = TOK_WIDTH // RWKV_HEAD_DIM
LORA = 96
PAGE_SIZE = 128
RMS_EPS = 1e-6
GN_EPS = 64e-5

LANES = 128
SUBLANES = 8
V7X_VMEM_BYTES = 64 * 1024 * 1024
VMEM_LIMIT_BYTES = V7X_VMEM_BYTES - 8 * 1024 * 1024

GATE_OFF = 0
TOKA_OFF = D_MODEL
TOKB_OFF = TOKA_OFF + TOK_WIDTH
TOKC_OFF = TOKB_OFF + TOK_WIDTH
QMEM_OFF = TOKC_OFF + TOK_WIDTH
MAIN_WIDTH = QMEM_OFF + MEM_WIDTH
FOX_EXTRA = LANES
RWKV_EXTRA = 2 * LANES

HEAD_PAD = 16
CHUNK = 64


def _params(*sem):
    return pltpu.CompilerParams(dimension_semantics=sem, vmem_limit_bytes=VMEM_LIMIT_BYTES)


def _rms_scale(x):
    return lax.rsqrt(jnp.mean(x * x, axis=-1, keepdims=True) + RMS_EPS)


def _sigmoid(x):
    return 1.0 / (1.0 + jnp.exp(-x))


def _softplus(x):
    return jnp.maximum(x, 0.0) + jnp.log(1.0 + jnp.exp(-jnp.abs(x)))


def _split(x, pieces):
    out = []
    for _ in range(pieces - 1):
        p = x.astype(BF16)
        out.append(p)
        x = x - p.astype(F32)
    return out + [x.astype(BF16)]


def _dot_exact_rhs(x, rhs, pieces=3):
    rows = x.shape[0]
    parts = jnp.dot(jnp.concatenate(_split(x, pieces), axis=0), rhs.astype(BF16), preferred_element_type=F32)
    return sum(parts[i * rows:(i + 1) * rows] for i in range(pieces))


def _dot_exact_lhs(lhs, x, pieces=3):
    cols = x.shape[1]
    parts = jnp.dot(lhs.astype(BF16), jnp.concatenate(_split(x, pieces), axis=1), preferred_element_type=F32)
    return sum(parts[:, i * cols:(i + 1) * cols] for i in range(pieces))


TN = 4 * LANES
GATE_BLOCKS = D_MODEL // TN
TOK_BLOCKS = TOK_WIDTH // TN
QMEM_BLOCK = GATE_BLOCKS + 3 * TOK_BLOCKS
GM_WIDTH = D_MODEL + MEM_WIDTH
N_GROUPS128 = TOK_WIDTH // LANES
assert MAIN_WIDTH == (QMEM_BLOCK + 1) * TN and MEM_WIDTH == TN


def _nt_dot(a, b):
    return lax.dot_general(a, b, (((1,), (1,)), ((), ())), preferred_element_type=F32)


def _norm_proj_kernel(x_ref, g_ref, wt_ref, *rest, has_gm, has_extra, head_mode):
    rest = list(rest)
    wg_ref = rest.pop(0) if has_gm else None
    we_ref = rest.pop(0) if has_extra else None
    h_ref = rest.pop()
    oe_ref = rest.pop() if has_extra else None
    j = pl.program_id(1)

    @pl.when(j == 0)
    def _():
        x = x_ref[...]
        h = (x * _rms_scale(x) * g_ref[...]).astype(BF16)
        h_ref[...] = h
        if has_extra:
            oe_ref[...] = _nt_dot(h, we_ref[...])

    if not has_gm:
        rest[0][...] = _nt_dot(h_ref[...], wt_ref[...])
        return
    in_gm = (j < GATE_BLOCKS) | (j == QMEM_BLOCK)

    @pl.when(in_gm)
    def _():
        rest[0][...] = _nt_dot(h_ref[...], wg_ref[...])

    if not head_mode:
        @pl.when(jnp.logical_not(in_gm))
        def _():
            rest[0][...] = _nt_dot(h_ref[...], wt_ref[...])
        return

    for gi, ref in enumerate(rest[1:]):
        first = GATE_BLOCKS + gi * TOK_BLOCKS

        @pl.when((j >= first) & (j < first + TOK_BLOCKS))
        def _(ref=ref):
            res = _nt_dot(h_ref[...], wt_ref[...])
            for hh in range(TN // LANES):
                ref[0, hh] = res[:, hh * LANES:(hh + 1) * LANES]


def norm_proj(x, g, w_tok_t, w_gm_t=None, w_extra_t=None, *, tm, heads=None):
    m, d = x.shape
    has_gm = w_gm_t is not None
    has_extra = w_extra_t is not None
    n = MAIN_WIDTH if has_gm else w_tok_t.shape[0]
    assert m % tm == 0 and n % TN == 0 and (has_gm or heads is None)
    gm_col = lambda j: jnp.where(j == QMEM_BLOCK, GATE_BLOCKS, jnp.minimum(j, GATE_BLOCKS - 1))
    tok_row = (lambda j: jnp.clip(j - GATE_BLOCKS, 0, 3 * TOK_BLOCKS - 1)) if has_gm else (lambda j: j)
    in_specs = [
        pl.BlockSpec((tm, d), lambda i, j: (i, 0)),
        pl.BlockSpec((1, d), lambda i, j: (0, 0)),
        pl.BlockSpec((TN, d), lambda i, j: (tok_row(j), 0)),
    ]
    args = [x, g.reshape(1, d), w_tok_t]
    if has_gm:
        in_specs.append(pl.BlockSpec((TN, d), lambda i, j: (gm_col(j), 0)))
        args.append(w_gm_t)
    if has_extra:
        in_specs.append(pl.BlockSpec(w_extra_t.shape, lambda i, j: (0, 0)))
        args.append(w_extra_t)
    if heads is None:
        out_specs = [pl.BlockSpec((tm, TN), lambda i, j: (i, j))]
        out_shape = [jax.ShapeDtypeStruct((m, n), F32)]
    else:
        batch, seq = heads
        assert seq % tm == 0
        per_seq = seq // tm
        out_specs = [pl.BlockSpec((tm, TN), lambda i, j: (i, gm_col(j)))]
        out_shape = [jax.ShapeDtypeStruct((m, GM_WIDTH), F32)]
        for gi in range(3):
            first = GATE_BLOCKS + gi * TOK_BLOCKS
            out_specs.append(pl.BlockSpec(
                (1, TN // LANES, tm, LANES),
                lambda i, j, first=first: (i // per_seq, jnp.clip(j - first, 0, TOK_BLOCKS - 1), i % per_seq, 0)))
            out_shape.append(jax.ShapeDtypeStruct((batch, N_GROUPS128, seq, LANES), F32))
    if has_extra:
        e = w_extra_t.shape[0]
        out_specs.append(pl.BlockSpec((tm, e), lambda i, j: (i, 0)))
        out_shape.append(jax.ShapeDtypeStruct((m, e), F32))
    outs = pl.pallas_call(
        functools.partial(_norm_proj_kernel, has_gm=has_gm, has_extra=has_extra, head_mode=heads is not None),
        grid=(m // tm, n // TN),
        in_specs=in_specs,
        out_specs=out_specs,
        out_shape=out_shape,
        scratch_shapes=[pltpu.VMEM((tm, d), BF16)],
        compiler_params=_params("parallel", "arbitrary"),
        name="norm_proj",
    )(*args)
    return outs if len(outs) > 1 else outs[0]


def _fox_gate_kernel(f_ref, b_ref, logf_ref, ccol_ref, crow_ref, carry_ref, *, tb):
    @pl.when(pl.program_id(1) == 0)
    def _():
        carry_ref[...] = jnp.zeros_like(carry_ref)

    logf = -_softplus(-(f_ref[...] + b_ref[...]))
    logf_ref[...] = logf
    row = lax.broadcasted_iota(jnp.int32, (tb, tb), 0)
    col = lax.broadcasted_iota(jnp.int32, (tb, tb), 1)
    c = _dot_exact_lhs(jnp.where(col <= row, 1.0, 0.0), logf) + carry_ref[...]
    ccol_ref[...] = c
    crow_ref[0] = c.T[:HEAD_PAD]
    carry_ref[...] = c[tb - 1:tb, :]


def fox_gate(f_ext, b_pad, *, batch, seq, tb):
    nt = seq // tb
    row_spec = pl.BlockSpec((tb, LANES), lambda b, t: (b * nt + t, 0))
    return pl.pallas_call(
        functools.partial(_fox_gate_kernel, tb=tb),
        grid=(batch, nt),
        in_specs=[row_spec, pl.BlockSpec((1, LANES), lambda b, t: (0, 0))],
        out_specs=[row_spec, row_spec, pl.BlockSpec((1, HEAD_PAD, tb), lambda b, t: (b, 0, t))],
        out_shape=[
            jax.ShapeDtypeStruct((batch * seq, LANES), F32),
            jax.ShapeDtypeStruct((batch * seq, LANES), F32),
            jax.ShapeDtypeStruct((batch, HEAD_PAD, seq), F32),
        ],
        scratch_shapes=[pltpu.VMEM((1, LANES), F32)],
        compiler_params=_params("parallel", "arbitrary"),
        name="fox_gate",
    )(f_ext, b_pad)


_MASKED = -1e30
LOG2E = 1.4426950408889634


def _flash_tile(h, qi, q_ref, k_ref, v_ref, ccol_ref, crow_ref, o_ref, kb_ref, vb_ref, *, tq):
    @pl.when(qi == 0)
    def _():
        kb_ref[...] = k_ref[0, 0].astype(BF16)
        vb_ref[...] = v_ref[0, 0].astype(BF16)

    q = (q_ref[0, 0] * (FOX_HEAD_DIM ** -0.5 * LOG2E)).astype(BF16)
    lane = lax.broadcasted_iota(jnp.int32, (tq, LANES), 1)
    cq = jnp.sum(jnp.where(lane == h, ccol_ref[...], 0.0), axis=-1, keepdims=True) * LOG2E

    def step(c, carry, diagonal):
        m, l, acc = carry
        start = pl.multiple_of(c * tq, tq)
        kc = kb_ref[pl.ds(start, tq), :]
        vc = vb_ref[pl.ds(start, tq), :]
        ck = crow_ref[0, 0, :, pl.ds(start, tq)] * LOG2E
        s = lax.dot_general(q, kc, (((1,), (1,)), ((), ())), preferred_element_type=F32) - ck
        if diagonal:
            row = lax.broadcasted_iota(jnp.int32, (tq, tq), 0)
            col = lax.broadcasted_iota(jnp.int32, (tq, tq), 1)
            s = jnp.where(col <= row, s, _MASKED)
        m_new = jnp.maximum(m, jnp.max(s, axis=-1, keepdims=True) + cq)
        p = jnp.exp2(s - (m_new - cq))
        alpha = jnp.exp2(m - m_new)
        l = alpha * l + jnp.sum(p, axis=-1, keepdims=True)
        acc = alpha * acc + jnp.dot(p.astype(BF16), vc, preferred_element_type=F32)
        return m_new, l, acc

    init = (jnp.full((tq, 1), _MASKED, F32), jnp.zeros((tq, 1), F32), jnp.zeros((tq, FOX_HEAD_DIM), F32))
    carry = lax.fori_loop(0, qi, lambda c, cr: step(c, cr, False), init)
    _, l, acc = step(qi, carry, True)
    o_ref[...] = acc / l


PAGES_PER_STEP = 8


def _head_mask():
    row = lax.broadcasted_iota(jnp.int32, (HEAD_PAD, TOK_WIDTH), 0)
    col = lax.broadcasted_iota(jnp.int32, (HEAD_PAD, TOK_WIDTH), 1)
    lo = row * FOX_HEAD_DIM
    return (col >= lo) & (col < lo + FOX_HEAD_DIM)


def _decode_pages(b, j, last, pt_ref, q_ref, kn_ref, vn_ref, f_ref, bf_ref, lf_ref, k_refs, v_refs,
                  o_ref, lfn_ref, qbd_ref, m_ref, l_ref, acc_ref, carry_ref):
    pps = PAGES_PER_STEP
    n_pages = pt_ref.shape[1]
    scale = FOX_HEAD_DIM ** -0.5
    hmask = _head_mask()

    @pl.when(j == 0)
    def _():
        q = q_ref[pl.ds(b, 1), :]
        q_bd = jnp.where(hmask, jnp.broadcast_to(q, (HEAD_PAD, TOK_WIDTH)), 0.0)
        qbd_ref[...] = q_bd.astype(BF16)
        logf_new = -_softplus(-(f_ref[pl.ds(b, 1), :] + bf_ref[...]))
        lfn_ref[pl.ds(b, 1), :] = logf_new
        eye = lax.broadcasted_iota(jnp.int32, (HEAD_PAD, LANES), 0) == lax.broadcasted_iota(jnp.int32, (HEAD_PAD, LANES), 1)
        carry_ref[...] = jnp.sum(jnp.where(eye, jnp.broadcast_to(logf_new, (HEAD_PAD, LANES)), 0.0), axis=-1, keepdims=True)
        m_ref[...] = jnp.sum(q_bd * kn_ref[pl.ds(b, 1), :], axis=-1, keepdims=True) * scale
        l_ref[...] = jnp.ones_like(l_ref)
        acc_ref[...] = jnp.broadcast_to(vn_ref[pl.ds(b, 1), :], (HEAD_PAD, TOK_WIDTH))

    def page_rows(ref):
        return jnp.concatenate([ref[0, 0, h].astype(BF16) for h in range(FOX_HEADS)], axis=1)

    lfs = [lf_ref[pt_ref[b, n_pages - 1 - j * pps - i]] for i in range(pps)]
    src = lax.broadcasted_iota(jnp.int32, (PAGE_SIZE, PAGE_SIZE), 0)
    dst = lax.broadcasted_iota(jnp.int32, (PAGE_SIZE, PAGE_SIZE), 1)
    inside = _dot_exact_rhs(jnp.concatenate(lfs, axis=0), jnp.where(src > dst, 1.0, 0.0))
    carries = [carry_ref[...]]
    for lf in lfs:
        carries.append(carries[-1] + jnp.sum(lf, axis=-1, keepdims=True))
    carry_ref[...] = carries[-1]

    q_bd = qbd_ref[...]
    scores = []
    for i in range(pps):
        s = lax.dot_general(q_bd, page_rows(k_refs[i]), (((1,), (1,)), ((), ())), preferred_element_type=F32)
        scores.append(s * scale + (inside[i * HEAD_PAD:(i + 1) * HEAD_PAD] + carries[i]))
    m_old = m_ref[...]
    m_new = m_old
    for s in scores:
        m_new = jnp.maximum(m_new, jnp.max(s, axis=-1, keepdims=True))
    alpha = jnp.exp(m_old - m_new)
    l_new = alpha * l_ref[...]
    acc = alpha * acc_ref[...]
    for i, s in enumerate(scores):
        p = jnp.exp(s - m_new)
        l_new = l_new + jnp.sum(p, axis=-1, keepdims=True)
        acc = acc + jnp.dot(p.astype(BF16), page_rows(v_refs[i]), preferred_element_type=F32)
    m_ref[...] = m_new
    l_ref[...] = l_new
    acc_ref[...] = acc

    @pl.when(last)
    def _():
        out = jnp.where(hmask, acc / l_new, 0.0)
        o_ref[pl.ds(b, 1), :] = jnp.sum(out, axis=0, keepdims=True)


def _decode_plan(page_table, q_s, k_new, v_new, f_ext, b_pad, k_pool, v_pool, logf_pool_t, *, layer, step_of):
    rows, n_pages = page_table.shape
    pps = PAGES_PER_STEP
    assert n_pages % pps == 0
    steps_per_row = n_pages // pps
    n_steps = rows * steps_per_row

    def page_id(i):
        def index(*idx):
            pt = idx[-1]
            d = jnp.minimum(step_of(*idx[:-1]), n_steps - 1)
            return pt[lax.div(d, steps_per_row), n_pages - 1 - lax.rem(d, steps_per_row) * pps - i]
        return index

    whole = lambda shape: pl.BlockSpec(shape, lambda *idx: (0,) * len(shape))
    page_spec = lambda i: pl.BlockSpec((1, 1, FOX_HEADS, PAGE_SIZE, FOX_HEAD_DIM),
                                       lambda *idx: (layer, page_id(i)(*idx), 0, 0, 0))
    return dict(
        in_specs=[whole((rows, TOK_WIDTH))] * 3 + [whole((rows, LANES)), whole((1, LANES)), whole(logf_pool_t.shape)]
                 + [page_spec(i) for i in range(pps)] * 2,
        args=[q_s, k_new, v_new, f_ext, b_pad, logf_pool_t] + [k_pool] * pps + [v_pool] * pps,
        out_specs=[whole((rows, TOK_WIDTH)), whole((rows, LANES))],
        out_shape=[jax.ShapeDtypeStruct((rows, TOK_WIDTH), F32), jax.ShapeDtypeStruct((rows, LANES), F32)],
        scratch_shapes=[
            pltpu.VMEM((HEAD_PAD, TOK_WIDTH), BF16),
            pltpu.VMEM((HEAD_PAD, 1), F32),
            pltpu.VMEM((HEAD_PAD, 1), F32),
            pltpu.VMEM((HEAD_PAD, TOK_WIDTH), F32),
            pltpu.VMEM((HEAD_PAD, 1), F32),
        ],
        steps_per_row=steps_per_row, n_steps=n_steps,
    )


N_DECODE_INPUTS = 6 + 2 * PAGES_PER_STEP
N_DECODE_OUTPUTS = 2
N_DECODE_SCRATCH = 5


def _decode_step(n, pt_ref, inputs, outputs, scratch, *, steps_per_row, n_steps):
    pps = PAGES_PER_STEP
    q_ref, kn_ref, vn_ref, f_ref, bf_ref, lf_ref = inputs[:6]
    k_refs, v_refs = inputs[6:6 + pps], inputs[6 + pps:]

    @pl.when(n < n_steps)
    def _():
        j = lax.rem(n, steps_per_row)
        _decode_pages(lax.div(n, steps_per_row), j, j == steps_per_row - 1, pt_ref, q_ref, kn_ref, vn_ref, f_ref,
                      bf_ref, lf_ref, k_refs, v_refs, *outputs, *scratch)


def _fox_flash_kernel(q_ref, k_ref, v_ref, ccol_ref, crow_ref, o_ref, kb_ref, vb_ref, *, tq):
    _flash_tile(pl.program_id(1), pl.program_id(2), q_ref, k_ref, v_ref, ccol_ref, crow_ref, o_ref, kb_ref, vb_ref, tq=tq)


def fox_flash(q, k, v, ccol, crow, *, tq):
    batch, _, seq, _ = q.shape
    nq = seq // tq
    crow4 = crow.reshape(batch, HEAD_PAD, 1, seq)
    kv_spec = pl.BlockSpec((1, 1, seq, FOX_HEAD_DIM), lambda b, h, i: (b, h, 0, 0))
    return pl.pallas_call(
        functools.partial(_fox_flash_kernel, tq=tq),
        grid=(batch, FOX_HEADS, nq),
        in_specs=[
            pl.BlockSpec((1, 1, tq, FOX_HEAD_DIM), lambda b, h, i: (b, h, i, 0)),
            kv_spec, kv_spec,
            pl.BlockSpec((tq, LANES), lambda b, h, i: (b * nq + i, 0)),
            pl.BlockSpec((1, 1, 1, seq), lambda b, h, i: (b, h, 0, 0)),
        ],
        out_specs=pl.BlockSpec((tq, LANES), lambda b, h, i: (b * nq + i, h)),
        out_shape=jax.ShapeDtypeStruct((batch * seq, TOK_WIDTH), F32),
        scratch_shapes=[pltpu.VMEM((seq, LANES), BF16), pltpu.VMEM((seq, LANES), BF16)],
        compiler_params=_params("parallel", "parallel", "arbitrary"),
        name="fox_flash",
    )(q, k, v, ccol, crow4)


def _mem_attn_kernel(q_ref, k_ref, v_ref, o_ref, *, tq):
    rows = max(tq, SUBLANES)
    scale = MEM_HEAD_DIM ** -0.5
    for h in range(MEM_HEADS):
        sl = slice(h * MEM_HEAD_DIM, (h + 1) * MEM_HEAD_DIM)
        qh = jnp.broadcast_to(q_ref[0, :, sl], (rows, MEM_HEAD_DIM)).astype(BF16)
        kh = k_ref[:, sl].astype(BF16)
        vh = v_ref[:, sl].astype(BF16)
        s = lax.dot_general(qh, kh, (((1,), (1,)), ((), ())), preferred_element_type=F32) * scale
        e = jnp.exp(s - jnp.max(s, axis=-1, keepdims=True))
        oh = jnp.dot(e.astype(BF16), vh, preferred_element_type=F32) / jnp.sum(e, axis=-1, keepdims=True)
        o_ref[0, :, sl] = oh[:tq]


def mem_attn(main3, mem_k, mem_v, *, tq, q_block):
    batch, seq, _ = main3.shape
    kv_spec = pl.BlockSpec((N_MEM, MEM_WIDTH), lambda b, i: (b, 0))
    return pl.pallas_call(
        functools.partial(_mem_attn_kernel, tq=tq),
        grid=(batch, seq // tq),
        in_specs=[pl.BlockSpec((1, tq, MEM_WIDTH), lambda b, i: (b, i, q_block)), kv_spec, kv_spec],
        out_specs=pl.BlockSpec((1, tq, MEM_WIDTH), lambda b, i: (b, i, 0)),
        out_shape=jax.ShapeDtypeStruct((batch, seq, MEM_WIDTH), F32),
        compiler_params=_params("parallel", "parallel"),
        name="mem_attn",
    )(main3, mem_k, mem_v)


def _out_proj_kernel(ot_ref, om_ref, gate_ref, x_ref, wt_ref, wm_ref, gp_ref, y_ref):
    gate = gate_ref[...]
    sg = gate * _sigmoid(gate)
    a = (ot_ref[...] * sg[:, :TOK_WIDTH]).astype(BF16)
    b = (om_ref[...] * sg[:, TOK_WIDTH:]).astype(BF16)
    y = jnp.dot(a, wt_ref[...], preferred_element_type=F32) + jnp.dot(b, wm_ref[...], preferred_element_type=F32)
    y_ref[...] = x_ref[...] + y * _rms_scale(y) * gp_ref[...]


def out_proj(o_tok, o_mem, main, x, w_tok, w_mem, g_post, *, tm):
    m = x.shape[0]
    row = lambda width: pl.BlockSpec((tm, width), lambda i: (i, 0))
    const = lambda shape: pl.BlockSpec(shape, lambda i: (0, 0))
    return pl.pallas_call(
        _out_proj_kernel,
        grid=(m // tm,),
        in_specs=[row(TOK_WIDTH), row(MEM_WIDTH), row(D_MODEL), row(D_MODEL),
                  const((TOK_WIDTH, D_MODEL)), const((MEM_WIDTH, D_MODEL)), const((1, D_MODEL))],
        out_specs=row(D_MODEL),
        out_shape=jax.ShapeDtypeStruct((m, D_MODEL), F32),
        compiler_params=_params("parallel"),
        name="out_proj",
    )(o_tok, o_mem, main, x, w_tok, w_mem, g_post.reshape(1, D_MODEL))


PAIR = 2 * RWKV_HEAD_DIM
assert PAIR == LANES
N_PAIRS = RWKV_HEADS // 2


def _pair_ones():
    row = lax.broadcasted_iota(jnp.int32, (PAIR, PAIR), 0) // RWKV_HEAD_DIM
    col = lax.broadcasted_iota(jnp.int32, (PAIR, PAIR), 1) // RWKV_HEAD_DIM
    return jnp.where(row == col, 1.0, 0.0).astype(F32)


def _head_sum(x, ones):
    return _dot_exact_rhs(x, ones, pieces=2)


def _stack(x):
    lane = lax.broadcasted_iota(jnp.int32, x.shape, 1)
    first = lane < RWKV_HEAD_DIM
    return jnp.concatenate([jnp.where(first, x, 0.0), jnp.where(first, 0.0, x)], axis=0)


def _bdot(a, b):
    return jnp.dot(a.astype(BF16), b.astype(BF16), preferred_element_type=F32)


def _wkv_chunk_terms(r, k, v, a_vec, b_vec, lw, between_stages=lambda: None):
    c = CHUNK
    chunks = [slice(c0, c0 + c) for c0 in range(0, r.shape[0], c)]
    row = lax.broadcasted_iota(jnp.int32, (c, c), 0)
    col = lax.broadcasted_iota(jnp.int32, (c, c), 1)
    tril = jnp.where(col <= row, 1.0, 0.0)
    gi = lax.broadcasted_iota(jnp.int32, (4 * c, 4 * c), 0)
    gj = lax.broadcasted_iota(jnp.int32, (4 * c, 4 * c), 1)
    keep = (gj & (c - 1)) < (gi & (c - 1)) + gi // (2 * c)
    eye = jnp.where(lax.broadcasted_iota(jnp.int32, (2 * c, 2 * c), 0) == lax.broadcasted_iota(jnp.int32, (2 * c, 2 * c), 1),
                    1.0, 0.0).astype(F32)

    cum = [_dot_exact_lhs(tril, lw[s]) for s in chunks]
    between_stages()
    last = [x[c - 1:c, :] for x in cum]
    a_t = [_stack(a_vec[s] * jnp.exp(cu - lw[s])) for s, cu in zip(chunks, cum)]
    r_t = [_stack(r[s] * jnp.exp(cu)) for s, cu in zip(chunks, cum)]
    v_s = [_stack(v[s]) for s in chunks]
    xs = [jnp.concatenate([a, rr], axis=0).astype(BF16) for a, rr in zip(a_t, r_t)]
    ys = [jnp.concatenate([_stack(b_vec[s] * jnp.exp(-cu)), _stack(k[s] * jnp.exp(-cu))], axis=0).astype(BF16)
          for s, cu in zip(chunks, cum)]
    g = [jnp.where(keep, lax.dot_general(x, y, (((1,), (1,)), ((), ())), preferred_element_type=F32), 0.0)
         for x, y in zip(xs, ys)]
    between_stages()
    l_ab = [x[:2 * c, :2 * c] for x in g]
    l_ak = [x[:2 * c, 2 * c:] for x in g]
    l_rb = [x[2 * c:, :2 * c] for x in g]
    l_rk = [x[2 * c:, 2 * c:] for x in g]
    t_inv = [eye + x for x in l_ab]
    l_pow = l_ab
    for _ in range(int(math.log2(c)) - 1):
        l_pow = [_bdot(x, x) for x in l_pow]
        between_stages()
        t_inv = [t + _bdot(p, t) for p, t in zip(l_pow, t_inv)]
        between_stages()
    akv = [_bdot(x, y) for x, y in zip(l_ak, v_s)]
    ta = [_bdot(t, jnp.concatenate([a, u], axis=1)) for t, a, u in zip(t_inv, a_t, akv)]
    between_stages()
    rbx = [_bdot(x, y) for x, y in zip(l_rb, ta)]
    rkv = [_bdot(x, y) for x, y in zip(l_rk, v_s)]
    between_stages()
    r_new = [x + y[:, :PAIR] for x, y in zip(r_t, rbx)]
    y0 = [x[:, PAIR:] + y for x, y in zip(rbx, rkv)]
    e_rem = [jnp.exp(la - cu) for la, cu in zip(last, cum)]
    mn = [_bdot(_stack(b_vec[s] * e).T, t) for s, e, t in zip(chunks, e_rem, ta)]
    kv = [_bdot(_stack(k[s] * e).T, vv) for s, e, vv in zip(chunks, e_rem, v_s)]
    m_c = [x[:, :PAIR] + eye * jnp.exp(la) for x, la in zip(mn, last)]
    n_c = [x[:, PAIR:] + y for x, y in zip(mn, kv)]
    return list(zip(r_new, y0, m_c, n_c))


def _token_shift(cur, prev_row, mu):
    rolled = pltpu.roll(cur, 1, 0)
    first = lax.broadcasted_iota(jnp.int32, cur.shape, 0) == 0
    prev = jnp.where(first, prev_row, rolled)
    return cur + (prev - cur) * mu


def _rwkv_transforms(r, k, v, e, w0, a0, k_k, k_a, ww2, wa2, ones):
    w = -_softplus(-(w0 + jnp.dot(jnp.tanh(e).astype(BF16), ww2, preferred_element_type=F32))) - 0.5
    lw = -jnp.exp(w)
    a = _sigmoid(a0 + jnp.dot(e.astype(BF16), wa2, preferred_element_type=F32))
    kk = k * k_k
    kk = kk / jnp.maximum(jnp.sqrt(_head_sum(kk * kk, ones)), 1e-12)
    k_mod = k * (1.0 + (a - 1.0) * k_a)
    return lw, a, kk, k_mod


def _group_norm(y, ln_w, ln_b, ones):
    mean = _head_sum(y, ones) * (1.0 / RWKV_HEAD_DIM)
    d = y - mean
    var = _head_sum(d * d, ones) * (1.0 / RWKV_HEAD_DIM)
    return d * lax.rsqrt(var + GN_EPS) * ln_w + ln_b


N_WKV_INPUTS = 17
N_WKV_SCRATCH = 11


def _wkv_prompt_kernel(pt_ref, *refs, tb, nt, steps_per_row, n_decode_steps):
    refs = list(refs)
    (r_ref, k_ref, v_ref, e_ref, mur_ref, muk_ref, muv_ref, mue_ref, w0_ref, a0_ref,
     kk_ref, ka_ref, rk_ref, ww2_ref, wa2_ref, lnw_ref, lnb_ref) = refs[:N_WKV_INPUTS]
    decode_inputs = refs[N_WKV_INPUTS:N_WKV_INPUTS + N_DECODE_INPUTS]
    o_ref, st_ref, *decode_outputs = refs[N_WKV_INPUTS + N_DECODE_INPUTS:N_WKV_INPUTS + N_DECODE_INPUTS + 2 + N_DECODE_OUTPUTS]
    scratch = refs[N_WKV_INPUTS + N_DECODE_INPUTS + 2 + N_DECODE_OUTPUTS:]
    (pr_ref, pk_ref, pv_ref, pe_ref, s_ref, y_ref, rn_ref, y0_ref, mc_ref, nc_ref, bonus_ref) = scratch[:N_WKV_SCRATCH]
    n = pl.program_id(0)
    nc = tb // CHUNK
    ones = _pair_ones()

    @pl.when(n == 0)
    def _():
        for ref in (rn_ref, y0_ref, mc_ref, nc_ref, bonus_ref, s_ref):
            ref[...] = jnp.zeros_like(ref)

    @pl.when(lax.rem(n, nt) == 0)
    def _():
        for ref in (pr_ref, pk_ref, pv_ref, pe_ref):
            ref[...] = jnp.zeros_like(ref)

    prev_starts_sequence = lax.rem(n - 1, nt) == 0
    state = [jnp.where(prev_starts_sequence, 0.0, s_ref[...])]
    pending = list(range(nc))

    def recurrence_step():
        if not pending:
            return
        i = pending.pop(0)
        s = state[0]
        ys = _bdot(rn_ref[i], s) + y0_ref[i]
        y_ref[i * CHUNK:(i + 1) * CHUNK, :] = ys[:CHUNK] + ys[CHUNK:]
        state[0] = _bdot(mc_ref[i], s) + nc_ref[i]

    r_cur, k_cur, v_cur, e_cur = r_ref[0, 0], k_ref[0, 0], v_ref[0, 0], e_ref[...]
    r = _token_shift(r_cur, pr_ref[...], mur_ref[...])
    k = _token_shift(k_cur, pk_ref[...], muk_ref[...])
    v = _token_shift(v_cur, pv_ref[...], muv_ref[...])
    e = _token_shift(e_cur, pe_ref[...], mue_ref[...])
    for cur, prev in ((r_cur, pr_ref), (k_cur, pk_ref), (v_cur, pv_ref), (e_cur, pe_ref)):
        prev[...] = cur[tb - 1:tb, :]
    recurrence_step()
    lw, a, kk, k_mod = _rwkv_transforms(r, k, v, e, w0_ref[...], a0_ref[...], kk_ref[...], ka_ref[...],
                                        ww2_ref[...], wa2_ref[...], ones)
    recurrence_step()
    bonus_new = _head_sum(r * k_mod * rk_ref[...], ones) * v
    terms = _wkv_chunk_terms(r, k_mod, v, -kk, kk * a, lw, between_stages=recurrence_step)
    while pending:
        recurrence_step()
    s_ref[...] = state[0]
    st_ref[0, 0] = state[0]
    o_ref[...] = _group_norm(y_ref[...], lnw_ref[...], lnb_ref[...], ones) + bonus_ref[...]

    bonus_ref[...] = bonus_new
    for i, (r_new, y0, m_c, n_c) in enumerate(terms):
        rn_ref[i] = r_new
        y0_ref[i] = y0
        mc_ref[i] = m_c
        nc_ref[i] = n_c

    _decode_step(n, pt_ref, decode_inputs, decode_outputs, scratch[N_WKV_SCRATCH:], steps_per_row=steps_per_row,
                 n_steps=n_decode_steps)


def wkv_prompt(r, k, v, ext, mu_rkv, mu_e, w0, a0, k_k, k_a, r_k, ln_w, ln_b, ww2, wa2, decode_args, *, tb, decode_layer):
    batch, _, seq, _ = r.shape
    nt = seq // tb
    n_blocks = batch * N_PAIRS * nt
    decode = _decode_plan(*decode_args, layer=decode_layer, step_of=lambda n: n)
    assert decode["n_steps"] <= n_blocks + 1

    def where(n):
        return lax.div(n, N_PAIRS * nt), lax.rem(lax.div(n, nt), N_PAIRS), lax.rem(n, nt)

    cur = lambda n: where(jnp.minimum(n, n_blocks - 1))
    prev = lambda n: where(jnp.maximum(n - 1, 0))
    rows = pl.BlockSpec((1, 1, tb, LANES), lambda n, pt: (cur(n)[0], cur(n)[1], cur(n)[2], 0))
    vec = lambda off: pl.BlockSpec((1, LANES), lambda n, pt: (0, off + cur(n)[1]))
    vec_prev = pl.BlockSpec((1, LANES), lambda n, pt: (0, prev(n)[1]))
    lora = pl.BlockSpec((RWKV_EXTRA, LANES), lambda n, pt: (0, cur(n)[1]))
    terms = pltpu.VMEM((tb // CHUNK, PAIR, PAIR), F32)
    grid_spec = pltpu.PrefetchScalarGridSpec(
        num_scalar_prefetch=1,
        grid=(n_blocks + 1,),
        in_specs=[
            rows, rows, rows,
            pl.BlockSpec((tb, RWKV_EXTRA), lambda n, pt: (cur(n)[0] * nt + cur(n)[2], 0)),
            vec(0), vec(N_PAIRS), vec(2 * N_PAIRS),
            pl.BlockSpec((1, RWKV_EXTRA), lambda n, pt: (0, 0)),
            vec(0), vec(0), vec(0), vec(0), vec(0), lora, lora, vec_prev, vec_prev,
        ] + decode["in_specs"],
        out_specs=[
            pl.BlockSpec((tb, LANES), lambda n, pt: (prev(n)[0] * nt + prev(n)[2], prev(n)[1])),
            pl.BlockSpec((1, 1, PAIR, PAIR), lambda n, pt: (prev(n)[0], prev(n)[1], 0, 0)),
        ] + decode["out_specs"],
        scratch_shapes=[
            pltpu.VMEM((1, LANES), F32), pltpu.VMEM((1, LANES), F32), pltpu.VMEM((1, LANES), F32),
            pltpu.VMEM((1, RWKV_EXTRA), F32), pltpu.VMEM((PAIR, PAIR), F32), pltpu.VMEM((tb, LANES), F32),
            terms, terms, terms, terms, pltpu.VMEM((tb, LANES), F32),
        ] + decode["scratch_shapes"],
    )
    return pl.pallas_call(
        functools.partial(_wkv_prompt_kernel, tb=tb, nt=nt, steps_per_row=decode["steps_per_row"],
                          n_decode_steps=decode["n_steps"]),
        grid_spec=grid_spec,
        out_shape=[
            jax.ShapeDtypeStruct((batch * seq, TOK_WIDTH), F32),
            jax.ShapeDtypeStruct((batch, N_PAIRS, PAIR, PAIR), F32),
        ] + decode["out_shape"],
        compiler_params=_params("arbitrary"),
        name="wkv_prompt",
    )(decode_args[0], r, k, v, ext, mu_rkv, mu_rkv, mu_rkv, mu_e, w0, a0, k_k, k_a, r_k, ww2, wa2, ln_w, ln_b,
      *decode["args"])


def _wkv_step_kernel(cur_ref, prev_ref, e_ref, pe_ref, mu_ref, mue_ref, w0_ref, a0_ref, kk_ref, ka_ref, rk_ref,
                     lnw_ref, lnb_ref, ww2_ref, wa2_ref, st_ref, o_ref, stn_ref):
    hd = RWKV_HEAD_DIM
    e_cur = e_ref[0]
    e = e_cur + (pe_ref[0] - e_cur) * mue_ref[...]
    e_rows = jnp.broadcast_to(e, (SUBLANES, RWKV_EXTRA))
    e_tanh = jnp.tanh(e_rows).astype(BF16)
    e_lin = e_rows.astype(BF16)
    eye = lax.broadcasted_iota(jnp.int32, (hd, hd), 0) == lax.broadcasted_iota(jnp.int32, (hd, hd), 1)

    def mixed(i, h):
        cur = cur_ref[0, i, h:h + 1, :]
        return cur + (prev_ref[0, i, h:h + 1, :] - cur) * mu_ref[i, h:h + 1, :]

    for h in range(RWKV_HEADS):
        row = lambda ref: ref[h:h + 1, :]
        r, k, v = mixed(0, h), mixed(1, h), mixed(2, h)
        w = -_softplus(-(row(w0_ref) + jnp.dot(e_tanh, ww2_ref[h], preferred_element_type=F32)[0:1])) - 0.5
        decay = jnp.exp(-jnp.exp(w))
        a = _sigmoid(row(a0_ref) + jnp.dot(e_lin, wa2_ref[h], preferred_element_type=F32)[0:1])
        kk = k * row(kk_ref)
        kk = kk / jnp.maximum(jnp.sqrt(jnp.sum(kk * kk, axis=-1, keepdims=True)), 1e-12)
        k_mod = k * (1.0 + (a - 1.0) * row(ka_ref))
        s = st_ref[0, h]
        sa = jnp.sum(s * (-kk), axis=-1, keepdims=True)
        v_col = jnp.sum(jnp.where(eye, jnp.broadcast_to(v, (hd, hd)), 0.0), axis=-1, keepdims=True)
        s_new = s * decay + sa * (kk * a) + v_col * k_mod
        stn_ref[0, h] = s_new
        y_col = jnp.sum(s_new * r, axis=-1, keepdims=True)
        y = jnp.sum(jnp.where(eye, jnp.broadcast_to(y_col, (hd, hd)), 0.0), axis=0, keepdims=True)
        mean = jnp.mean(y, axis=-1, keepdims=True)
        d = y - mean
        var = jnp.mean(d * d, axis=-1, keepdims=True)
        yn = d * lax.rsqrt(var + GN_EPS) * row(lnw_ref) + row(lnb_ref)
        o_ref[0, h:h + 1, :] = yn + jnp.sum(r * k_mod * row(rk_ref), axis=-1, keepdims=True) * v


def wkv_step(cur, prev, e_cur, e_prev, mu, mu_e, w0, a0, k_k, k_a, r_k, ln_w, ln_b, ww2h, wa2h, state):
    batch = cur.shape[0]
    hshape = (RWKV_HEADS, RWKV_HEAD_DIM)
    per_row = lambda shape: pl.BlockSpec((1,) + shape, lambda b: (b,) + (0,) * len(shape))
    const = lambda shape: pl.BlockSpec(shape, lambda b: (0,) * len(shape))
    return pl.pallas_call(
        _wkv_step_kernel,
        grid=(batch,),
        in_specs=[per_row((3,) + hshape), per_row((3,) + hshape), per_row((1, RWKV_EXTRA)), per_row((1, RWKV_EXTRA)),
                  const((3,) + hshape), const((1, RWKV_EXTRA))] + [const(hshape)] * 7
                 + [const((RWKV_HEADS, RWKV_EXTRA, RWKV_HEAD_DIM))] * 2 + [per_row(hshape + (RWKV_HEAD_DIM,))],
        out_specs=[per_row(hshape), per_row(hshape + (RWKV_HEAD_DIM,))],
        out_shape=[jax.ShapeDtypeStruct((batch,) + hshape, F32),
                   jax.ShapeDtypeStruct((batch,) + hshape + (RWKV_HEAD_DIM,), F32)],
        compiler_params=_params("parallel"),
        name="wkv_step",
    )(cur, prev, e_cur, e_prev, mu, mu_e, w0, a0, k_k, k_a, r_k, ln_w, ln_b, ww2h, wa2h, state)


def _relayout_in_proj(w_in, tok_cols, extra_cols, extra_width):
    w_t = jnp.transpose(w_in).astype(BF16)
    extra = w_t[tok_cols:tok_cols + extra_cols]
    q_mem = w_t[tok_cols + extra_cols:tok_cols + extra_cols + MEM_WIDTH]
    gate = w_t[tok_cols + extra_cols + MEM_WIDTH:]
    return w_t, jnp.concatenate([gate, q_mem], axis=0), jnp.pad(extra, ((0, extra_width - extra_cols), (0, 0)))


def _pad_lanes(v, width):
    return jnp.pad(v, (0, width - v.shape[0])).reshape(1, width)


def _split_w_o(w_o):
    return w_o[:TOK_WIDTH].astype(BF16), w_o[TOK_WIDTH:].astype(BF16)


def _memory_kv(mem, g_mem, w_mem):
    kv = norm_proj(mem, g_mem, jnp.transpose(w_mem).astype(BF16), tm=mem.shape[0])
    return kv[:, :MEM_WIDTH], kv[:, MEM_WIDTH:]


def kernel(x_prompt, x_sample, cache_fox_k, cache_fox_v, cache_fox_logf, cache_mem_k, cache_mem_v, state_rwkv_wkv, state_rwkv_shift, page_table, mem_prompt, g_pre, g_post, g_mem, w_mem, w_o, fox_w_in, fox_b_f, rwkv_w_in, rwkv_mu, rwkv_w0, rwkv_w_w2, rwkv_a0, rwkv_w_a2, rwkv_k_k, rwkv_k_a, rwkv_r_k, rwkv_ln_w, rwkv_ln_b):
    bp, seq, _ = x_prompt.shape
    bs = x_sample.shape[0]
    xp = x_prompt.reshape(bp * seq, D_MODEL)
    xs = x_sample.reshape(bs, D_MODEL)
    mem = mem_prompt.reshape(bp * N_MEM, D_MODEL)
    tm_p = 1024
    mem_shape = (bp, N_MEM, MEM_HEADS, MEM_HEAD_DIM)

    w_tok, w_gm, w_ext = _relayout_in_proj(fox_w_in[0], 3 * TOK_WIDTH, FOX_HEADS, FOX_EXTRA)
    wo_tok0, wo_mem0 = _split_w_o(w_o[0])
    b_pad = _pad_lanes(fox_b_f[0], LANES)
    mk0, mv0 = _memory_kv(mem, g_mem[0], w_mem[0])

    gm_p, q_p, k_p, v_p, ext_p = norm_proj(xp, g_pre[0], w_tok, w_gm, w_ext, tm=tm_p, heads=(bp, seq))
    logf_p, ccol, crow = fox_gate(ext_p, b_pad, batch=bp, seq=seq, tb=256)
    o_tok = fox_flash(q_p, k_p, v_p, ccol, crow, tq=512)
    o_mem = mem_attn(gm_p.reshape(bp, seq, GM_WIDTH), mk0, mv0, tq=1024, q_block=D_MODEL // MEM_WIDTH)
    xp = out_proj(o_tok, o_mem.reshape(bp * seq, MEM_WIDTH), gm_p, xp, wo_tok0, wo_mem0, g_post[0], tm=256)

    main_s0, ext_s0 = norm_proj(xs, g_pre[0], w_tok, w_gm, w_ext, tm=bs)
    q_s, k_s, v_s = (main_s0[:, off:off + TOK_WIDTH] for off in (TOKA_OFF, TOKB_OFF, TOKC_OFF))
    logf_pool_t = jnp.pad(jnp.transpose(cache_fox_logf[0], (0, 2, 1)), ((0, 0), (0, HEAD_PAD - FOX_HEADS), (0, 0)))
    k_pool, v_pool = (jnp.transpose(c, (0, 1, 3, 2, 4)) for c in (cache_fox_k, cache_fox_v))
    decode_args = (page_table, q_s, k_s, v_s, ext_s0, b_pad, k_pool, v_pool, logf_pool_t)

    head_shape = (FOX_HEADS, FOX_HEAD_DIM)
    fox_k_p = jnp.transpose(k_p, (0, 2, 1, 3))[None]
    fox_v_p = jnp.transpose(v_p, (0, 2, 1, 3))[None]
    fox_f_p = logf_p[:, :FOX_HEADS].reshape(1, bp, seq, FOX_HEADS)
    fox_k_s = k_s.reshape((1, bs, 1) + head_shape)
    fox_v_s = v_s.reshape((1, bs, 1) + head_shape)

    w_tok, w_gm, w_ext = _relayout_in_proj(rwkv_w_in[0], 3 * TOK_WIDTH, 2 * LORA, RWKV_EXTRA)
    wo_tok, wo_mem = _split_w_o(w_o[1])
    mk1, mv1 = _memory_kv(mem, g_mem[1], w_mem[1])
    mu_rkv = rwkv_mu[0][:3 * TOK_WIDTH].reshape(1, 3 * TOK_WIDTH)
    mu_e = _pad_lanes(rwkv_mu[0][3 * TOK_WIDTH:], RWKV_EXTRA)
    ww2 = jnp.pad(rwkv_w_w2[0], ((0, RWKV_EXTRA - LORA), (0, 0))).astype(BF16)
    wa2 = jnp.pad(rwkv_w_a2[0], ((LORA, RWKV_EXTRA - 2 * LORA), (0, 0))).astype(BF16)
    vecs = [p[0] for p in (rwkv_w0, rwkv_a0, rwkv_k_k, rwkv_k_a, rwkv_r_k, rwkv_ln_w, rwkv_ln_b)]

    gm_p, r_p, k_p, v_p, ext_p = norm_proj(xp, g_pre[1], w_tok, w_gm, w_ext, tm=tm_p, heads=(bp, seq))
    o_tok, st_pairs, o_tok_s, logf_s = wkv_prompt(r_p, k_p, v_p, ext_p, mu_rkv, mu_e,
                                                  *[v.reshape(1, TOK_WIDTH) for v in vecs], ww2, wa2, decode_args,
                                                  tb=512, decode_layer=0)
    o_mem = mem_attn(gm_p.reshape(bp, seq, GM_WIDTH), mk1, mv1, tq=1024, q_block=D_MODEL // MEM_WIDTH)
    xp = out_proj(o_tok, o_mem.reshape(bp * seq, MEM_WIDTH), gm_p, xp, wo_tok, wo_mem, g_post[1], tm=256)

    fox_f_s = logf_s[:, :FOX_HEADS].reshape(1, bs, 1, FOX_HEADS)
    o_mem_s = mem_attn(main_s0.reshape(bs, 1, MAIN_WIDTH), cache_mem_k[0].reshape(bs * N_MEM, MEM_WIDTH),
                       cache_mem_v[0].reshape(bs * N_MEM, MEM_WIDTH), tq=1, q_block=QMEM_OFF // MEM_WIDTH)
    xs = out_proj(o_tok_s, o_mem_s.reshape(bs, MEM_WIDTH), main_s0, xs, wo_tok0, wo_mem0, g_post[0], tm=bs)

    main_s, ext_s = norm_proj(xs, g_pre[1], w_tok, w_gm, w_ext, tm=bs)
    hshape = (RWKV_HEADS, RWKV_HEAD_DIM)
    shift_prev = state_rwkv_shift[0][:, 0, :]
    cur = main_s[:, TOKA_OFF:QMEM_OFF].reshape((bs, 3) + hshape)
    prev = shift_prev[:, :3 * TOK_WIDTH].reshape((bs, 3) + hshape)
    e_prev = jnp.pad(shift_prev[:, 3 * TOK_WIDTH:], ((0, 0), (0, RWKV_EXTRA - 2 * LORA))).reshape(bs, 1, RWKV_EXTRA)
    per_head = lambda w: jnp.transpose(w.reshape((RWKV_EXTRA,) + hshape), (1, 0, 2))
    o_tok_s, wkv_s = wkv_step(cur, prev, ext_s.reshape(bs, 1, RWKV_EXTRA), e_prev, mu_rkv.reshape((3,) + hshape), mu_e,
                              *[v.reshape(hshape) for v in vecs], per_head(ww2), per_head(wa2), state_rwkv_wkv[0])
    o_mem_s = mem_attn(main_s.reshape(bs, 1, MAIN_WIDTH), cache_mem_k[1].reshape(bs * N_MEM, MEM_WIDTH),
                       cache_mem_v[1].reshape(bs * N_MEM, MEM_WIDTH), tq=1, q_block=QMEM_OFF // MEM_WIDTH)
    xs = out_proj(o_tok_s.reshape(bs, TOK_WIDTH), o_mem_s.reshape(bs, MEM_WIDTH), main_s, xs, wo_tok, wo_mem, g_post[1], tm=bs)

    st6 = st_pairs.reshape(bp, N_PAIRS, 2, RWKV_HEAD_DIM, 2, RWKV_HEAD_DIM)
    st_heads = jnp.stack([st6[:, :, 0, :, 0, :], st6[:, :, 1, :, 1, :]], axis=2)
    wkv_p = jnp.swapaxes(st_heads.reshape((bp,) + hshape + (RWKV_HEAD_DIM,)), -1, -2)
    last_p = jnp.concatenate([a[:, :, seq - 1, :].reshape(bp, TOK_WIDTH) for a in (r_p, k_p, v_p)], axis=-1)
    last_e = ext_p.reshape(bp, seq, RWKV_EXTRA)[:, seq - 1, :2 * LORA]
    shift_p = jnp.concatenate([last_p, last_e], axis=-1).reshape(1, bp, 1, 3 * TOK_WIDTH + 2 * LORA)
    shift_s = jnp.concatenate([main_s[:, TOKA_OFF:QMEM_OFF], ext_s[:, :2 * LORA]], axis=-1).reshape(1, bs, 1, 3 * TOK_WIDTH + 2 * LORA)

    return (xp.reshape(bp, seq, D_MODEL), xs.reshape(bs, 1, D_MODEL),
            jnp.stack([mk0.reshape(mem_shape), mk1.reshape(mem_shape)]),
            jnp.stack([mv0.reshape(mem_shape), mv1.reshape(mem_shape)]),
            fox_k_p, fox_v_p, fox_f_p, fox_k_s, fox_v_s, fox_f_s,
            wkv_p[None], shift_p, wkv_s[None], shift_s)
```

```python
import functools
import math

import jax
import jax.numpy as jnp
from jax import lax
from jax.experimental import pallas as pl
from jax.experimental.pallas import tpu as pltpu

F32 = jnp.float32
BF16 = jnp.bfloat16

D_MODEL = 2048
N_MEM = 256
MEM_HEADS = 4
MEM_HEAD_DIM = 128
MEM_WIDTH = MEM_HEADS * MEM_HEAD_DIM
TOK_WIDTH = D_MODEL - MEM_WIDTH
FOX_HEAD_DIM = 128
FOX_HEADS = TOK_WIDTH // FOX_HEAD_DIM
RWKV_HEAD_DIM = 64
RWKV_HEADS = TOK_WIDTH // RWKV_HEAD_DIM
LORA = 96
PAGE_SIZE = 128
RMS_EPS = 1e-6
GN_EPS = 64e-5

LANES = 128
SUBLANES = 8
V7X_VMEM_BYTES = 64 * 1024 * 1024
VMEM_LIMIT_BYTES = V7X_VMEM_BYTES - 8 * 1024 * 1024

GATE_OFF = 0
TOKA_OFF = D_MODEL
TOKB_OFF = TOKA_OFF + TOK_WIDTH
TOKC_OFF = TOKB_OFF + TOK_WIDTH
QMEM_OFF = TOKC_OFF + TOK_WIDTH
MAIN_WIDTH = QMEM_OFF + MEM_WIDTH
FOX_EXTRA = LANES
RWKV_EXTRA = 2 * LANES

HEAD_PAD = 16
CHUNK = 64


def _params(*sem):
    return pltpu.CompilerParams(dimension_semantics=sem, vmem_limit_bytes=VMEM_LIMIT_BYTES)


def _rms_scale(x):
    return lax.rsqrt(jnp.mean(x * x, axis=-1, keepdims=True) + RMS_EPS)


def _sigmoid(x):
    return 1.0 / (1.0 + jnp.exp(-x))


def _softplus(x):
    return jnp.maximum(x, 0.0) + jnp.log(1.0 + jnp.exp(-jnp.abs(x)))


def _split(x, pieces):
    out = []
    for _ in range(pieces - 1):
        p = x.astype(BF16)
        out.append(p)
        x = x - p.astype(F32)
    return out + [x.astype(BF16)]


def _dot_exact_rhs(x, rhs, pieces=3):
    rows = x.shape[0]
    parts = jnp.dot(jnp.concatenate(_split(x, pieces), axis=0), rhs.astype(BF16), preferred_element_type=F32)
    return sum(parts[i * rows:(i + 1) * rows] for i in range(pieces))


def _dot_exact_lhs(lhs, x, pieces=3):
    cols = x.shape[1]
    parts = jnp.dot(lhs.astype(BF16), jnp.concatenate(_split(x, pieces), axis=1), preferred_element_type=F32)
    return sum(parts[:, i * cols:(i + 1) * cols] for i in range(pieces))


TN = 4 * LANES
GATE_BLOCKS = D_MODEL // TN
TOK_BLOCKS = TOK_WIDTH // TN
QMEM_BLOCK = GATE_BLOCKS + 3 * TOK_BLOCKS
GM_WIDTH = D_MODEL + MEM_WIDTH
N_GROUPS128 = TOK_WIDTH // LANES
assert MAIN_WIDTH == (QMEM_BLOCK + 1) * TN and MEM_WIDTH == TN


def _nt_dot(a, b):
    return lax.dot_general(a, b, (((1,), (1,)), ((), ())), preferred_element_type=F32)


def _norm_proj_kernel(x_ref, g_ref, wt_ref, *rest, has_gm, has_extra, head_mode):
    rest = list(rest)
    wg_ref = rest.pop(0) if has_gm else None
    we_ref = rest.pop(0) if has_extra else None
    h_ref = rest.pop()
    oe_ref = rest.pop() if has_extra else None
    j = pl.program_id(1)

    @pl.when(j == 0)
    def _():
        x = x_ref[...]
        h = (x * _rms_scale(x) * g_ref[...]).astype(BF16)
        h_ref[...] = h
        if has_extra:
            oe_ref[...] = _nt_dot(h, we_ref[...])

    if not has_gm:
        rest[0][...] = jnp.dot(h_ref[...], wt_ref[...].astype(BF16), preferred_element_type=F32)
        return
    in_gm = (j < GATE_BLOCKS) | (j == QMEM_BLOCK)

    @pl.when(in_gm)
    def _():
        rest[0][...] = _nt_dot(h_ref[...], wg_ref[...])

    if not head_mode:
        @pl.when(jnp.logical_not(in_gm))
        def _():
            rest[0][...] = _nt_dot(h_ref[...], wt_ref[...])
        return

    for gi, ref in enumerate(rest[1:]):
        first = GATE_BLOCKS + gi * TOK_BLOCKS

        @pl.when((j >= first) & (j < first + TOK_BLOCKS))
        def _(ref=ref):
            res = _nt_dot(h_ref[...], wt_ref[...])
            for hh in range(TN // LANES):
                ref[0, hh] = res[:, hh * LANES:(hh + 1) * LANES]


def norm_proj(x, g, w, w_gm_t=None, w_extra_t=None, *, tm, heads=None):
    m, d = x.shape
    has_gm = w_gm_t is not None
    has_extra = w_extra_t is not None
    n = MAIN_WIDTH if has_gm else w.shape[1]
    assert m % tm == 0 and n % TN == 0 and (has_gm or heads is None)
    gm_col = lambda j: jnp.where(j == QMEM_BLOCK, GATE_BLOCKS, jnp.minimum(j, GATE_BLOCKS - 1))
    gm_row = lambda j: jnp.where(j == QMEM_BLOCK, 0, jnp.minimum(j, GATE_BLOCKS - 1) + MEM_WIDTH // TN)
    in_specs = [pl.BlockSpec((tm, d), lambda i, j: (i, 0)), pl.BlockSpec((1, d), lambda i, j: (0, 0))]
    args = [x, g.reshape(1, d), w]
    if has_gm:
        in_specs.append(pl.BlockSpec((TN, d), lambda i, j: (jnp.clip(j - GATE_BLOCKS, 0, 3 * TOK_BLOCKS - 1), 0)))
        in_specs.append(pl.BlockSpec((TN, d), lambda i, j: (gm_row(j), 0)))
        args.append(w_gm_t)
    else:
        in_specs.append(pl.BlockSpec((d, TN), lambda i, j: (0, j)))
    if has_extra:
        in_specs.append(pl.BlockSpec(w_extra_t.shape, lambda i, j: (0, 0)))
        args.append(w_extra_t)
    if heads is None:
        out_specs = [pl.BlockSpec((tm, TN), lambda i, j: (i, j))]
        out_shape = [jax.ShapeDtypeStruct((m, n), F32)]
    else:
        batch, seq = heads
        assert seq % tm == 0
        per_seq = seq // tm
        out_specs = [pl.BlockSpec((tm, TN), lambda i, j: (i, gm_col(j)))]
        out_shape = [jax.ShapeDtypeStruct((m, GM_WIDTH), F32)]
        for gi in range(3):
            first = GATE_BLOCKS + gi * TOK_BLOCKS
            out_specs.append(pl.BlockSpec(
                (1, TN // LANES, tm, LANES),
                lambda i, j, first=first: (i // per_seq, jnp.clip(j - first, 0, TOK_BLOCKS - 1), i % per_seq, 0)))
            out_shape.append(jax.ShapeDtypeStruct((batch, N_GROUPS128, seq, LANES), F32))
    if has_extra:
        e = w_extra_t.shape[0]
        out_specs.append(pl.BlockSpec((tm, e), lambda i, j: (i, 0)))
        out_shape.append(jax.ShapeDtypeStruct((m, e), F32))
    outs = pl.pallas_call(
        functools.partial(_norm_proj_kernel, has_gm=has_gm, has_extra=has_extra, head_mode=heads is not None),
        grid=(m // tm, n // TN),
        in_specs=in_specs,
        out_specs=out_specs,
        out_shape=out_shape,
        scratch_shapes=[pltpu.VMEM((tm, d), BF16)],
        compiler_params=_params("parallel", "arbitrary"),
        name="norm_proj",
    )(*args)
    return outs if len(outs) > 1 else outs[0]


def _fox_gate_kernel(f_ref, b_ref, logf_ref, ccol_ref, crow_ref, carry_ref, *, tb):
    @pl.when(pl.program_id(1) == 0)
    def _():
        carry_ref[...] = jnp.zeros_like(carry_ref)

    logf = -_softplus(-(f_ref[...] + b_ref[...]))
    logf_ref[...] = logf
    row = lax.broadcasted_iota(jnp.int32, (tb, tb), 0)
    col = lax.broadcasted_iota(jnp.int32, (tb, tb), 1)
    c = _dot_exact_lhs(jnp.where(col <= row, 1.0, 0.0), logf) + carry_ref[...]
    ccol_ref[...] = c
    crow_ref[0] = c.T[:HEAD_PAD]
    carry_ref[...] = c[tb - 1:tb, :]


def fox_gate(f_ext, b_pad, *, batch, seq, tb):
    nt = seq // tb
    row_spec = pl.BlockSpec((tb, LANES), lambda b, t: (b * nt + t, 0))
    return pl.pallas_call(
        functools.partial(_fox_gate_kernel, tb=tb),
        grid=(batch, nt),
        in_specs=[row_spec, pl.BlockSpec((1, LANES), lambda b, t: (0, 0))],
        out_specs=[row_spec, row_spec, pl.BlockSpec((1, HEAD_PAD, tb), lambda b, t: (b, 0, t))],
        out_shape=[
            jax.ShapeDtypeStruct((batch * seq, LANES), F32),
            jax.ShapeDtypeStruct((batch * seq, LANES), F32),
            jax.ShapeDtypeStruct((batch, HEAD_PAD, seq), F32),
        ],
        scratch_shapes=[pltpu.VMEM((1, LANES), F32)],
        compiler_params=_params("parallel", "arbitrary"),
        name="fox_gate",
    )(f_ext, b_pad)


_MASKED = -1e30
LOG2E = 1.4426950408889634


def _flash_tile(h, qi, q_ref, k_ref, v_ref, ccol_ref, crow_ref, o_ref, kb_ref, vb_ref, *, tq):
    @pl.when(qi == 0)
    def _():
        kb_ref[...] = k_ref[0, 0].astype(BF16)
        vb_ref[...] = v_ref[0, 0].astype(BF16)

    q = (q_ref[0, 0] * (FOX_HEAD_DIM ** -0.5 * LOG2E)).astype(BF16)
    lane = lax.broadcasted_iota(jnp.int32, (tq, LANES), 1)
    cq = jnp.sum(jnp.where(lane == h, ccol_ref[...], 0.0), axis=-1, keepdims=True) * LOG2E

    def step(c, carry, diagonal):
        m, l, acc = carry
        start = pl.multiple_of(c * tq, tq)
        kc = kb_ref[pl.ds(start, tq), :]
        vc = vb_ref[pl.ds(start, tq), :]
        ck = crow_ref[0, 0, :, pl.ds(start, tq)] * LOG2E
        s = lax.dot_general(q, kc, (((1,), (1,)), ((), ())), preferred_element_type=F32) - ck
        if diagonal:
            row = lax.broadcasted_iota(jnp.int32, (tq, tq), 0)
            col = lax.broadcasted_iota(jnp.int32, (tq, tq), 1)
            s = jnp.where(col <= row, s, _MASKED)
        m_new = jnp.maximum(m, jnp.max(s, axis=-1, keepdims=True) + cq)
        p = jnp.exp2(s - (m_new - cq))
        alpha = jnp.exp2(m - m_new)
        l = alpha * l + jnp.sum(p, axis=-1, keepdims=True)
        acc = alpha * acc + jnp.dot(p.astype(BF16), vc, preferred_element_type=F32)
        return m_new, l, acc

    init = (jnp.full((tq, 1), _MASKED, F32), jnp.zeros((tq, 1), F32), jnp.zeros((tq, FOX_HEAD_DIM), F32))
    carry = lax.fori_loop(0, qi, lambda c, cr: step(c, cr, False), init)
    _, l, acc = step(qi, carry, True)
    o_ref[...] = acc / l


PAGES_PER_STEP = 8


def _head_mask():
    row = lax.broadcasted_iota(jnp.int32, (HEAD_PAD, TOK_WIDTH), 0)
    col = lax.broadcasted_iota(jnp.int32, (HEAD_PAD, TOK_WIDTH), 1)
    lo = row * FOX_HEAD_DIM
    return (col >= lo) & (col < lo + FOX_HEAD_DIM)


def _decode_pages(b, j, last, pt_ref, q_ref, kn_ref, vn_ref, f_ref, bf_ref, lf_ref, k_refs, v_refs,
                  o_ref, lfn_ref, qbd_ref, m_ref, l_ref, acc_ref, carry_ref):
    pps = PAGES_PER_STEP
    n_pages = pt_ref.shape[1]
    scale = FOX_HEAD_DIM ** -0.5
    hmask = _head_mask()

    @pl.when(j == 0)
    def _():
        q = q_ref[pl.ds(b, 1), :]
        q_bd = jnp.where(hmask, jnp.broadcast_to(q, (HEAD_PAD, TOK_WIDTH)), 0.0)
        qbd_ref[...] = q_bd.astype(BF16)
        logf_new = -_softplus(-(f_ref[pl.ds(b, 1), :] + bf_ref[...]))
        lfn_ref[pl.ds(b, 1), :] = logf_new
        eye = lax.broadcasted_iota(jnp.int32, (HEAD_PAD, LANES), 0) == lax.broadcasted_iota(jnp.int32, (HEAD_PAD, LANES), 1)
        carry_ref[...] = jnp.sum(jnp.where(eye, jnp.broadcast_to(logf_new, (HEAD_PAD, LANES)), 0.0), axis=-1, keepdims=True)
        m_ref[...] = jnp.sum(q_bd * kn_ref[pl.ds(b, 1), :], axis=-1, keepdims=True) * scale
        l_ref[...] = jnp.ones_like(l_ref)
        acc_ref[...] = jnp.broadcast_to(vn_ref[pl.ds(b, 1), :], (HEAD_PAD, TOK_WIDTH))

    def page_rows(ref):
        return jnp.concatenate([ref[0, 0, h].astype(BF16) for h in range(FOX_HEADS)], axis=1)

    lfs = [lf_ref[pt_ref[b, n_pages - 1 - j * pps - i]] for i in range(pps)]
    src = lax.broadcasted_iota(jnp.int32, (PAGE_SIZE, PAGE_SIZE), 0)
    dst = lax.broadcasted_iota(jnp.int32, (PAGE_SIZE, PAGE_SIZE), 1)
    inside = _dot_exact_rhs(jnp.concatenate(lfs, axis=0), jnp.where(src > dst, 1.0, 0.0))
    carries = [carry_ref[...]]
    for lf in lfs:
        carries.append(carries[-1] + jnp.sum(lf, axis=-1, keepdims=True))
    carry_ref[...] = carries[-1]

    q_bd = qbd_ref[...]
    scores = []
    for i in range(pps):
        s = lax.dot_general(q_bd, page_rows(k_refs[i]), (((1,), (1,)), ((), ())), preferred_element_type=F32)
        scores.append(s * scale + (inside[i * HEAD_PAD:(i + 1) * HEAD_PAD] + carries[i]))
    m_old = m_ref[...]
    m_new = m_old
    for s in scores:
        m_new = jnp.maximum(m_new, jnp.max(s, axis=-1, keepdims=True))
    alpha = jnp.exp(m_old - m_new)
    l_new = alpha * l_ref[...]
    acc = alpha * acc_ref[...]
    for i, s in enumerate(scores):
        p = jnp.exp(s - m_new)
        l_new = l_new + jnp.sum(p, axis=-1, keepdims=True)
        acc = acc + jnp.dot(p.astype(BF16), page_rows(v_refs[i]), preferred_element_type=F32)
    m_ref[...] = m_new
    l_ref[...] = l_new
    acc_ref[...] = acc

    @pl.when(last)
    def _():
        out = jnp.where(hmask, acc / l_new, 0.0)
        o_ref[pl.ds(b, 1), :] = jnp.sum(out, axis=0, keepdims=True)


def _decode_plan(page_table, q_s, k_new, v_new, f_ext, b_pad, k_pool, v_pool, logf_pool_t, *, layer, step_of):
    rows, n_pages = page_table.shape
    pps = PAGES_PER_STEP
    assert n_pages % pps == 0
    steps_per_row = n_pages // pps
    n_steps = rows * steps_per_row

    def page_id(i):
        def index(*idx):
            pt = idx[-1]
            d = jnp.minimum(step_of(*idx[:-1]), n_steps - 1)
            return pt[lax.div(d, steps_per_row), n_pages - 1 - lax.rem(d, steps_per_row) * pps - i]
        return index

    whole = lambda shape: pl.BlockSpec(shape, lambda *idx: (0,) * len(shape))
    page_spec = lambda i: pl.BlockSpec((1, 1, FOX_HEADS, PAGE_SIZE, FOX_HEAD_DIM),
                                       lambda *idx: (layer, page_id(i)(*idx), 0, 0, 0))
    return dict(
        in_specs=[whole((rows, TOK_WIDTH))] * 3 + [whole((rows, LANES)), whole((1, LANES)), whole(logf_pool_t.shape)]
                 + [page_spec(i) for i in range(pps)] * 2,
        args=[q_s, k_new, v_new, f_ext, b_pad, logf_pool_t] + [k_pool] * pps + [v_pool] * pps,
        out_specs=[whole((rows, TOK_WIDTH)), whole((rows, LANES))],
        out_shape=[jax.ShapeDtypeStruct((rows, TOK_WIDTH), F32), jax.ShapeDtypeStruct((rows, LANES), F32)],
        scratch_shapes=[
            pltpu.VMEM((HEAD_PAD, TOK_WIDTH), BF16),
            pltpu.VMEM((HEAD_PAD, 1), F32),
            pltpu.VMEM((HEAD_PAD, 1), F32),
            pltpu.VMEM((HEAD_PAD, TOK_WIDTH), F32),
            pltpu.VMEM((HEAD_PAD, 1), F32),
        ],
        steps_per_row=steps_per_row, n_steps=n_steps,
    )


N_DECODE_INPUTS = 6 + 2 * PAGES_PER_STEP
N_DECODE_OUTPUTS = 2
N_DECODE_SCRATCH = 5


def _decode_step(n, pt_ref, inputs, outputs, scratch, *, steps_per_row, n_steps):
    pps = PAGES_PER_STEP
    q_ref, kn_ref, vn_ref, f_ref, bf_ref, lf_ref = inputs[:6]
    k_refs, v_refs = inputs[6:6 + pps], inputs[6 + pps:]

    @pl.when(n < n_steps)
    def _():
        j = lax.rem(n, steps_per_row)
        _decode_pages(lax.div(n, steps_per_row), j, j == steps_per_row - 1, pt_ref, q_ref, kn_ref, vn_ref, f_ref,
                      bf_ref, lf_ref, k_refs, v_refs, *outputs, *scratch)


def _fox_flash_kernel(q_ref, k_ref, v_ref, ccol_ref, crow_ref, o_ref, kb_ref, vb_ref, *, tq):
    _flash_tile(pl.program_id(1), pl.program_id(2), q_ref, k_ref, v_ref, ccol_ref, crow_ref, o_ref, kb_ref, vb_ref, tq=tq)


def fox_flash(q, k, v, ccol, crow, *, tq):
    batch, _, seq, _ = q.shape
    nq = seq // tq
    crow4 = crow.reshape(batch, HEAD_PAD, 1, seq)
    kv_spec = pl.BlockSpec((1, 1, seq, FOX_HEAD_DIM), lambda b, h, i: (b, h, 0, 0))
    return pl.pallas_call(
        functools.partial(_fox_flash_kernel, tq=tq),
        grid=(batch, FOX_HEADS, nq),
        in_specs=[
            pl.BlockSpec((1, 1, tq, FOX_HEAD_DIM), lambda b, h, i: (b, h, i, 0)),
            kv_spec, kv_spec,
            pl.BlockSpec((tq, LANES), lambda b, h, i: (b * nq + i, 0)),
            pl.BlockSpec((1, 1, 1, seq), lambda b, h, i: (b, h, 0, 0)),
        ],
        out_specs=pl.BlockSpec((tq, LANES), lambda b, h, i: (b * nq + i, h)),
        out_shape=jax.ShapeDtypeStruct((batch * seq, TOK_WIDTH), F32),
        scratch_shapes=[pltpu.VMEM((seq, LANES), BF16), pltpu.VMEM((seq, LANES), BF16)],
        compiler_params=_params("parallel", "parallel", "arbitrary"),
        name="fox_flash",
    )(q, k, v, ccol, crow4)


def _mem_attn_kernel(q_ref, k_ref, v_ref, o_ref, *, tq):
    rows = max(tq, SUBLANES)
    scale = MEM_HEAD_DIM ** -0.5
    for h in range(MEM_HEADS):
        sl = slice(h * MEM_HEAD_DIM, (h + 1) * MEM_HEAD_DIM)
        qh = jnp.broadcast_to(q_ref[0, :, sl], (rows, MEM_HEAD_DIM)).astype(BF16)
        kh = k_ref[:, sl].astype(BF16)
        vh = v_ref[:, sl].astype(BF16)
        s = lax.dot_general(qh, kh, (((1,), (1,)), ((), ())), preferred_element_type=F32) * scale
        e = jnp.exp(s - jnp.max(s, axis=-1, keepdims=True))
        oh = jnp.dot(e.astype(BF16), vh, preferred_element_type=F32) / jnp.sum(e, axis=-1, keepdims=True)
        o_ref[0, :, sl] = oh[:tq]


def mem_attn(main3, mem_k, mem_v, *, tq, q_block):
    batch, seq, _ = main3.shape
    kv_spec = pl.BlockSpec((N_MEM, MEM_WIDTH), lambda b, i: (b, 0))
    return pl.pallas_call(
        functools.partial(_mem_attn_kernel, tq=tq),
        grid=(batch, seq // tq),
        in_specs=[pl.BlockSpec((1, tq, MEM_WIDTH), lambda b, i: (b, i, q_block)), kv_spec, kv_spec],
        out_specs=pl.BlockSpec((1, tq, MEM_WIDTH), lambda b, i: (b, i, 0)),
        out_shape=jax.ShapeDtypeStruct((batch, seq, MEM_WIDTH), F32),
        compiler_params=_params("parallel", "parallel"),
        name="mem_attn",
    )(main3, mem_k, mem_v)


def _out_proj_kernel(ot_ref, om_ref, gate_ref, x_ref, w_ref, gp_ref, y_ref):
    gate = gate_ref[...]
    sg = gate * _sigmoid(gate)
    a = (ot_ref[...] * sg[:, :TOK_WIDTH]).astype(BF16)
    b = (om_ref[...] * sg[:, TOK_WIDTH:]).astype(BF16)
    y = (jnp.dot(a, w_ref[0, :TOK_WIDTH], preferred_element_type=F32)
         + jnp.dot(b, w_ref[0, TOK_WIDTH:], preferred_element_type=F32))
    y_ref[...] = x_ref[...] + y * _rms_scale(y) * gp_ref[...]


def out_proj(o_tok, o_mem, main, x, w_o, g_post, *, layer, tm):
    m = x.shape[0]
    row = lambda width: pl.BlockSpec((tm, width), lambda i: (i, 0))
    return pl.pallas_call(
        _out_proj_kernel,
        grid=(m // tm,),
        in_specs=[row(TOK_WIDTH), row(MEM_WIDTH), row(D_MODEL), row(D_MODEL),
                  pl.BlockSpec((1, D_MODEL, D_MODEL), lambda i: (layer, 0, 0)),
                  pl.BlockSpec((1, D_MODEL), lambda i: (0, 0))],
        out_specs=row(D_MODEL),
        out_shape=jax.ShapeDtypeStruct((m, D_MODEL), F32),
        compiler_params=_params("parallel"),
        name="out_proj",
    )(o_tok, o_mem, main, x, w_o, g_post.reshape(1, D_MODEL))


PAIR = 2 * RWKV_HEAD_DIM
assert PAIR == LANES
N_PAIRS = RWKV_HEADS // 2


def _pair_ones():
    row = lax.broadcasted_iota(jnp.int32, (PAIR, PAIR), 0) // RWKV_HEAD_DIM
    col = lax.broadcasted_iota(jnp.int32, (PAIR, PAIR), 1) // RWKV_HEAD_DIM
    return jnp.where(row == col, 1.0, 0.0).astype(F32)


def _head_sum(x, ones):
    return _dot_exact_rhs(x, ones, pieces=2)


def _stack(x):
    lane = lax.broadcasted_iota(jnp.int32, x.shape, 1)
    first = lane < RWKV_HEAD_DIM
    return jnp.concatenate([jnp.where(first, x, 0.0), jnp.where(first, 0.0, x)], axis=0)


def _bdot(a, b):
    return jnp.dot(a.astype(BF16), b.astype(BF16), preferred_element_type=F32)


def _wkv_chunk_terms(r, k, v, a_vec, b_vec, lw, between_stages=lambda: None):
    c = CHUNK
    chunks = [slice(c0, c0 + c) for c0 in range(0, r.shape[0], c)]
    row = lax.broadcasted_iota(jnp.int32, (c, c), 0)
    col = lax.broadcasted_iota(jnp.int32, (c, c), 1)
    tril = jnp.where(col <= row, 1.0, 0.0)
    gi = lax.broadcasted_iota(jnp.int32, (4 * c, 4 * c), 0)
    gj = lax.broadcasted_iota(jnp.int32, (4 * c, 4 * c), 1)
    keep = (gj & (c - 1)) < (gi & (c - 1)) + gi // (2 * c)
    eye = jnp.where(lax.broadcasted_iota(jnp.int32, (2 * c, 2 * c), 0) == lax.broadcasted_iota(jnp.int32, (2 * c, 2 * c), 1),
                    1.0, 0.0).astype(F32)

    cum = [_dot_exact_lhs(tril, lw[s]) for s in chunks]
    between_stages()
    last = [x[c - 1:c, :] for x in cum]
    a_t = [_stack(a_vec[s] * jnp.exp(cu - lw[s])) for s, cu in zip(chunks, cum)]
    r_t = [_stack(r[s] * jnp.exp(cu)) for s, cu in zip(chunks, cum)]
    v_s = [_stack(v[s]) for s in chunks]
    xs = [jnp.concatenate([a, rr], axis=0).astype(BF16) for a, rr in zip(a_t, r_t)]
    ys = [jnp.concatenate([_stack(b_vec[s] * jnp.exp(-cu)), _stack(k[s] * jnp.exp(-cu))], axis=0).astype(BF16)
          for s, cu in zip(chunks, cum)]
    g = [jnp.where(keep, lax.dot_general(x, y, (((1,), (1,)), ((), ())), preferred_element_type=F32), 0.0)
         for x, y in zip(xs, ys)]
    between_stages()
    l_ab = [x[:2 * c, :2 * c] for x in g]
    l_ak = [x[:2 * c, 2 * c:] for x in g]
    l_rb = [x[2 * c:, :2 * c] for x in g]
    l_rk = [x[2 * c:, 2 * c:] for x in g]
    t_inv = [eye + x for x in l_ab]
    l_pow = l_ab
    for _ in range(int(math.log2(c)) - 1):
        l_pow = [_bdot(x, x) for x in l_pow]
        between_stages()
        t_inv = [t + _bdot(p, t) for p, t in zip(l_pow, t_inv)]
        between_stages()
    akv = [_bdot(x, y) for x, y in zip(l_ak, v_s)]
    ta = [_bdot(t, jnp.concatenate([a, u], axis=1)) for t, a, u in zip(t_inv, a_t, akv)]
    between_stages()
    rbx = [_bdot(x, y) for x, y in zip(l_rb, ta)]
    rkv = [_bdot(x, y) for x, y in zip(l_rk, v_s)]
    between_stages()
    r_new = [x + y[:, :PAIR] for x, y in zip(r_t, rbx)]
    y0 = [x[:, PAIR:] + y for x, y in zip(rbx, rkv)]
    e_rem = [jnp.exp(la - cu) for la, cu in zip(last, cum)]
    mn = [_bdot(_stack(b_vec[s] * e).T, t) for s, e, t in zip(chunks, e_rem, ta)]
    kv = [_bdot(_stack(k[s] * e).T, vv) for s, e, vv in zip(chunks, e_rem, v_s)]
    m_c = [x[:, :PAIR] + eye * jnp.exp(la) for x, la in zip(mn, last)]
    n_c = [x[:, PAIR:] + y for x, y in zip(mn, kv)]
    return list(zip(r_new, y0, m_c, n_c))


def _token_shift(cur, prev_row, mu):
    rolled = pltpu.roll(cur, 1, 0)
    first = lax.broadcasted_iota(jnp.int32, cur.shape, 0) == 0
    prev = jnp.where(first, prev_row, rolled)
    return cur + (prev - cur) * mu


def _rwkv_transforms(r, k, v, e, w0, a0, k_k, k_a, ww2, wa2, ones):
    w = -_softplus(-(w0 + jnp.dot(jnp.tanh(e).astype(BF16), ww2, preferred_element_type=F32))) - 0.5
    lw = -jnp.exp(w)
    a = _sigmoid(a0 + jnp.dot(e.astype(BF16), wa2, preferred_element_type=F32))
    kk = k * k_k
    kk = kk / jnp.maximum(jnp.sqrt(_head_sum(kk * kk, ones)), 1e-12)
    k_mod = k * (1.0 + (a - 1.0) * k_a)
    return lw, a, kk, k_mod


def _group_norm(y, ln_w, ln_b, ones):
    mean = _head_sum(y, ones) * (1.0 / RWKV_HEAD_DIM)
    d = y - mean
    var = _head_sum(d * d, ones) * (1.0 / RWKV_HEAD_DIM)
    return d * lax.rsqrt(var + GN_EPS) * ln_w + ln_b


N_WKV_INPUTS = 17
N_WKV_SCRATCH = 11


def _wkv_prompt_kernel(pt_ref, *refs, tb, nt, steps_per_row, n_decode_steps):
    refs = list(refs)
    (r_ref, k_ref, v_ref, e_ref, mur_ref, muk_ref, muv_ref, mue_ref, w0_ref, a0_ref,
     kk_ref, ka_ref, rk_ref, ww2_ref, wa2_ref, lnw_ref, lnb_ref) = refs[:N_WKV_INPUTS]
    decode_inputs = refs[N_WKV_INPUTS:N_WKV_INPUTS + N_DECODE_INPUTS]
    o_ref, st_ref, *decode_outputs = refs[N_WKV_INPUTS + N_DECODE_INPUTS:N_WKV_INPUTS + N_DECODE_INPUTS + 2 + N_DECODE_OUTPUTS]
    scratch = refs[N_WKV_INPUTS + N_DECODE_INPUTS + 2 + N_DECODE_OUTPUTS:]
    (pr_ref, pk_ref, pv_ref, pe_ref, s_ref, y_ref, rn_ref, y0_ref, mc_ref, nc_ref, bonus_ref) = scratch[:N_WKV_SCRATCH]
    n = pl.program_id(0)
    nc = tb // CHUNK
    ones = _pair_ones()

    @pl.when(n == 0)
    def _():
        for ref in (rn_ref, y0_ref, mc_ref, nc_ref, bonus_ref, s_ref):
            ref[...] = jnp.zeros_like(ref)

    @pl.when(lax.rem(n, nt) == 0)
    def _():
        for ref in (pr_ref, pk_ref, pv_ref, pe_ref):
            ref[...] = jnp.zeros_like(ref)

    prev_starts_sequence = lax.rem(n - 1, nt) == 0
    state = [jnp.where(prev_starts_sequence, 0.0, s_ref[...])]
    pending = list(range(nc))

    def recurrence_step():
        if not pending:
            return
        i = pending.pop(0)
        s = state[0]
        ys = _bdot(rn_ref[i], s) + y0_ref[i]
        y_ref[i * CHUNK:(i + 1) * CHUNK, :] = ys[:CHUNK] + ys[CHUNK:]
        state[0] = _bdot(mc_ref[i], s) + nc_ref[i]

    r_cur, k_cur, v_cur, e_cur = r_ref[0, 0], k_ref[0, 0], v_ref[0, 0], e_ref[...]
    r = _token_shift(r_cur, pr_ref[...], mur_ref[...])
    k = _token_shift(k_cur, pk_ref[...], muk_ref[...])
    v = _token_shift(v_cur, pv_ref[...], muv_ref[...])
    e = _token_shift(e_cur, pe_ref[...], mue_ref[...])
    for cur, prev in ((r_cur, pr_ref), (k_cur, pk_ref), (v_cur, pv_ref), (e_cur, pe_ref)):
        prev[...] = cur[tb - 1:tb, :]
    recurrence_step()
    lw, a, kk, k_mod = _rwkv_transforms(r, k, v, e, w0_ref[...], a0_ref[...], kk_ref[...], ka_ref[...],
                                        ww2_ref[...], wa2_ref[...], ones)
    recurrence_step()
    bonus_new = _head_sum(r * k_mod * rk_ref[...], ones) * v
    terms = _wkv_chunk_terms(r, k_mod, v, -kk, kk * a, lw, between_stages=recurrence_step)
    while pending:
        recurrence_step()
    s_ref[...] = state[0]
    state_t = state[0].T
    for hh in range(2):
        block = slice(hh * RWKV_HEAD_DIM, (hh + 1) * RWKV_HEAD_DIM)
        st_ref[0, 0, hh] = state_t[block, block]
    o_ref[...] = _group_norm(y_ref[...], lnw_ref[...], lnb_ref[...], ones) + bonus_ref[...]

    bonus_ref[...] = bonus_new
    for i, (r_new, y0, m_c, n_c) in enumerate(terms):
        rn_ref[i] = r_new
        y0_ref[i] = y0
        mc_ref[i] = m_c
        nc_ref[i] = n_c

    _decode_step(n, pt_ref, decode_inputs, decode_outputs, scratch[N_WKV_SCRATCH:], steps_per_row=steps_per_row,
                 n_steps=n_decode_steps)


def wkv_prompt(r, k, v, ext, mu_rkv, mu_e, w0, a0, k_k, k_a, r_k, ln_w, ln_b, ww2, wa2, decode_args, *, tb, decode_layer):
    batch, _, seq, _ = r.shape
    nt = seq // tb
    n_blocks = batch * N_PAIRS * nt
    decode = _decode_plan(*decode_args, layer=decode_layer, step_of=lambda n: n)
    assert decode["n_steps"] <= n_blocks + 1

    def where(n):
        return lax.div(n, N_PAIRS * nt), lax.rem(lax.div(n, nt), N_PAIRS), lax.rem(n, nt)

    cur = lambda n: where(jnp.minimum(n, n_blocks - 1))
    prev = lambda n: where(jnp.maximum(n - 1, 0))
    rows = pl.BlockSpec((1, 1, tb, LANES), lambda n, pt: (cur(n)[0], cur(n)[1], cur(n)[2], 0))
    vec = lambda off: pl.BlockSpec((1, LANES), lambda n, pt: (0, off + cur(n)[1]))
    vec_prev = pl.BlockSpec((1, LANES), lambda n, pt: (0, prev(n)[1]))
    lora = pl.BlockSpec((RWKV_EXTRA, LANES), lambda n, pt: (0, cur(n)[1]))
    terms = pltpu.VMEM((tb // CHUNK, PAIR, PAIR), F32)
    grid_spec = pltpu.PrefetchScalarGridSpec(
        num_scalar_prefetch=1,
        grid=(n_blocks + 1,),
        in_specs=[
            rows, rows, rows,
            pl.BlockSpec((tb, RWKV_EXTRA), lambda n, pt: (cur(n)[0] * nt + cur(n)[2], 0)),
            vec(0), vec(N_PAIRS), vec(2 * N_PAIRS),
            pl.BlockSpec((1, RWKV_EXTRA), lambda n, pt: (0, 0)),
            vec(0), vec(0), vec(0), vec(0), vec(0), lora, lora, vec_prev, vec_prev,
        ] + decode["in_specs"],
        out_specs=[
            pl.BlockSpec((tb, LANES), lambda n, pt: (prev(n)[0] * nt + prev(n)[2], prev(n)[1])),
            pl.BlockSpec((1, 1, 2, RWKV_HEAD_DIM, RWKV_HEAD_DIM), lambda n, pt: (prev(n)[0], prev(n)[1], 0, 0, 0)),
        ] + decode["out_specs"],
        scratch_shapes=[
            pltpu.VMEM((1, LANES), F32), pltpu.VMEM((1, LANES), F32), pltpu.VMEM((1, LANES), F32),
            pltpu.VMEM((1, RWKV_EXTRA), F32), pltpu.VMEM((PAIR, PAIR), F32), pltpu.VMEM((tb, LANES), F32),
            terms, terms, terms, terms, pltpu.VMEM((tb, LANES), F32),
        ] + decode["scratch_shapes"],
    )
    return pl.pallas_call(
        functools.partial(_wkv_prompt_kernel, tb=tb, nt=nt, steps_per_row=decode["steps_per_row"],
                          n_decode_steps=decode["n_steps"]),
        grid_spec=grid_spec,
        out_shape=[
            jax.ShapeDtypeStruct((batch * seq, TOK_WIDTH), F32),
            jax.ShapeDtypeStruct((batch, N_PAIRS, 2, RWKV_HEAD_DIM, RWKV_HEAD_DIM), F32),
        ] + decode["out_shape"],
        compiler_params=_params("arbitrary"),
        name="wkv_prompt",
    )(decode_args[0], r, k, v, ext, mu_rkv, mu_rkv, mu_rkv, mu_e, w0, a0, k_k, k_a, r_k, ww2, wa2, ln_w, ln_b,
      *decode["args"])


def _wkv_step_kernel(cur_ref, prev_ref, e_ref, pe_ref, mu_ref, mue_ref, w0_ref, a0_ref, kk_ref, ka_ref, rk_ref,
                     lnw_ref, lnb_ref, ww2_ref, wa2_ref, st_ref, o_ref, stn_ref):
    hd = RWKV_HEAD_DIM
    e_cur = e_ref[0]
    e = e_cur + (pe_ref[0] - e_cur) * mue_ref[...]
    e_rows = jnp.broadcast_to(e, (SUBLANES, RWKV_EXTRA))
    e_tanh = jnp.tanh(e_rows).astype(BF16)
    e_lin = e_rows.astype(BF16)
    eye = lax.broadcasted_iota(jnp.int32, (hd, hd), 0) == lax.broadcasted_iota(jnp.int32, (hd, hd), 1)

    def mixed(i, h):
        cur = cur_ref[0, i, h:h + 1, :]
        return cur + (prev_ref[0, i, h:h + 1, :] - cur) * mu_ref[i, h:h + 1, :]

    for h in range(RWKV_HEADS):
        row = lambda ref: ref[h:h + 1, :]
        r, k, v = mixed(0, h), mixed(1, h), mixed(2, h)
        w = -_softplus(-(row(w0_ref) + jnp.dot(e_tanh, ww2_ref[h], preferred_element_type=F32)[0:1])) - 0.5
        decay = jnp.exp(-jnp.exp(w))
        a = _sigmoid(row(a0_ref) + jnp.dot(e_lin, wa2_ref[h], preferred_element_type=F32)[0:1])
        kk = k * row(kk_ref)
        kk = kk / jnp.maximum(jnp.sqrt(jnp.sum(kk * kk, axis=-1, keepdims=True)), 1e-12)
        k_mod = k * (1.0 + (a - 1.0) * row(ka_ref))
        s = st_ref[0, h]
        sa = jnp.sum(s * (-kk), axis=-1, keepdims=True)
        v_col = jnp.sum(jnp.where(eye, jnp.broadcast_to(v, (hd, hd)), 0.0), axis=-1, keepdims=True)
        s_new = s * decay + sa * (kk * a) + v_col * k_mod
        stn_ref[0, h] = s_new
        y_col = jnp.sum(s_new * r, axis=-1, keepdims=True)
        y = jnp.sum(jnp.where(eye, jnp.broadcast_to(y_col, (hd, hd)), 0.0), axis=0, keepdims=True)
        mean = jnp.mean(y, axis=-1, keepdims=True)
        d = y - mean
        var = jnp.mean(d * d, axis=-1, keepdims=True)
        yn = d * lax.rsqrt(var + GN_EPS) * row(lnw_ref) + row(lnb_ref)
        o_ref[0, h:h + 1, :] = yn + jnp.sum(r * k_mod * row(rk_ref), axis=-1, keepdims=True) * v


def wkv_step(cur, prev, e_cur, e_prev, mu, mu_e, w0, a0, k_k, k_a, r_k, ln_w, ln_b, ww2h, wa2h, state):
    batch = cur.shape[0]
    hshape = (RWKV_HEADS, RWKV_HEAD_DIM)
    per_row = lambda shape: pl.BlockSpec((1,) + shape, lambda b: (b,) + (0,) * len(shape))
    const = lambda shape: pl.BlockSpec(shape, lambda b: (0,) * len(shape))
    return pl.pallas_call(
        _wkv_step_kernel,
        grid=(batch,),
        in_specs=[per_row((3,) + hshape), per_row((3,) + hshape), per_row((1, RWKV_EXTRA)), per_row((1, RWKV_EXTRA)),
                  const((3,) + hshape), const((1, RWKV_EXTRA))] + [const(hshape)] * 7
                 + [const((RWKV_HEADS, RWKV_EXTRA, RWKV_HEAD_DIM))] * 2 + [per_row(hshape + (RWKV_HEAD_DIM,))],
        out_specs=[per_row(hshape), per_row(hshape + (RWKV_HEAD_DIM,))],
        out_shape=[jax.ShapeDtypeStruct((batch,) + hshape, F32),
                   jax.ShapeDtypeStruct((batch,) + hshape + (RWKV_HEAD_DIM,), F32)],
        compiler_params=_params("parallel"),
        name="wkv_step",
    )(cur, prev, e_cur, e_prev, mu, mu_e, w0, a0, k_k, k_a, r_k, ln_w, ln_b, ww2h, wa2h, state)


def _relayout_in_proj(w_in, extra_cols, extra_width):
    w_t = jnp.transpose(w_in).astype(BF16)
    gm_start = 3 * TOK_WIDTH + extra_cols
    return w_t, w_t[gm_start:], jnp.pad(w_t[3 * TOK_WIDTH:gm_start], ((0, extra_width - extra_cols), (0, 0)))


def _pad_lanes(v, width):
    return jnp.pad(v, (0, width - v.shape[0])).reshape(1, width)


def _memory_kv(mem, g_mem, w_mem):
    kv = norm_proj(mem, g_mem, w_mem, tm=mem.shape[0])
    return kv[:, :MEM_WIDTH], kv[:, MEM_WIDTH:]


def kernel(x_prompt, x_sample, cache_fox_k, cache_fox_v, cache_fox_logf, cache_mem_k, cache_mem_v, state_rwkv_wkv, state_rwkv_shift, page_table, mem_prompt, g_pre, g_post, g_mem, w_mem, w_o, fox_w_in, fox_b_f, rwkv_w_in, rwkv_mu, rwkv_w0, rwkv_w_w2, rwkv_a0, rwkv_w_a2, rwkv_k_k, rwkv_k_a, rwkv_r_k, rwkv_ln_w, rwkv_ln_b):
    bp, seq, _ = x_prompt.shape
    bs = x_sample.shape[0]
    xp = x_prompt.reshape(bp * seq, D_MODEL)
    xs = x_sample.reshape(bs, D_MODEL)
    mem = mem_prompt.reshape(bp * N_MEM, D_MODEL)
    tm_p = 1024
    mem_shape = (bp, N_MEM, MEM_HEADS, MEM_HEAD_DIM)

    w_fox = _relayout_in_proj(fox_w_in[0], FOX_HEADS, FOX_EXTRA)
    w_o = w_o.astype(BF16)
    b_pad = _pad_lanes(fox_b_f[0], LANES)
    mk0, mv0 = _memory_kv(mem, g_mem[0], w_mem[0])

    gm_p, q_p, k_p, v_p, ext_p = norm_proj(xp, g_pre[0], *w_fox, tm=tm_p, heads=(bp, seq))
    logf_p, ccol, crow = fox_gate(ext_p, b_pad, batch=bp, seq=seq, tb=256)
    o_tok = fox_flash(q_p, k_p, v_p, ccol, crow, tq=512)
    o_mem = mem_attn(gm_p.reshape(bp, seq, GM_WIDTH), mk0, mv0, tq=1024, q_block=D_MODEL // MEM_WIDTH)
    xp = out_proj(o_tok, o_mem.reshape(bp * seq, MEM_WIDTH), gm_p, xp, w_o, g_post[0], layer=0, tm=256)

    main_s0, ext_s0 = norm_proj(xs, g_pre[0], *w_fox, tm=bs)
    q_s, k_s, v_s = (main_s0[:, off:off + TOK_WIDTH] for off in (TOKA_OFF, TOKB_OFF, TOKC_OFF))
    logf_pool_t = jnp.pad(jnp.transpose(cache_fox_logf[0], (0, 2, 1)), ((0, 0), (0, HEAD_PAD - FOX_HEADS), (0, 0)))
    k_pool, v_pool = (jnp.transpose(c, (0, 1, 3, 2, 4)) for c in (cache_fox_k, cache_fox_v))
    decode_args = (page_table, q_s, k_s, v_s, ext_s0, b_pad, k_pool, v_pool, logf_pool_t)

    head_shape = (FOX_HEADS, FOX_HEAD_DIM)
    fox_k_p = jnp.transpose(k_p, (0, 2, 1, 3))[None]
    fox_v_p = jnp.transpose(v_p, (0, 2, 1, 3))[None]
    fox_f_p = logf_p[:, :FOX_HEADS].reshape(1, bp, seq, FOX_HEADS)
    fox_k_s = k_s.reshape((1, bs, 1) + head_shape)
    fox_v_s = v_s.reshape((1, bs, 1) + head_shape)

    w_rwkv = _relayout_in_proj(rwkv_w_in[0], 2 * LORA, RWKV_EXTRA)
    mk1, mv1 = _memory_kv(mem, g_mem[1], w_mem[1])
    mu_rkv = rwkv_mu[0][:3 * TOK_WIDTH].reshape(1, 3 * TOK_WIDTH)
    mu_e = _pad_lanes(rwkv_mu[0][3 * TOK_WIDTH:], RWKV_EXTRA)
    ww2 = jnp.pad(rwkv_w_w2[0], ((0, RWKV_EXTRA - LORA), (0, 0))).astype(BF16)
    wa2 = jnp.pad(rwkv_w_a2[0], ((LORA, RWKV_EXTRA - 2 * LORA), (0, 0))).astype(BF16)
    vecs = [p[0] for p in (rwkv_w0, rwkv_a0, rwkv_k_k, rwkv_k_a, rwkv_r_k, rwkv_ln_w, rwkv_ln_b)]

    gm_p, r_p, k_p, v_p, ext_p = norm_proj(xp, g_pre[1], *w_rwkv, tm=tm_p, heads=(bp, seq))
    o_tok, st_pairs, o_tok_s, logf_s = wkv_prompt(r_p, k_p, v_p, ext_p, mu_rkv, mu_e,
                                                  *[v.reshape(1, TOK_WIDTH) for v in vecs], ww2, wa2, decode_args,
                                                  tb=512, decode_layer=0)
    o_mem = mem_attn(gm_p.reshape(bp, seq, GM_WIDTH), mk1, mv1, tq=1024, q_block=D_MODEL // MEM_WIDTH)
    xp = out_proj(o_tok, o_mem.reshape(bp * seq, MEM_WIDTH), gm_p, xp, w_o, g_post[1], layer=1, tm=256)

    fox_f_s = logf_s[:, :FOX_HEADS].reshape(1, bs, 1, FOX_HEADS)
    o_mem_s = mem_attn(main_s0.reshape(bs, 1, MAIN_WIDTH), cache_mem_k[0].reshape(bs * N_MEM, MEM_WIDTH),
                       cache_mem_v[0].reshape(bs * N_MEM, MEM_WIDTH), tq=1, q_block=QMEM_OFF // MEM_WIDTH)
    xs = out_proj(o_tok_s, o_mem_s.reshape(bs, MEM_WIDTH), main_s0, xs, w_o, g_post[0], layer=0, tm=bs)

    main_s, ext_s = norm_proj(xs, g_pre[1], *w_rwkv, tm=bs)
    hshape = (RWKV_HEADS, RWKV_HEAD_DIM)
    shift_prev = state_rwkv_shift[0][:, 0, :]
    cur = main_s[:, TOKA_OFF:QMEM_OFF].reshape((bs, 3) + hshape)
    prev = shift_prev[:, :3 * TOK_WIDTH].reshape((bs, 3) + hshape)
    e_prev = jnp.pad(shift_prev[:, 3 * TOK_WIDTH:], ((0, 0), (0, RWKV_EXTRA - 2 * LORA))).reshape(bs, 1, RWKV_EXTRA)
    per_head = lambda w: jnp.transpose(w.reshape((RWKV_EXTRA,) + hshape), (1, 0, 2))
    o_tok_s, wkv_s = wkv_step(cur, prev, ext_s.reshape(bs, 1, RWKV_EXTRA), e_prev, mu_rkv.reshape((3,) + hshape), mu_e,
                              *[v.reshape(hshape) for v in vecs], per_head(ww2), per_head(wa2), state_rwkv_wkv[0])
    o_mem_s = mem_attn(main_s.reshape(bs, 1, MAIN_WIDTH), cache_mem_k[1].reshape(bs * N_MEM, MEM_WIDTH),
                       cache_mem_v[1].reshape(bs * N_MEM, MEM_WIDTH), tq=1, q_block=QMEM_OFF // MEM_WIDTH)
    xs = out_proj(o_tok_s.reshape(bs, TOK_WIDTH), o_mem_s.reshape(bs, MEM_WIDTH), main_s, xs, w_o, g_post[1], layer=1, tm=bs)

    wkv_p = st_pairs.reshape((bp,) + hshape + (RWKV_HEAD_DIM,))
    last_p = jnp.concatenate([a[:, :, seq - 1, :].reshape(bp, TOK_WIDTH) for a in (r_p, k_p, v_p)], axis=-1)
    last_e = ext_p.reshape(bp, seq, RWKV_EXTRA)[:, seq - 1, :2 * LORA]
    shift_p = jnp.concatenate([last_p, last_e], axis=-1).reshape(1, bp, 1, 3 * TOK_WIDTH + 2 * LORA)
    shift_s = jnp.concatenate([main_s[:, TOKA_OFF:QMEM_OFF], ext_s[:, :2 * LORA]], axis=-1).reshape(1, bs, 1, 3 * TOK_WIDTH + 2 * LORA)

    return (xp.reshape(bp, seq, D_MODEL), xs.reshape(bs, 1, D_MODEL),
            jnp.stack([mk0.reshape(mem_shape), mk1.reshape(mem_shape)]),
            jnp.stack([mv0.reshape(mem_shape), mv1.reshape(mem_shape)]),
            fox_k_p, fox_v_p, fox_f_p, fox_k_s, fox_v_s, fox_f_s,
            wkv_p[None], shift_p, wkv_s[None], shift_s)
```

```python
import functools
import math

import jax
import jax.numpy as jnp
from jax import lax
from jax.experimental import pallas as pl
from jax.experimental.pallas import tpu as pltpu

F32 = jnp.float32
BF16 = jnp.bfloat16

D_MODEL = 2048
N_MEM = 256
MEM_HEADS = 4
MEM_HEAD_DIM = 128
MEM_WIDTH = MEM_HEADS * MEM_HEAD_DIM
TOK_WIDTH = D_MODEL - MEM_WIDTH
FOX_HEAD_DIM = 128
FOX_HEADS = TOK_WIDTH // FOX_HEAD_DIM
RWKV_HEAD_DIM = 64
RWKV_HEADS = TOK_WIDTH // RWKV_HEAD_DIM
LORA = 96
PAGE_SIZE = 128
RMS_EPS = 1e-6
GN_EPS = 64e-5

LANES = 128
SUBLANES = 8
V7X_VMEM_BYTES = 64 * 1024 * 1024
VMEM_LIMIT_BYTES = V7X_VMEM_BYTES - 8 * 1024 * 1024

GATE_OFF = 0
TOKA_OFF = D_MODEL
TOKB_OFF = TOKA_OFF + TOK_WIDTH
TOKC_OFF = TOKB_OFF + TOK_WIDTH
QMEM_OFF = TOKC_OFF + TOK_WIDTH
MAIN_WIDTH = QMEM_OFF + MEM_WIDTH
FOX_EXTRA = LANES
RWKV_EXTRA = 2 * LANES

HEAD_PAD = 16
CHUNK = 64

PROJ_ROWS = 1024
GATE_ROWS = 256
FLASH_ROWS = 512
MEM_ATTN_ROWS = 1024
OUT_PROJ_ROWS = 256
WKV_ROWS = 8 * CHUNK


def _params(*sem):
    return pltpu.CompilerParams(dimension_semantics=sem, vmem_limit_bytes=VMEM_LIMIT_BYTES)


def _rms_scale(x):
    return lax.rsqrt(jnp.mean(x * x, axis=-1, keepdims=True) + RMS_EPS)


def _sigmoid(x):
    return 1.0 / (1.0 + jnp.exp(-x))


def _softplus(x):
    return jnp.maximum(x, 0.0) + jnp.log(1.0 + jnp.exp(-jnp.abs(x)))


def _split(x, pieces):
    out = []
    for _ in range(pieces - 1):
        p = x.astype(BF16)
        out.append(p)
        x = x - p.astype(F32)
    return out + [x.astype(BF16)]


def _dot_exact_rhs(x, rhs, pieces=3):
    rows = x.shape[0]
    parts = jnp.dot(jnp.concatenate(_split(x, pieces), axis=0), rhs.astype(BF16), preferred_element_type=F32)
    return sum(parts[i * rows:(i + 1) * rows] for i in range(pieces))


def _dot_exact_lhs(lhs, x, pieces=3):
    cols = x.shape[1]
    parts = jnp.dot(lhs.astype(BF16), jnp.concatenate(_split(x, pieces), axis=1), preferred_element_type=F32)
    return sum(parts[:, i * cols:(i + 1) * cols] for i in range(pieces))


TN = 4 * LANES
GATE_BLOCKS = D_MODEL // TN
TOK_BLOCKS = TOK_WIDTH // TN
QMEM_BLOCK = GATE_BLOCKS + 3 * TOK_BLOCKS
GM_WIDTH = D_MODEL + MEM_WIDTH
N_GROUPS128 = TOK_WIDTH // LANES
assert MAIN_WIDTH == (QMEM_BLOCK + 1) * TN and MEM_WIDTH == TN


def _nt_dot(a, b):
    return lax.dot_general(a, b, (((1,), (1,)), ((), ())), preferred_element_type=F32)


def _norm_proj_kernel(x_ref, g_ref, wt_ref, *rest, has_gm, has_extra, head_mode):
    rest = list(rest)
    wg_ref = rest.pop(0) if has_gm else None
    we_ref = rest.pop(0) if has_extra else None
    h_ref = rest.pop()
    oe_ref = rest.pop() if has_extra else None
    j = pl.program_id(1)

    @pl.when(j == 0)
    def _():
        x = x_ref[...]
        h = (x * _rms_scale(x) * g_ref[...]).astype(BF16)
        h_ref[...] = h
        if has_extra:
            oe_ref[...] = _nt_dot(h, we_ref[...])

    if not has_gm:
        rest[0][...] = jnp.dot(h_ref[...], wt_ref[...].astype(BF16), preferred_element_type=F32)
        return
    in_gm = (j < GATE_BLOCKS) | (j == QMEM_BLOCK)

    @pl.when(in_gm)
    def _():
        rest[0][...] = _nt_dot(h_ref[...], wg_ref[...])

    if not head_mode:
        @pl.when(jnp.logical_not(in_gm))
        def _():
            rest[0][...] = _nt_dot(h_ref[...], wt_ref[...])
        return

    for gi, ref in enumerate(rest[1:]):
        first = GATE_BLOCKS + gi * TOK_BLOCKS

        @pl.when((j >= first) & (j < first + TOK_BLOCKS))
        def _(ref=ref):
            res = _nt_dot(h_ref[...], wt_ref[...])
            for hh in range(TN // LANES):
                ref[0, hh] = res[:, hh * LANES:(hh + 1) * LANES]


def norm_proj(x, g, w, w_gm_t=None, w_extra_t=None, *, tm, heads=None):
    m, d = x.shape
    has_gm = w_gm_t is not None
    has_extra = w_extra_t is not None
    n = MAIN_WIDTH if has_gm else w.shape[1]
    assert m % tm == 0 and n % TN == 0 and (has_gm or heads is None)
    gm_col = lambda j: jnp.where(j == QMEM_BLOCK, GATE_BLOCKS, jnp.minimum(j, GATE_BLOCKS - 1))
    gm_row = lambda j: jnp.where(j == QMEM_BLOCK, 0, jnp.minimum(j, GATE_BLOCKS - 1) + MEM_WIDTH // TN)
    in_specs = [pl.BlockSpec((tm, d), lambda i, j: (i, 0)), pl.BlockSpec((1, d), lambda i, j: (0, 0))]
    args = [x, g.reshape(1, d), w]
    if has_gm:
        in_specs.append(pl.BlockSpec((TN, d), lambda i, j: (jnp.clip(j - GATE_BLOCKS, 0, 3 * TOK_BLOCKS - 1), 0)))
        in_specs.append(pl.BlockSpec((TN, d), lambda i, j: (gm_row(j), 0)))
        args.append(w_gm_t)
    else:
        in_specs.append(pl.BlockSpec((d, TN), lambda i, j: (0, j)))
    if has_extra:
        in_specs.append(pl.BlockSpec(w_extra_t.shape, lambda i, j: (0, 0)))
        args.append(w_extra_t)
    if heads is None:
        out_specs = [pl.BlockSpec((tm, TN), lambda i, j: (i, j))]
        out_shape = [jax.ShapeDtypeStruct((m, n), F32)]
    else:
        batch, seq = heads
        assert seq % tm == 0
        per_seq = seq // tm
        out_specs = [pl.BlockSpec((tm, TN), lambda i, j: (i, gm_col(j)))]
        out_shape = [jax.ShapeDtypeStruct((m, GM_WIDTH), F32)]
        for gi in range(3):
            first = GATE_BLOCKS + gi * TOK_BLOCKS
            out_specs.append(pl.BlockSpec(
                (1, TN // LANES, tm, LANES),
                lambda i, j, first=first: (i // per_seq, jnp.clip(j - first, 0, TOK_BLOCKS - 1), i % per_seq, 0)))
            out_shape.append(jax.ShapeDtypeStruct((batch, N_GROUPS128, seq, LANES), F32))
    if has_extra:
        e = w_extra_t.shape[0]
        out_specs.append(pl.BlockSpec((tm, e), lambda i, j: (i, 0)))
        out_shape.append(jax.ShapeDtypeStruct((m, e), F32))
    outs = pl.pallas_call(
        functools.partial(_norm_proj_kernel, has_gm=has_gm, has_extra=has_extra, head_mode=heads is not None),
        grid=(m // tm, n // TN),
        in_specs=in_specs,
        out_specs=out_specs,
        out_shape=out_shape,
        scratch_shapes=[pltpu.VMEM((tm, d), BF16)],
        compiler_params=_params("parallel", "arbitrary"),
        name="norm_proj",
    )(*args)
    return outs if len(outs) > 1 else outs[0]


def _fox_gate_kernel(f_ref, b_ref, logf_ref, ccol_ref, crow_ref, carry_ref, *, tb):
    @pl.when(pl.program_id(1) == 0)
    def _():
        carry_ref[...] = jnp.zeros_like(carry_ref)

    logf = -_softplus(-(f_ref[...] + b_ref[...]))
    logf_ref[...] = logf
    row = lax.broadcasted_iota(jnp.int32, (tb, tb), 0)
    col = lax.broadcasted_iota(jnp.int32, (tb, tb), 1)
    c = _dot_exact_lhs(jnp.where(col <= row, 1.0, 0.0), logf) + carry_ref[...]
    ccol_ref[...] = c
    crow_ref[0] = c.T[:HEAD_PAD]
    carry_ref[...] = c[tb - 1:tb, :]


def fox_gate(f_ext, b_pad, *, batch, seq, tb):
    nt = seq // tb
    row_spec = pl.BlockSpec((tb, LANES), lambda b, t: (b * nt + t, 0))
    return pl.pallas_call(
        functools.partial(_fox_gate_kernel, tb=tb),
        grid=(batch, nt),
        in_specs=[row_spec, pl.BlockSpec((1, LANES), lambda b, t: (0, 0))],
        out_specs=[row_spec, row_spec, pl.BlockSpec((1, HEAD_PAD, tb), lambda b, t: (b, 0, t))],
        out_shape=[
            jax.ShapeDtypeStruct((batch * seq, LANES), F32),
            jax.ShapeDtypeStruct((batch * seq, LANES), F32),
            jax.ShapeDtypeStruct((batch, HEAD_PAD, seq), F32),
        ],
        scratch_shapes=[pltpu.VMEM((1, LANES), F32)],
        compiler_params=_params("parallel", "arbitrary"),
        name="fox_gate",
    )(f_ext, b_pad)


_MASKED = -1e30
LOG2E = 1.4426950408889634


def _flash_tile(h, qi, q_ref, k_ref, v_ref, ccol_ref, crow_ref, o_ref, kb_ref, vb_ref, *, tq):
    @pl.when(qi == 0)
    def _():
        kb_ref[...] = k_ref[0, 0].astype(BF16)
        vb_ref[...] = v_ref[0, 0].astype(BF16)

    q = (q_ref[0, 0] * (FOX_HEAD_DIM ** -0.5 * LOG2E)).astype(BF16)
    lane = lax.broadcasted_iota(jnp.int32, (tq, LANES), 1)
    cq = jnp.sum(jnp.where(lane == h, ccol_ref[...], 0.0), axis=-1, keepdims=True) * LOG2E

    def step(c, carry, diagonal):
        m, l, acc = carry
        start = pl.multiple_of(c * tq, tq)
        kc = kb_ref[pl.ds(start, tq), :]
        vc = vb_ref[pl.ds(start, tq), :]
        ck = crow_ref[0, 0, :, pl.ds(start, tq)] * LOG2E
        s = lax.dot_general(q, kc, (((1,), (1,)), ((), ())), preferred_element_type=F32) - ck
        if diagonal:
            row = lax.broadcasted_iota(jnp.int32, (tq, tq), 0)
            col = lax.broadcasted_iota(jnp.int32, (tq, tq), 1)
            s = jnp.where(col <= row, s, _MASKED)
        m_new = jnp.maximum(m, jnp.max(s, axis=-1, keepdims=True) + cq)
        p = jnp.exp2(s - (m_new - cq))
        alpha = jnp.exp2(m - m_new)
        l = alpha * l + jnp.sum(p, axis=-1, keepdims=True)
        acc = alpha * acc + jnp.dot(p.astype(BF16), vc, preferred_element_type=F32)
        return m_new, l, acc

    init = (jnp.full((tq, 1), _MASKED, F32), jnp.zeros((tq, 1), F32), jnp.zeros((tq, FOX_HEAD_DIM), F32))
    carry = lax.fori_loop(0, qi, lambda c, cr: step(c, cr, False), init)
    _, l, acc = step(qi, carry, True)
    o_ref[...] = acc / l


PAGES_PER_STEP = 8


def _head_mask():
    row = lax.broadcasted_iota(jnp.int32, (HEAD_PAD, TOK_WIDTH), 0)
    col = lax.broadcasted_iota(jnp.int32, (HEAD_PAD, TOK_WIDTH), 1)
    lo = row * FOX_HEAD_DIM
    return (col >= lo) & (col < lo + FOX_HEAD_DIM)


def _decode_pages(b, j, last, pt_ref, q_ref, kn_ref, vn_ref, f_ref, bf_ref, lf_ref, k_refs, v_refs,
                  o_ref, lfn_ref, qbd_ref, m_ref, l_ref, acc_ref, carry_ref):
    pps = PAGES_PER_STEP
    n_pages = pt_ref.shape[1]
    scale = FOX_HEAD_DIM ** -0.5
    hmask = _head_mask()

    @pl.when(j == 0)
    def _():
        q = q_ref[pl.ds(b, 1), :]
        q_bd = jnp.where(hmask, jnp.broadcast_to(q, (HEAD_PAD, TOK_WIDTH)), 0.0)
        qbd_ref[...] = q_bd.astype(BF16)
        logf_new = -_softplus(-(f_ref[pl.ds(b, 1), :] + bf_ref[...]))
        lfn_ref[pl.ds(b, 1), :] = logf_new
        eye = lax.broadcasted_iota(jnp.int32, (HEAD_PAD, LANES), 0) == lax.broadcasted_iota(jnp.int32, (HEAD_PAD, LANES), 1)
        carry_ref[...] = jnp.sum(jnp.where(eye, jnp.broadcast_to(logf_new, (HEAD_PAD, LANES)), 0.0), axis=-1, keepdims=True)
        m_ref[...] = jnp.sum(q_bd * kn_ref[pl.ds(b, 1), :], axis=-1, keepdims=True) * scale
        l_ref[...] = jnp.ones_like(l_ref)
        acc_ref[...] = jnp.broadcast_to(vn_ref[pl.ds(b, 1), :], (HEAD_PAD, TOK_WIDTH))

    def page_rows(ref):
        return jnp.concatenate([ref[0, 0, h].astype(BF16) for h in range(FOX_HEADS)], axis=1)

    lfs = [lf_ref[pt_ref[b, n_pages - 1 - j * pps - i]] for i in range(pps)]
    src = lax.broadcasted_iota(jnp.int32, (PAGE_SIZE, PAGE_SIZE), 0)
    dst = lax.broadcasted_iota(jnp.int32, (PAGE_SIZE, PAGE_SIZE), 1)
    inside = _dot_exact_rhs(jnp.concatenate(lfs, axis=0), jnp.where(src > dst, 1.0, 0.0))
    carries = [carry_ref[...]]
    for lf in lfs:
        carries.append(carries[-1] + jnp.sum(lf, axis=-1, keepdims=True))
    carry_ref[...] = carries[-1]

    q_bd = qbd_ref[...]
    scores = []
    for i in range(pps):
        s = lax.dot_general(q_bd, page_rows(k_refs[i]), (((1,), (1,)), ((), ())), preferred_element_type=F32)
        scores.append(s * scale + (inside[i * HEAD_PAD:(i + 1) * HEAD_PAD] + carries[i]))
    m_old = m_ref[...]
    m_new = m_old
    for s in scores:
        m_new = jnp.maximum(m_new, jnp.max(s, axis=-1, keepdims=True))
    alpha = jnp.exp(m_old - m_new)
    l_new = alpha * l_ref[...]
    acc = alpha * acc_ref[...]
    for i, s in enumerate(scores):
        p = jnp.exp(s - m_new)
        l_new = l_new + jnp.sum(p, axis=-1, keepdims=True)
        acc = acc + jnp.dot(p.astype(BF16), page_rows(v_refs[i]), preferred_element_type=F32)
    m_ref[...] = m_new
    l_ref[...] = l_new
    acc_ref[...] = acc

    @pl.when(last)
    def _():
        out = jnp.where(hmask, acc / l_new, 0.0)
        o_ref[pl.ds(b, 1), :] = jnp.sum(out, axis=0, keepdims=True)


def _decode_plan(page_table, q_s, k_new, v_new, f_ext, b_pad, k_pool, v_pool, logf_pool_t, *, layer, step_of):
    rows, n_pages = page_table.shape
    pps = PAGES_PER_STEP
    assert n_pages % pps == 0
    steps_per_row = n_pages // pps
    n_steps = rows * steps_per_row

    def page_id(i):
        def index(*idx):
            pt = idx[-1]
            d = jnp.minimum(step_of(*idx[:-1]), n_steps - 1)
            return pt[lax.div(d, steps_per_row), n_pages - 1 - lax.rem(d, steps_per_row) * pps - i]
        return index

    whole = lambda shape: pl.BlockSpec(shape, lambda *idx: (0,) * len(shape))
    page_spec = lambda i: pl.BlockSpec((1, 1, FOX_HEADS, PAGE_SIZE, FOX_HEAD_DIM),
                                       lambda *idx: (layer, page_id(i)(*idx), 0, 0, 0))
    return dict(
        in_specs=[whole((rows, TOK_WIDTH))] * 3 + [whole((rows, LANES)), whole((1, LANES)), whole(logf_pool_t.shape)]
                 + [page_spec(i) for i in range(pps)] * 2,
        args=[q_s, k_new, v_new, f_ext, b_pad, logf_pool_t] + [k_pool] * pps + [v_pool] * pps,
        out_specs=[whole((rows, TOK_WIDTH)), whole((rows, LANES))],
        out_shape=[jax.ShapeDtypeStruct((rows, TOK_WIDTH), F32), jax.ShapeDtypeStruct((rows, LANES), F32)],
        scratch_shapes=[
            pltpu.VMEM((HEAD_PAD, TOK_WIDTH), BF16),
            pltpu.VMEM((HEAD_PAD, 1), F32),
            pltpu.VMEM((HEAD_PAD, 1), F32),
            pltpu.VMEM((HEAD_PAD, TOK_WIDTH), F32),
            pltpu.VMEM((HEAD_PAD, 1), F32),
        ],
        steps_per_row=steps_per_row, n_steps=n_steps,
    )


N_DECODE_INPUTS = 6 + 2 * PAGES_PER_STEP
N_DECODE_OUTPUTS = 2
N_DECODE_SCRATCH = 5


def _decode_step(n, pt_ref, inputs, outputs, scratch, *, steps_per_row, n_steps):
    pps = PAGES_PER_STEP
    q_ref, kn_ref, vn_ref, f_ref, bf_ref, lf_ref = inputs[:6]
    k_refs, v_refs = inputs[6:6 + pps], inputs[6 + pps:]

    @pl.when(n < n_steps)
    def _():
        j = lax.rem(n, steps_per_row)
        _decode_pages(lax.div(n, steps_per_row), j, j == steps_per_row - 1, pt_ref, q_ref, kn_ref, vn_ref, f_ref,
                      bf_ref, lf_ref, k_refs, v_refs, *outputs, *scratch)


def _fox_flash_kernel(q_ref, k_ref, v_ref, ccol_ref, crow_ref, o_ref, kb_ref, vb_ref, *, tq):
    _flash_tile(pl.program_id(1), pl.program_id(2), q_ref, k_ref, v_ref, ccol_ref, crow_ref, o_ref, kb_ref, vb_ref, tq=tq)


def fox_flash(q, k, v, ccol, crow, *, tq):
    batch, _, seq, _ = q.shape
    nq = seq // tq
    crow4 = crow.reshape(batch, HEAD_PAD, 1, seq)
    kv_spec = pl.BlockSpec((1, 1, seq, FOX_HEAD_DIM), lambda b, h, i: (b, h, 0, 0))
    return pl.pallas_call(
        functools.partial(_fox_flash_kernel, tq=tq),
        grid=(batch, FOX_HEADS, nq),
        in_specs=[
            pl.BlockSpec((1, 1, tq, FOX_HEAD_DIM), lambda b, h, i: (b, h, i, 0)),
            kv_spec, kv_spec,
            pl.BlockSpec((tq, LANES), lambda b, h, i: (b * nq + i, 0)),
            pl.BlockSpec((1, 1, 1, seq), lambda b, h, i: (b, h, 0, 0)),
        ],
        out_specs=pl.BlockSpec((tq, LANES), lambda b, h, i: (b * nq + i, h)),
        out_shape=jax.ShapeDtypeStruct((batch * seq, TOK_WIDTH), F32),
        scratch_shapes=[pltpu.VMEM((seq, LANES), BF16), pltpu.VMEM((seq, LANES), BF16)],
        compiler_params=_params("parallel", "parallel", "arbitrary"),
        name="fox_flash",
    )(q, k, v, ccol, crow4)


def _mem_attn_kernel(q_ref, k_ref, v_ref, o_ref, *, tq):
    rows = max(tq, SUBLANES)
    scale = MEM_HEAD_DIM ** -0.5
    for h in range(MEM_HEADS):
        sl = slice(h * MEM_HEAD_DIM, (h + 1) * MEM_HEAD_DIM)
        qh = jnp.broadcast_to(q_ref[0, :, sl], (rows, MEM_HEAD_DIM)).astype(BF16)
        kh = k_ref[:, sl].astype(BF16)
        vh = v_ref[:, sl].astype(BF16)
        s = lax.dot_general(qh, kh, (((1,), (1,)), ((), ())), preferred_element_type=F32) * scale
        e = jnp.exp(s - jnp.max(s, axis=-1, keepdims=True))
        oh = jnp.dot(e.astype(BF16), vh, preferred_element_type=F32) / jnp.sum(e, axis=-1, keepdims=True)
        o_ref[0, :, sl] = oh[:tq]


def mem_attn(main3, mem_k, mem_v, *, tq, q_block):
    batch, seq, _ = main3.shape
    kv_spec = pl.BlockSpec((N_MEM, MEM_WIDTH), lambda b, i: (b, 0))
    return pl.pallas_call(
        functools.partial(_mem_attn_kernel, tq=tq),
        grid=(batch, seq // tq),
        in_specs=[pl.BlockSpec((1, tq, MEM_WIDTH), lambda b, i: (b, i, q_block)), kv_spec, kv_spec],
        out_specs=pl.BlockSpec((1, tq, MEM_WIDTH), lambda b, i: (b, i, 0)),
        out_shape=jax.ShapeDtypeStruct((batch, seq, MEM_WIDTH), F32),
        compiler_params=_params("parallel", "parallel"),
        name="mem_attn",
    )(main3, mem_k, mem_v)


def _out_proj_kernel(ot_ref, om_ref, gate_ref, x_ref, w_ref, gp_ref, y_ref):
    gate = gate_ref[...]
    sg = gate * _sigmoid(gate)
    a = (ot_ref[...] * sg[:, :TOK_WIDTH]).astype(BF16)
    b = (om_ref[...] * sg[:, TOK_WIDTH:]).astype(BF16)
    y = (jnp.dot(a, w_ref[0, :TOK_WIDTH], preferred_element_type=F32)
         + jnp.dot(b, w_ref[0, TOK_WIDTH:], preferred_element_type=F32))
    y_ref[...] = x_ref[...] + y * _rms_scale(y) * gp_ref[...]


def out_proj(o_tok, o_mem, main, x, w_o, g_post, *, layer, tm):
    m = x.shape[0]
    row = lambda width: pl.BlockSpec((tm, width), lambda i: (i, 0))
    return pl.pallas_call(
        _out_proj_kernel,
        grid=(m // tm,),
        in_specs=[row(TOK_WIDTH), row(MEM_WIDTH), row(D_MODEL), row(D_MODEL),
                  pl.BlockSpec((1, D_MODEL, D_MODEL), lambda i: (layer, 0, 0)),
                  pl.BlockSpec((1, D_MODEL), lambda i: (0, 0))],
        out_specs=row(D_MODEL),
        out_shape=jax.ShapeDtypeStruct((m, D_MODEL), F32),
        compiler_params=_params("parallel"),
        name="out_proj",
    )(o_tok, o_mem, main, x, w_o, g_post.reshape(1, D_MODEL))


PAIR = 2 * RWKV_HEAD_DIM
assert PAIR == LANES
N_PAIRS = RWKV_HEADS // 2


def _pair_ones():
    row = lax.broadcasted_iota(jnp.int32, (PAIR, PAIR), 0) // RWKV_HEAD_DIM
    col = lax.broadcasted_iota(jnp.int32, (PAIR, PAIR), 1) // RWKV_HEAD_DIM
    return jnp.where(row == col, 1.0, 0.0).astype(F32)


def _head_sum(x, ones):
    return _dot_exact_rhs(x, ones, pieces=2)


def _stack(x):
    lane = lax.broadcasted_iota(jnp.int32, x.shape, 1)
    first = lane < RWKV_HEAD_DIM
    return jnp.concatenate([jnp.where(first, x, 0.0), jnp.where(first, 0.0, x)], axis=0)


def _bdot(a, b):
    return jnp.dot(a.astype(BF16), b.astype(BF16), preferred_element_type=F32)


def _wkv_chunk_terms(r, k, v, a_vec, b_vec, lw, between_stages=lambda: None):
    c = CHUNK
    chunks = [slice(c0, c0 + c) for c0 in range(0, r.shape[0], c)]
    row = lax.broadcasted_iota(jnp.int32, (c, c), 0)
    col = lax.broadcasted_iota(jnp.int32, (c, c), 1)
    tril = jnp.where(col <= row, 1.0, 0.0)
    gi = lax.broadcasted_iota(jnp.int32, (4 * c, 4 * c), 0)
    gj = lax.broadcasted_iota(jnp.int32, (4 * c, 4 * c), 1)
    keep = (gj & (c - 1)) < (gi & (c - 1)) + gi // (2 * c)
    eye = jnp.where(lax.broadcasted_iota(jnp.int32, (2 * c, 2 * c), 0) == lax.broadcasted_iota(jnp.int32, (2 * c, 2 * c), 1),
                    1.0, 0.0).astype(F32)

    cum = [_dot_exact_lhs(tril, lw[s]) for s in chunks]
    between_stages()
    last = [x[c - 1:c, :] for x in cum]
    a_t = [_stack(a_vec[s] * jnp.exp(cu - lw[s])) for s, cu in zip(chunks, cum)]
    r_t = [_stack(r[s] * jnp.exp(cu)) for s, cu in zip(chunks, cum)]
    v_s = [_stack(v[s]) for s in chunks]
    xs = [jnp.concatenate([a, rr], axis=0).astype(BF16) for a, rr in zip(a_t, r_t)]
    ys = [jnp.concatenate([_stack(b_vec[s] * jnp.exp(-cu)), _stack(k[s] * jnp.exp(-cu))], axis=0).astype(BF16)
          for s, cu in zip(chunks, cum)]
    g = [jnp.where(keep, lax.dot_general(x, y, (((1,), (1,)), ((), ())), preferred_element_type=F32), 0.0)
         for x, y in zip(xs, ys)]
    between_stages()
    l_ab = [x[:2 * c, :2 * c] for x in g]
    l_ak = [x[:2 * c, 2 * c:] for x in g]
    l_rb = [x[2 * c:, :2 * c] for x in g]
    l_rk = [x[2 * c:, 2 * c:] for x in g]
    t_inv = [eye + x for x in l_ab]
    l_pow = l_ab
    for _ in range(int(math.log2(c)) - 1):
        l_pow = [_bdot(x, x) for x in l_pow]
        between_stages()
        t_inv = [t + _bdot(p, t) for p, t in zip(l_pow, t_inv)]
        between_stages()
    akv = [_bdot(x, y) for x, y in zip(l_ak, v_s)]
    ta = [_bdot(t, jnp.concatenate([a, u], axis=1)) for t, a, u in zip(t_inv, a_t, akv)]
    between_stages()
    rbx = [_bdot(x, y) for x, y in zip(l_rb, ta)]
    rkv = [_bdot(x, y) for x, y in zip(l_rk, v_s)]
    between_stages()
    r_new = [x + y[:, :PAIR] for x, y in zip(r_t, rbx)]
    y0 = [x[:, PAIR:] + y for x, y in zip(rbx, rkv)]
    e_rem = [jnp.exp(la - cu) for la, cu in zip(last, cum)]
    mn = [_bdot(_stack(b_vec[s] * e).T, t) for s, e, t in zip(chunks, e_rem, ta)]
    kv = [_bdot(_stack(k[s] * e).T, vv) for s, e, vv in zip(chunks, e_rem, v_s)]
    m_c = [x[:, :PAIR] + eye * jnp.exp(la) for x, la in zip(mn, last)]
    n_c = [x[:, PAIR:] + y for x, y in zip(mn, kv)]
    return list(zip(r_new, y0, m_c, n_c))


def _token_shift(cur, prev_row, mu):
    rolled = pltpu.roll(cur, 1, 0)
    first = lax.broadcasted_iota(jnp.int32, cur.shape, 0) == 0
    prev = jnp.where(first, prev_row, rolled)
    return cur + (prev - cur) * mu


def _rwkv_transforms(r, k, v, e, w0, a0, k_k, k_a, ww2, wa2, ones):
    w = -_softplus(-(w0 + jnp.dot(jnp.tanh(e).astype(BF16), ww2, preferred_element_type=F32))) - 0.5
    lw = -jnp.exp(w)
    a = _sigmoid(a0 + jnp.dot(e.astype(BF16), wa2, preferred_element_type=F32))
    kk = k * k_k
    kk = kk / jnp.maximum(jnp.sqrt(_head_sum(kk * kk, ones)), 1e-12)
    k_mod = k * (1.0 + (a - 1.0) * k_a)
    return lw, a, kk, k_mod


def _group_norm(y, ln_w, ln_b, ones):
    mean = _head_sum(y, ones) * (1.0 / RWKV_HEAD_DIM)
    d = y - mean
    var = _head_sum(d * d, ones) * (1.0 / RWKV_HEAD_DIM)
    return d * lax.rsqrt(var + GN_EPS) * ln_w + ln_b


N_WKV_INPUTS = 17
N_WKV_SCRATCH = 11


def _wkv_prompt_kernel(pt_ref, *refs, tb, nt, steps_per_row, n_decode_steps):
    refs = list(refs)
    (r_ref, k_ref, v_ref, e_ref, mur_ref, muk_ref, muv_ref, mue_ref, w0_ref, a0_ref,
     kk_ref, ka_ref, rk_ref, ww2_ref, wa2_ref, lnw_ref, lnb_ref) = refs[:N_WKV_INPUTS]
    decode_inputs = refs[N_WKV_INPUTS:N_WKV_INPUTS + N_DECODE_INPUTS]
    o_ref, st_ref, *decode_outputs = refs[N_WKV_INPUTS + N_DECODE_INPUTS:N_WKV_INPUTS + N_DECODE_INPUTS + 2 + N_DECODE_OUTPUTS]
    scratch = refs[N_WKV_INPUTS + N_DECODE_INPUTS + 2 + N_DECODE_OUTPUTS:]
    (pr_ref, pk_ref, pv_ref, pe_ref, s_ref, y_ref, rn_ref, y0_ref, mc_ref, nc_ref, bonus_ref) = scratch[:N_WKV_SCRATCH]
    n = pl.program_id(0)
    nc = tb // CHUNK
    ones = _pair_ones()

    @pl.when(n == 0)
    def _():
        for ref in (rn_ref, y0_ref, mc_ref, nc_ref, bonus_ref, s_ref):
            ref[...] = jnp.zeros_like(ref)

    @pl.when(lax.rem(n, nt) == 0)
    def _():
        for ref in (pr_ref, pk_ref, pv_ref, pe_ref):
            ref[...] = jnp.zeros_like(ref)

    prev_starts_sequence = lax.rem(n - 1, nt) == 0
    state = [jnp.where(prev_starts_sequence, 0.0, s_ref[...])]
    pending = list(range(nc))

    def recurrence_step():
        if not pending:
            return
        i = pending.pop(0)
        s = state[0]
        ys = _bdot(rn_ref[i], s) + y0_ref[i]
        y_ref[i * CHUNK:(i + 1) * CHUNK, :] = ys[:CHUNK] + ys[CHUNK:]
        state[0] = _bdot(mc_ref[i], s) + nc_ref[i]

    r_cur, k_cur, v_cur, e_cur = r_ref[0, 0], k_ref[0, 0], v_ref[0, 0], e_ref[...]
    r = _token_shift(r_cur, pr_ref[...], mur_ref[...])
    k = _token_shift(k_cur, pk_ref[...], muk_ref[...])
    v = _token_shift(v_cur, pv_ref[...], muv_ref[...])
    e = _token_shift(e_cur, pe_ref[...], mue_ref[...])
    for cur, prev in ((r_cur, pr_ref), (k_cur, pk_ref), (v_cur, pv_ref), (e_cur, pe_ref)):
        prev[...] = cur[tb - 1:tb, :]
    recurrence_step()
    lw, a, kk, k_mod = _rwkv_transforms(r, k, v, e, w0_ref[...], a0_ref[...], kk_ref[...], ka_ref[...],
                                        ww2_ref[...], wa2_ref[...], ones)
    recurrence_step()
    bonus_new = _head_sum(r * k_mod * rk_ref[...], ones) * v
    terms = _wkv_chunk_terms(r, k_mod, v, -kk, kk * a, lw, between_stages=recurrence_step)
    while pending:
        recurrence_step()
    s_ref[...] = state[0]
    state_t = state[0].T
    for hh in range(2):
        block = slice(hh * RWKV_HEAD_DIM, (hh + 1) * RWKV_HEAD_DIM)
        st_ref[0, 0, hh] = state_t[block, block]
    o_ref[...] = _group_norm(y_ref[...], lnw_ref[...], lnb_ref[...], ones) + bonus_ref[...]

    bonus_ref[...] = bonus_new
    for i, (r_new, y0, m_c, n_c) in enumerate(terms):
        rn_ref[i] = r_new
        y0_ref[i] = y0
        mc_ref[i] = m_c
        nc_ref[i] = n_c

    _decode_step(n, pt_ref, decode_inputs, decode_outputs, scratch[N_WKV_SCRATCH:], steps_per_row=steps_per_row,
                 n_steps=n_decode_steps)


def wkv_prompt(r, k, v, ext, mu_rkv, mu_e, w0, a0, k_k, k_a, r_k, ln_w, ln_b, ww2, wa2, decode_args, *, tb, decode_layer):
    batch, _, seq, _ = r.shape
    nt = seq // tb
    n_blocks = batch * N_PAIRS * nt
    decode = _decode_plan(*decode_args, layer=decode_layer, step_of=lambda n: n)
    assert decode["n_steps"] <= n_blocks + 1

    def where(n):
        return lax.div(n, N_PAIRS * nt), lax.rem(lax.div(n, nt), N_PAIRS), lax.rem(n, nt)

    cur = lambda n: where(jnp.minimum(n, n_blocks - 1))
    prev = lambda n: where(jnp.maximum(n - 1, 0))
    rows = pl.BlockSpec((1, 1, tb, LANES), lambda n, pt: (cur(n)[0], cur(n)[1], cur(n)[2], 0))
    vec = lambda off: pl.BlockSpec((1, LANES), lambda n, pt: (0, off + cur(n)[1]))
    vec_prev = pl.BlockSpec((1, LANES), lambda n, pt: (0, prev(n)[1]))
    lora = pl.BlockSpec((RWKV_EXTRA, LANES), lambda n, pt: (0, cur(n)[1]))
    terms = pltpu.VMEM((tb // CHUNK, PAIR, PAIR), F32)
    grid_spec = pltpu.PrefetchScalarGridSpec(
        num_scalar_prefetch=1,
        grid=(n_blocks + 1,),
        in_specs=[
            rows, rows, rows,
            pl.BlockSpec((tb, RWKV_EXTRA), lambda n, pt: (cur(n)[0] * nt + cur(n)[2], 0)),
            vec(0), vec(N_PAIRS), vec(2 * N_PAIRS),
            pl.BlockSpec((1, RWKV_EXTRA), lambda n, pt: (0, 0)),
            vec(0), vec(0), vec(0), vec(0), vec(0), lora, lora, vec_prev, vec_prev,
        ] + decode["in_specs"],
        out_specs=[
            pl.BlockSpec((tb, LANES), lambda n, pt: (prev(n)[0] * nt + prev(n)[2], prev(n)[1])),
            pl.BlockSpec((1, 1, 2, RWKV_HEAD_DIM, RWKV_HEAD_DIM), lambda n, pt: (prev(n)[0], prev(n)[1], 0, 0, 0)),
        ] + decode["out_specs"],
        scratch_shapes=[
            pltpu.VMEM((1, LANES), F32), pltpu.VMEM((1, LANES), F32), pltpu.VMEM((1, LANES), F32),
            pltpu.VMEM((1, RWKV_EXTRA), F32), pltpu.VMEM((PAIR, PAIR), F32), pltpu.VMEM((tb, LANES), F32),
            terms, terms, terms, terms, pltpu.VMEM((tb, LANES), F32),
        ] + decode["scratch_shapes"],
    )
    return pl.pallas_call(
        functools.partial(_wkv_prompt_kernel, tb=tb, nt=nt, steps_per_row=decode["steps_per_row"],
                          n_decode_steps=decode["n_steps"]),
        grid_spec=grid_spec,
        out_shape=[
            jax.ShapeDtypeStruct((batch * seq, TOK_WIDTH), F32),
            jax.ShapeDtypeStruct((batch, N_PAIRS, 2, RWKV_HEAD_DIM, RWKV_HEAD_DIM), F32),
        ] + decode["out_shape"],
        compiler_params=_params("arbitrary"),
        name="wkv_prompt",
    )(decode_args[0], r, k, v, ext, mu_rkv, mu_rkv, mu_rkv, mu_e, w0, a0, k_k, k_a, r_k, ww2, wa2, ln_w, ln_b,
      *decode["args"])


def _wkv_step_kernel(cur_ref, prev_ref, e_ref, pe_ref, mu_ref, mue_ref, w0_ref, a0_ref, kk_ref, ka_ref, rk_ref,
                     lnw_ref, lnb_ref, ww2_ref, wa2_ref, st_ref, o_ref, stn_ref):
    hd = RWKV_HEAD_DIM
    e_cur = e_ref[0]
    e = e_cur + (pe_ref[0] - e_cur) * mue_ref[...]
    e_rows = jnp.broadcast_to(e, (SUBLANES, RWKV_EXTRA))
    e_tanh = jnp.tanh(e_rows).astype(BF16)
    e_lin = e_rows.astype(BF16)
    eye = lax.broadcasted_iota(jnp.int32, (hd, hd), 0) == lax.broadcasted_iota(jnp.int32, (hd, hd), 1)

    def mixed(i, h):
        cur = cur_ref[0, i, h:h + 1, :]
        return cur + (prev_ref[0, i, h:h + 1, :] - cur) * mu_ref[i, h:h + 1, :]

    for h in range(RWKV_HEADS):
        row = lambda ref: ref[h:h + 1, :]
        r, k, v = mixed(0, h), mixed(1, h), mixed(2, h)
        w = -_softplus(-(row(w0_ref) + jnp.dot(e_tanh, ww2_ref[h], preferred_element_type=F32)[0:1])) - 0.5
        decay = jnp.exp(-jnp.exp(w))
        a = _sigmoid(row(a0_ref) + jnp.dot(e_lin, wa2_ref[h], preferred_element_type=F32)[0:1])
        kk = k * row(kk_ref)
        kk = kk / jnp.maximum(jnp.sqrt(jnp.sum(kk * kk, axis=-1, keepdims=True)), 1e-12)
        k_mod = k * (1.0 + (a - 1.0) * row(ka_ref))
        s = st_ref[0, h]
        sa = jnp.sum(s * (-kk), axis=-1, keepdims=True)
        v_col = jnp.sum(jnp.where(eye, jnp.broadcast_to(v, (hd, hd)), 0.0), axis=-1, keepdims=True)
        s_new = s * decay + sa * (kk * a) + v_col * k_mod
        stn_ref[0, h] = s_new
        y_col = jnp.sum(s_new * r, axis=-1, keepdims=True)
        y = jnp.sum(jnp.where(eye, jnp.broadcast_to(y_col, (hd, hd)), 0.0), axis=0, keepdims=True)
        mean = jnp.mean(y, axis=-1, keepdims=True)
        d = y - mean
        var = jnp.mean(d * d, axis=-1, keepdims=True)
        yn = d * lax.rsqrt(var + GN_EPS) * row(lnw_ref) + row(lnb_ref)
        o_ref[0, h:h + 1, :] = yn + jnp.sum(r * k_mod * row(rk_ref), axis=-1, keepdims=True) * v


def wkv_step(cur, prev, e_cur, e_prev, mu, mu_e, w0, a0, k_k, k_a, r_k, ln_w, ln_b, ww2h, wa2h, state):
    batch = cur.shape[0]
    hshape = (RWKV_HEADS, RWKV_HEAD_DIM)
    per_row = lambda shape: pl.BlockSpec((1,) + shape, lambda b: (b,) + (0,) * len(shape))
    const = lambda shape: pl.BlockSpec(shape, lambda b: (0,) * len(shape))
    return pl.pallas_call(
        _wkv_step_kernel,
        grid=(batch,),
        in_specs=[per_row((3,) + hshape), per_row((3,) + hshape), per_row((1, RWKV_EXTRA)), per_row((1, RWKV_EXTRA)),
                  const((3,) + hshape), const((1, RWKV_EXTRA))] + [const(hshape)] * 7
                 + [const((RWKV_HEADS, RWKV_EXTRA, RWKV_HEAD_DIM))] * 2 + [per_row(hshape + (RWKV_HEAD_DIM,))],
        out_specs=[per_row(hshape), per_row(hshape + (RWKV_HEAD_DIM,))],
        out_shape=[jax.ShapeDtypeStruct((batch,) + hshape, F32),
                   jax.ShapeDtypeStruct((batch,) + hshape + (RWKV_HEAD_DIM,), F32)],
        compiler_params=_params("parallel"),
        name="wkv_step",
    )(cur, prev, e_cur, e_prev, mu, mu_e, w0, a0, k_k, k_a, r_k, ln_w, ln_b, ww2h, wa2h, state)


def _relayout_in_proj(w_in, extra_cols, extra_width):
    w_t = jnp.transpose(w_in).astype(BF16)
    gm_start = 3 * TOK_WIDTH + extra_cols
    return w_t, w_t[gm_start:], jnp.pad(w_t[3 * TOK_WIDTH:gm_start], ((0, extra_width - extra_cols), (0, 0)))


def _pad_lanes(v, width):
    return jnp.pad(v, (0, width - v.shape[0])).reshape(1, width)


def _memory_kv(mem, g_mem, w_mem):
    kv = norm_proj(mem, g_mem, w_mem, tm=mem.shape[0])
    return kv[:, :MEM_WIDTH], kv[:, MEM_WIDTH:]


def kernel(x_prompt, x_sample, cache_fox_k, cache_fox_v, cache_fox_logf, cache_mem_k, cache_mem_v, state_rwkv_wkv, state_rwkv_shift, page_table, mem_prompt, g_pre, g_post, g_mem, w_mem, w_o, fox_w_in, fox_b_f, rwkv_w_in, rwkv_mu, rwkv_w0, rwkv_w_w2, rwkv_a0, rwkv_w_a2, rwkv_k_k, rwkv_k_a, rwkv_r_k, rwkv_ln_w, rwkv_ln_b):
    bp, seq, _ = x_prompt.shape
    bs = x_sample.shape[0]
    xp = x_prompt.reshape(bp * seq, D_MODEL)
    xs = x_sample.reshape(bs, D_MODEL)
    mem = mem_prompt.reshape(bp * N_MEM, D_MODEL)
    mem_shape = (bp, N_MEM, MEM_HEADS, MEM_HEAD_DIM)

    w_fox = _relayout_in_proj(fox_w_in[0], FOX_HEADS, FOX_EXTRA)
    w_o = w_o.astype(BF16)
    b_pad = _pad_lanes(fox_b_f[0], LANES)
    mk0, mv0 = _memory_kv(mem, g_mem[0], w_mem[0])

    gm_p, q_p, k_p, v_p, ext_p = norm_proj(xp, g_pre[0], *w_fox, tm=PROJ_ROWS, heads=(bp, seq))
    logf_p, ccol, crow = fox_gate(ext_p, b_pad, batch=bp, seq=seq, tb=GATE_ROWS)
    o_tok = fox_flash(q_p, k_p, v_p, ccol, crow, tq=FLASH_ROWS)
    o_mem = mem_attn(gm_p.reshape(bp, seq, GM_WIDTH), mk0, mv0, tq=MEM_ATTN_ROWS, q_block=D_MODEL // MEM_WIDTH)
    xp = out_proj(o_tok, o_mem.reshape(bp * seq, MEM_WIDTH), gm_p, xp, w_o, g_post[0], layer=0, tm=OUT_PROJ_ROWS)

    main_s0, ext_s0 = norm_proj(xs, g_pre[0], *w_fox, tm=bs)
    q_s, k_s, v_s = (main_s0[:, off:off + TOK_WIDTH] for off in (TOKA_OFF, TOKB_OFF, TOKC_OFF))
    logf_pool_t = jnp.pad(jnp.transpose(cache_fox_logf[0], (0, 2, 1)), ((0, 0), (0, HEAD_PAD - FOX_HEADS), (0, 0)))
    k_pool, v_pool = (jnp.transpose(c, (0, 1, 3, 2, 4)) for c in (cache_fox_k, cache_fox_v))
    decode_args = (page_table, q_s, k_s, v_s, ext_s0, b_pad, k_pool, v_pool, logf_pool_t)

    head_shape = (FOX_HEADS, FOX_HEAD_DIM)
    fox_k_p = jnp.transpose(k_p, (0, 2, 1, 3))[None]
    fox_v_p = jnp.transpose(v_p, (0, 2, 1, 3))[None]
    fox_f_p = logf_p[:, :FOX_HEADS].reshape(1, bp, seq, FOX_HEADS)
    fox_k_s = k_s.reshape((1, bs, 1) + head_shape)
    fox_v_s = v_s.reshape((1, bs, 1) + head_shape)

    w_rwkv = _relayout_in_proj(rwkv_w_in[0], 2 * LORA, RWKV_EXTRA)
    mk1, mv1 = _memory_kv(mem, g_mem[1], w_mem[1])
    mu_rkv = rwkv_mu[0][:3 * TOK_WIDTH].reshape(1, 3 * TOK_WIDTH)
    mu_e = _pad_lanes(rwkv_mu[0][3 * TOK_WIDTH:], RWKV_EXTRA)
    ww2 = jnp.pad(rwkv_w_w2[0], ((0, RWKV_EXTRA - LORA), (0, 0))).astype(BF16)
    wa2 = jnp.pad(rwkv_w_a2[0], ((LORA, RWKV_EXTRA - 2 * LORA), (0, 0))).astype(BF16)
    vecs = [p[0] for p in (rwkv_w0, rwkv_a0, rwkv_k_k, rwkv_k_a, rwkv_r_k, rwkv_ln_w, rwkv_ln_b)]

    gm_p, r_p, k_p, v_p, ext_p = norm_proj(xp, g_pre[1], *w_rwkv, tm=PROJ_ROWS, heads=(bp, seq))
    o_tok, st_pairs, o_tok_s, logf_s = wkv_prompt(r_p, k_p, v_p, ext_p, mu_rkv, mu_e,
                                                  *[v.reshape(1, TOK_WIDTH) for v in vecs], ww2, wa2, decode_args,
                                                  tb=WKV_ROWS, decode_layer=0)
    o_mem = mem_attn(gm_p.reshape(bp, seq, GM_WIDTH), mk1, mv1, tq=MEM_ATTN_ROWS, q_block=D_MODEL // MEM_WIDTH)
    xp = out_proj(o_tok, o_mem.reshape(bp * seq, MEM_WIDTH), gm_p, xp, w_o, g_post[1], layer=1, tm=OUT_PROJ_ROWS)

    fox_f_s = logf_s[:, :FOX_HEADS].reshape(1, bs, 1, FOX_HEADS)
    o_mem_s = mem_attn(main_s0.reshape(bs, 1, MAIN_WIDTH), cache_mem_k[0].reshape(bs * N_MEM, MEM_WIDTH),
                       cache_mem_v[0].reshape(bs * N_MEM, MEM_WIDTH), tq=1, q_block=QMEM_OFF // MEM_WIDTH)
    xs = out_proj(o_tok_s, o_mem_s.reshape(bs, MEM_WIDTH), main_s0, xs, w_o, g_post[0], layer=0, tm=bs)

    main_s, ext_s = norm_proj(xs, g_pre[1], *w_rwkv, tm=bs)
    hshape = (RWKV_HEADS, RWKV_HEAD_DIM)
    shift_prev = state_rwkv_shift[0][:, 0, :]
    cur = main_s[:, TOKA_OFF:QMEM_OFF].reshape((bs, 3) + hshape)
    prev = shift_prev[:, :3 * TOK_WIDTH].reshape((bs, 3) + hshape)
    e_prev = jnp.pad(shift_prev[:, 3 * TOK_WIDTH:], ((0, 0), (0, RWKV_EXTRA - 2 * LORA))).reshape(bs, 1, RWKV_EXTRA)
    per_head = lambda w: jnp.transpose(w.reshape((RWKV_EXTRA,) + hshape), (1, 0, 2))
    o_tok_s, wkv_s = wkv_step(cur, prev, ext_s.reshape(bs, 1, RWKV_EXTRA), e_prev, mu_rkv.reshape((3,) + hshape), mu_e,
                              *[v.reshape(hshape) for v in vecs], per_head(ww2), per_head(wa2), state_rwkv_wkv[0])
    o_mem_s = mem_attn(main_s.reshape(bs, 1, MAIN_WIDTH), cache_mem_k[1].reshape(bs * N_MEM, MEM_WIDTH),
                       cache_mem_v[1].reshape(bs * N_MEM, MEM_WIDTH), tq=1, q_block=QMEM_OFF // MEM_WIDTH)
    xs = out_proj(o_tok_s.reshape(bs, TOK_WIDTH), o_mem_s.reshape(bs, MEM_WIDTH), main_s, xs, w_o, g_post[1], layer=1, tm=bs)

    wkv_p = st_pairs.reshape((bp,) + hshape + (RWKV_HEAD_DIM,))
    last_p = jnp.concatenate([a[:, :, seq - 1, :].reshape(bp, TOK_WIDTH) for a in (r_p, k_p, v_p)], axis=-1)
    last_e = ext_p.reshape(bp, seq, RWKV_EXTRA)[:, seq - 1, :2 * LORA]
    shift_p = jnp.concatenate([last_p, last_e], axis=-1).reshape(1, bp, 1, 3 * TOK_WIDTH + 2 * LORA)
    shift_s = jnp.concatenate([main_s[:, TOKA_OFF:QMEM_OFF], ext_s[:, :2 * LORA]], axis=-1).reshape(1, bs, 1, 3 * TOK_WIDTH + 2 * LORA)

    return (xp.reshape(bp, seq, D_MODEL), xs.reshape(bs, 1, D_MODEL),
            jnp.stack([mk0.reshape(mem_shape), mk1.reshape(mem_shape)]),
            jnp.stack([mv0.reshape(mem_shape), mv1.reshape(mem_shape)]),
            fox_k_p, fox_v_p, fox_f_p, fox_k_s, fox_v_s, fox_f_s,
            wkv_p[None], shift_p, wkv_s[None], shift_s)
```

```python
import functools
import math

import jax
import jax.numpy as jnp
from jax import lax
from jax.experimental import pallas as pl
from jax.experimental.pallas import tpu as pltpu

F32 = jnp.float32
BF16 = jnp.bfloat16

D_MODEL = 2048
N_MEM = 256
MEM_HEADS = 4
MEM_HEAD_DIM = 128
MEM_WIDTH = MEM_HEADS * MEM_HEAD_DIM
TOK_WIDTH = D_MODEL - MEM_WIDTH
FOX_HEAD_DIM = 128
FOX_HEADS = TOK_WIDTH // FOX_HEAD_DIM
RWKV_HEAD_DIM = 64
RWKV_HEADS = TOK_WIDTH // RWKV_HEAD_DIM
LORA = 96
PAGE_SIZE = 128
RMS_EPS = 1e-6
GN_EPS = 64e-5

LANES = 128
SUBLANES = 8
V7X_VMEM_BYTES = 64 * 1024 * 1024
VMEM_LIMIT_BYTES = V7X_VMEM_BYTES - 8 * 1024 * 1024

GATE_OFF = 0
TOKA_OFF = D_MODEL
TOKB_OFF = TOKA_OFF + TOK_WIDTH
TOKC_OFF = TOKB_OFF + TOK_WIDTH
QMEM_OFF = TOKC_OFF + TOK_WIDTH
MAIN_WIDTH = QMEM_OFF + MEM_WIDTH
FOX_EXTRA = LANES
RWKV_EXTRA = 2 * LANES

HEAD_PAD = 16
CHUNK = 64

PROJ_ROWS = 1024
GATE_ROWS = 256
FLASH_ROWS = 512
MEM_ATTN_ROWS = 1024
OUT_PROJ_ROWS = 256
WKV_ROWS = 8 * CHUNK


def _params(*sem):
    return pltpu.CompilerParams(dimension_semantics=sem, vmem_limit_bytes=VMEM_LIMIT_BYTES)


def _rms_scale(x):
    return lax.rsqrt(jnp.mean(x * x, axis=-1, keepdims=True) + RMS_EPS)


def _sigmoid(x):
    return 1.0 / (1.0 + jnp.exp(-x))


def _softplus(x):
    return jnp.maximum(x, 0.0) + jnp.log(1.0 + jnp.exp(-jnp.abs(x)))


def _split(x, pieces):
    out = []
    for _ in range(pieces - 1):
        p = x.astype(BF16)
        out.append(p)
        x = x - p.astype(F32)
    return out + [x.astype(BF16)]


def _dot_exact_rhs(x, rhs, pieces=3):
    rows = x.shape[0]
    parts = jnp.dot(jnp.concatenate(_split(x, pieces), axis=0), rhs.astype(BF16), preferred_element_type=F32)
    return sum(parts[i * rows:(i + 1) * rows] for i in range(pieces))


def _dot_exact_lhs(lhs, x, pieces=3):
    cols = x.shape[1]
    parts = jnp.dot(lhs.astype(BF16), jnp.concatenate(_split(x, pieces), axis=1), preferred_element_type=F32)
    return sum(parts[:, i * cols:(i + 1) * cols] for i in range(pieces))


TN = 4 * LANES
GATE_BLOCKS = D_MODEL // TN
TOK_BLOCKS = TOK_WIDTH // TN
QMEM_BLOCK = GATE_BLOCKS + 3 * TOK_BLOCKS
GM_WIDTH = D_MODEL + MEM_WIDTH
N_GROUPS128 = TOK_WIDTH // LANES
assert MAIN_WIDTH == (QMEM_BLOCK + 1) * TN and MEM_WIDTH == TN


def _nt_dot(a, b):
    return lax.dot_general(a, b, (((1,), (1,)), ((), ())), preferred_element_type=F32)


def _norm_proj_kernel(x_ref, g_ref, wt_ref, *rest, has_gm, has_extra, head_mode):
    rest = list(rest)
    wg_ref = rest.pop(0) if has_gm else None
    we_ref = rest.pop(0) if has_extra else None
    h_ref = rest.pop()
    oe_ref = rest.pop() if has_extra else None
    j = pl.program_id(1)

    @pl.when(j == 0)
    def _():
        x = x_ref[...]
        h = (x * _rms_scale(x) * g_ref[...]).astype(BF16)
        h_ref[...] = h
        if has_extra:
            oe_ref[...] = _nt_dot(h, we_ref[...])

    if not has_gm:
        rest[0][...] = jnp.dot(h_ref[...], wt_ref[...].astype(BF16), preferred_element_type=F32)
        return
    in_gm = (j < GATE_BLOCKS) | (j == QMEM_BLOCK)

    @pl.when(in_gm)
    def _():
        rest[0][...] = _nt_dot(h_ref[...], wg_ref[...])

    if not head_mode:
        @pl.when(jnp.logical_not(in_gm))
        def _():
            rest[0][...] = _nt_dot(h_ref[...], wt_ref[...])
        return

    for gi, ref in enumerate(rest[1:]):
        first = GATE_BLOCKS + gi * TOK_BLOCKS

        @pl.when((j >= first) & (j < first + TOK_BLOCKS))
        def _(ref=ref):
            res = _nt_dot(h_ref[...], wt_ref[...])
            for hh in range(TN // LANES):
                ref[0, hh] = res[:, hh * LANES:(hh + 1) * LANES]


def norm_proj(x, g, w, w_gm_t=None, w_extra_t=None, *, tm, heads=None):
    m, d = x.shape
    has_gm = w_gm_t is not None
    has_extra = w_extra_t is not None
    n = MAIN_WIDTH if has_gm else w.shape[1]
    assert m % tm == 0 and n % TN == 0 and (has_gm or heads is None)
    gm_col = lambda j: jnp.where(j == QMEM_BLOCK, GATE_BLOCKS, jnp.minimum(j, GATE_BLOCKS - 1))
    gm_row = lambda j: jnp.where(j == QMEM_BLOCK, 0, jnp.minimum(j, GATE_BLOCKS - 1) + MEM_WIDTH // TN)
    in_specs = [pl.BlockSpec((tm, d), lambda i, j: (i, 0)), pl.BlockSpec((1, d), lambda i, j: (0, 0))]
    args = [x, g.reshape(1, d), w]
    if has_gm:
        in_specs.append(pl.BlockSpec((TN, d), lambda i, j: (jnp.clip(j - GATE_BLOCKS, 0, 3 * TOK_BLOCKS - 1), 0)))
        in_specs.append(pl.BlockSpec((TN, d), lambda i, j: (gm_row(j), 0)))
        args.append(w_gm_t)
    else:
        in_specs.append(pl.BlockSpec((d, TN), lambda i, j: (0, j)))
    if has_extra:
        in_specs.append(pl.BlockSpec(w_extra_t.shape, lambda i, j: (0, 0)))
        args.append(w_extra_t)
    if heads is None:
        out_specs = [pl.BlockSpec((tm, TN), lambda i, j: (i, j))]
        out_shape = [jax.ShapeDtypeStruct((m, n), F32)]
    else:
        batch, seq = heads
        assert seq % tm == 0
        per_seq = seq // tm
        out_specs = [pl.BlockSpec((tm, TN), lambda i, j: (i, gm_col(j)))]
        out_shape = [jax.ShapeDtypeStruct((m, GM_WIDTH), F32)]
        for gi in range(3):
            first = GATE_BLOCKS + gi * TOK_BLOCKS
            out_specs.append(pl.BlockSpec(
                (1, TN // LANES, tm, LANES),
                lambda i, j, first=first: (i // per_seq, jnp.clip(j - first, 0, TOK_BLOCKS - 1), i % per_seq, 0)))
            out_shape.append(jax.ShapeDtypeStruct((batch, N_GROUPS128, seq, LANES), F32))
    if has_extra:
        e = w_extra_t.shape[0]
        out_specs.append(pl.BlockSpec((tm, e), lambda i, j: (i, 0)))
        out_shape.append(jax.ShapeDtypeStruct((m, e), F32))
    outs = pl.pallas_call(
        functools.partial(_norm_proj_kernel, has_gm=has_gm, has_extra=has_extra, head_mode=heads is not None),
        grid=(m // tm, n // TN),
        in_specs=in_specs,
        out_specs=out_specs,
        out_shape=out_shape,
        scratch_shapes=[pltpu.VMEM((tm, d), BF16)],
        compiler_params=_params("parallel", "arbitrary"),
        name="norm_proj",
    )(*args)
    return outs if len(outs) > 1 else outs[0]


def _fox_gate_kernel(f_ref, b_ref, logf_ref, ccol_ref, crow_ref, carry_ref, *, tb):
    @pl.when(pl.program_id(1) == 0)
    def _():
        carry_ref[...] = jnp.zeros_like(carry_ref)

    logf = -_softplus(-(f_ref[...] + b_ref[...]))
    logf_ref[...] = logf
    row = lax.broadcasted_iota(jnp.int32, (tb, tb), 0)
    col = lax.broadcasted_iota(jnp.int32, (tb, tb), 1)
    c = _dot_exact_lhs(jnp.where(col <= row, 1.0, 0.0), logf) + carry_ref[...]
    ccol_ref[...] = c
    crow_ref[0] = c.T[:HEAD_PAD]
    carry_ref[...] = c[tb - 1:tb, :]


def fox_gate(f_ext, b_pad, *, batch, seq, tb):
    nt = seq // tb
    row_spec = pl.BlockSpec((tb, LANES), lambda b, t: (b * nt + t, 0))
    return pl.pallas_call(
        functools.partial(_fox_gate_kernel, tb=tb),
        grid=(batch, nt),
        in_specs=[row_spec, pl.BlockSpec((1, LANES), lambda b, t: (0, 0))],
        out_specs=[row_spec, row_spec, pl.BlockSpec((1, HEAD_PAD, tb), lambda b, t: (b, 0, t))],
        out_shape=[
            jax.ShapeDtypeStruct((batch * seq, LANES), F32),
            jax.ShapeDtypeStruct((batch * seq, LANES), F32),
            jax.ShapeDtypeStruct((batch, HEAD_PAD, seq), F32),
        ],
        scratch_shapes=[pltpu.VMEM((1, LANES), F32)],
        compiler_params=_params("parallel", "arbitrary"),
        name="fox_gate",
    )(f_ext, b_pad)


_MASKED = -1e30
LOG2E = 1.4426950408889634


def _flash_tile(h, qi, q_ref, k_ref, v_ref, ccol_ref, crow_ref, o_ref, kb_ref, vb_ref, *, tq):
    @pl.when(qi == 0)
    def _():
        kb_ref[...] = k_ref[0, 0].astype(BF16)
        vb_ref[...] = v_ref[0, 0].astype(BF16)

    q = (q_ref[0, 0] * (FOX_HEAD_DIM ** -0.5 * LOG2E)).astype(BF16)
    lane = lax.broadcasted_iota(jnp.int32, (tq, LANES), 1)
    cq = jnp.sum(jnp.where(lane == h, ccol_ref[...], 0.0), axis=-1, keepdims=True) * LOG2E

    def step(c, carry, diagonal):
        m, l, acc = carry
        start = pl.multiple_of(c * tq, tq)
        kc = kb_ref[pl.ds(start, tq), :]
        vc = vb_ref[pl.ds(start, tq), :]
        ck = crow_ref[0, 0, :, pl.ds(start, tq)] * LOG2E
        s = lax.dot_general(q, kc, (((1,), (1,)), ((), ())), preferred_element_type=F32) - ck
        if diagonal:
            row = lax.broadcasted_iota(jnp.int32, (tq, tq), 0)
            col = lax.broadcasted_iota(jnp.int32, (tq, tq), 1)
            s = jnp.where(col <= row, s, _MASKED)
        m_new = jnp.maximum(m, jnp.max(s, axis=-1, keepdims=True) + cq)
        p = jnp.exp2(s - (m_new - cq))
        alpha = jnp.exp2(m - m_new)
        l = alpha * l + jnp.sum(p, axis=-1, keepdims=True)
        acc = alpha * acc + jnp.dot(p.astype(BF16), vc, preferred_element_type=F32)
        return m_new, l, acc

    init = (jnp.full((tq, 1), _MASKED, F32), jnp.zeros((tq, 1), F32), jnp.zeros((tq, FOX_HEAD_DIM), F32))
    carry = lax.fori_loop(0, qi, lambda c, cr: step(c, cr, False), init)
    _, l, acc = step(qi, carry, True)
    o_ref[...] = acc / l


PAGES_PER_STEP = 8


def _head_mask():
    row = lax.broadcasted_iota(jnp.int32, (HEAD_PAD, TOK_WIDTH), 0)
    col = lax.broadcasted_iota(jnp.int32, (HEAD_PAD, TOK_WIDTH), 1)
    lo = row * FOX_HEAD_DIM
    return (col >= lo) & (col < lo + FOX_HEAD_DIM)


def _decode_pages(b, j, last, pt_ref, q_ref, kn_ref, vn_ref, f_ref, bf_ref, lf_ref, k_refs, v_refs,
                  o_ref, lfn_ref, qbd_ref, m_ref, l_ref, acc_ref, carry_ref):
    pps = PAGES_PER_STEP
    n_pages = pt_ref.shape[1]
    scale = FOX_HEAD_DIM ** -0.5
    hmask = _head_mask()

    @pl.when(j == 0)
    def _():
        q = q_ref[pl.ds(b, 1), :]
        q_bd = jnp.where(hmask, jnp.broadcast_to(q, (HEAD_PAD, TOK_WIDTH)), 0.0)
        qbd_ref[...] = q_bd.astype(BF16)
        logf_new = -_softplus(-(f_ref[pl.ds(b, 1), :] + bf_ref[...]))
        lfn_ref[pl.ds(b, 1), :] = logf_new
        eye = lax.broadcasted_iota(jnp.int32, (HEAD_PAD, LANES), 0) == lax.broadcasted_iota(jnp.int32, (HEAD_PAD, LANES), 1)
        carry_ref[...] = jnp.sum(jnp.where(eye, jnp.broadcast_to(logf_new, (HEAD_PAD, LANES)), 0.0), axis=-1, keepdims=True)
        m_ref[...] = jnp.sum(q_bd * kn_ref[pl.ds(b, 1), :], axis=-1, keepdims=True) * scale
        l_ref[...] = jnp.ones_like(l_ref)
        acc_ref[...] = jnp.broadcast_to(vn_ref[pl.ds(b, 1), :], (HEAD_PAD, TOK_WIDTH))

    def page_rows(ref):
        return jnp.concatenate([ref[0, 0, h].astype(BF16) for h in range(FOX_HEADS)], axis=1)

    lfs = [lf_ref[pt_ref[b, n_pages - 1 - j * pps - i]] for i in range(pps)]
    src = lax.broadcasted_iota(jnp.int32, (PAGE_SIZE, PAGE_SIZE), 0)
    dst = lax.broadcasted_iota(jnp.int32, (PAGE_SIZE, PAGE_SIZE), 1)
    inside = _dot_exact_rhs(jnp.concatenate(lfs, axis=0), jnp.where(src > dst, 1.0, 0.0))
    carries = [carry_ref[...]]
    for lf in lfs:
        carries.append(carries[-1] + jnp.sum(lf, axis=-1, keepdims=True))
    carry_ref[...] = carries[-1]

    q_bd = qbd_ref[...]
    scores = []
    for i in range(pps):
        s = lax.dot_general(q_bd, page_rows(k_refs[i]), (((1,), (1,)), ((), ())), preferred_element_type=F32)
        scores.append(s * scale + (inside[i * HEAD_PAD:(i + 1) * HEAD_PAD] + carries[i]))
    m_old = m_ref[...]
    m_new = m_old
    for s in scores:
        m_new = jnp.maximum(m_new, jnp.max(s, axis=-1, keepdims=True))
    alpha = jnp.exp(m_old - m_new)
    l_new = alpha * l_ref[...]
    acc = alpha * acc_ref[...]
    for i, s in enumerate(scores):
        p = jnp.exp(s - m_new)
        l_new = l_new + jnp.sum(p, axis=-1, keepdims=True)
        acc = acc + jnp.dot(p.astype(BF16), page_rows(v_refs[i]), preferred_element_type=F32)
    m_ref[...] = m_new
    l_ref[...] = l_new
    acc_ref[...] = acc

    @pl.when(last)
    def _():
        out = jnp.where(hmask, acc / l_new, 0.0)
        o_ref[pl.ds(b, 1), :] = jnp.sum(out, axis=0, keepdims=True)


def _decode_plan(page_table, q_s, k_new, v_new, f_ext, b_pad, k_pool, v_pool, logf_pool_t, *, layer, step_of):
    rows, n_pages = page_table.shape
    pps = PAGES_PER_STEP
    assert n_pages % pps == 0
    steps_per_row = n_pages // pps
    n_steps = rows * steps_per_row

    def page_id(i):
        def index(*idx):
            pt = idx[-1]
            d = jnp.minimum(step_of(*idx[:-1]), n_steps - 1)
            return pt[lax.div(d, steps_per_row), n_pages - 1 - lax.rem(d, steps_per_row) * pps - i]
        return index

    whole = lambda shape: pl.BlockSpec(shape, lambda *idx: (0,) * len(shape))
    page_spec = lambda i: pl.BlockSpec((1, 1, FOX_HEADS, PAGE_SIZE, FOX_HEAD_DIM),
                                       lambda *idx: (layer, page_id(i)(*idx), 0, 0, 0))
    return dict(
        in_specs=[whole((rows, TOK_WIDTH))] * 3 + [whole((rows, LANES)), whole((1, LANES)), whole(logf_pool_t.shape)]
                 + [page_spec(i) for i in range(pps)] * 2,
        args=[q_s, k_new, v_new, f_ext, b_pad, logf_pool_t] + [k_pool] * pps + [v_pool] * pps,
        out_specs=[whole((rows, TOK_WIDTH)), whole((rows, LANES))],
        out_shape=[jax.ShapeDtypeStruct((rows, TOK_WIDTH), F32), jax.ShapeDtypeStruct((rows, LANES), F32)],
        scratch_shapes=[
            pltpu.VMEM((HEAD_PAD, TOK_WIDTH), BF16),
            pltpu.VMEM((HEAD_PAD, 1), F32),
            pltpu.VMEM((HEAD_PAD, 1), F32),
            pltpu.VMEM((HEAD_PAD, TOK_WIDTH), F32),
            pltpu.VMEM((HEAD_PAD, 1), F32),
        ],
        steps_per_row=steps_per_row, n_steps=n_steps,
    )


N_DECODE_INPUTS = 6 + 2 * PAGES_PER_STEP
N_DECODE_OUTPUTS = 2
N_DECODE_SCRATCH = 5


def _decode_step(n, pt_ref, inputs, outputs, scratch, *, steps_per_row, n_steps):
    pps = PAGES_PER_STEP
    q_ref, kn_ref, vn_ref, f_ref, bf_ref, lf_ref = inputs[:6]
    k_refs, v_refs = inputs[6:6 + pps], inputs[6 + pps:]

    @pl.when(n < n_steps)
    def _():
        j = lax.rem(n, steps_per_row)
        _decode_pages(lax.div(n, steps_per_row), j, j == steps_per_row - 1, pt_ref, q_ref, kn_ref, vn_ref, f_ref,
                      bf_ref, lf_ref, k_refs, v_refs, *outputs, *scratch)


N_FLASH_INPUTS = 5


def _fox_flash_kernel(*refs, tq, nq, cast_blocks):
    n_casts = len(cast_blocks)
    cast_in = refs[N_FLASH_INPUTS:N_FLASH_INPUTS + n_casts]
    o_ref = refs[N_FLASH_INPUTS + n_casts]
    cast_out = refs[N_FLASH_INPUTS + n_casts + 1:N_FLASH_INPUTS + 2 * n_casts + 1]
    kb_ref, vb_ref = refs[N_FLASH_INPUTS + 2 * n_casts + 1:]
    _flash_tile(pl.program_id(1), pl.program_id(2), *refs[:N_FLASH_INPUTS], o_ref, kb_ref, vb_ref, tq=tq)
    step = (pl.program_id(0) * FOX_HEADS + pl.program_id(1)) * nq + pl.program_id(2)
    for src, dst, blocks in zip(cast_in, cast_out, cast_blocks):
        @pl.when(step < blocks)
        def _(src=src, dst=dst):
            dst[...] = src[...].astype(BF16)


def fox_flash(q, k, v, ccol, crow, *, tq, casts=()):
    batch, _, seq, _ = q.shape
    nq = seq // tq
    n_steps = batch * FOX_HEADS * nq
    crow4 = crow.reshape(batch, HEAD_PAD, 1, seq)
    kv_spec = pl.BlockSpec((1, 1, seq, FOX_HEAD_DIM), lambda b, h, i: (b, h, 0, 0))
    cast_specs, cast_shapes, cast_blocks = [], [], []
    for a in casts:
        rows, width = a.shape
        rb = next(r for r in range(2 * SUBLANES, rows + 1, 2 * SUBLANES) if rows % r == 0 and rows // r <= n_steps)
        blocks = rows // rb
        cast_specs.append(pl.BlockSpec(
            (rb, width), lambda b, h, i, blocks=blocks: (jnp.minimum((b * FOX_HEADS + h) * nq + i, blocks - 1), 0)))
        cast_shapes.append(jax.ShapeDtypeStruct(a.shape, BF16))
        cast_blocks.append(blocks)
    outs = pl.pallas_call(
        functools.partial(_fox_flash_kernel, tq=tq, nq=nq, cast_blocks=tuple(cast_blocks)),
        grid=(batch, FOX_HEADS, nq),
        in_specs=[
            pl.BlockSpec((1, 1, tq, FOX_HEAD_DIM), lambda b, h, i: (b, h, i, 0)),
            kv_spec, kv_spec,
            pl.BlockSpec((tq, LANES), lambda b, h, i: (b * nq + i, 0)),
            pl.BlockSpec((1, 1, 1, seq), lambda b, h, i: (b, h, 0, 0)),
        ] + cast_specs,
        out_specs=[pl.BlockSpec((tq, LANES), lambda b, h, i: (b * nq + i, h))] + cast_specs,
        out_shape=[jax.ShapeDtypeStruct((batch * seq, TOK_WIDTH), F32)] + cast_shapes,
        scratch_shapes=[pltpu.VMEM((seq, LANES), BF16), pltpu.VMEM((seq, LANES), BF16)],
        compiler_params=_params("arbitrary", "arbitrary", "arbitrary"),
        name="fox_flash",
    )(q, k, v, ccol, crow4, *casts)
    return outs[0] if not casts else outs


def _mem_attn_kernel(q_ref, k_ref, v_ref, o_ref, *, tq):
    rows = max(tq, SUBLANES)
    scale = MEM_HEAD_DIM ** -0.5
    for h in range(MEM_HEADS):
        sl = slice(h * MEM_HEAD_DIM, (h + 1) * MEM_HEAD_DIM)
        qh = jnp.broadcast_to(q_ref[0, :, sl], (rows, MEM_HEAD_DIM)).astype(BF16)
        kh = k_ref[:, sl].astype(BF16)
        vh = v_ref[:, sl].astype(BF16)
        s = lax.dot_general(qh, kh, (((1,), (1,)), ((), ())), preferred_element_type=F32) * scale
        e = jnp.exp(s - jnp.max(s, axis=-1, keepdims=True))
        oh = jnp.dot(e.astype(BF16), vh, preferred_element_type=F32) / jnp.sum(e, axis=-1, keepdims=True)
        o_ref[0, :, sl] = oh[:tq]


def mem_attn(main3, mem_k, mem_v, *, tq, q_block):
    batch, seq, _ = main3.shape
    kv_spec = pl.BlockSpec((N_MEM, MEM_WIDTH), lambda b, i: (b, 0))
    return pl.pallas_call(
        functools.partial(_mem_attn_kernel, tq=tq),
        grid=(batch, seq // tq),
        in_specs=[pl.BlockSpec((1, tq, MEM_WIDTH), lambda b, i: (b, i, q_block)), kv_spec, kv_spec],
        out_specs=pl.BlockSpec((1, tq, MEM_WIDTH), lambda b, i: (b, i, 0)),
        out_shape=jax.ShapeDtypeStruct((batch, seq, MEM_WIDTH), F32),
        compiler_params=_params("parallel", "parallel"),
        name="mem_attn",
    )(main3, mem_k, mem_v)


def _out_proj_kernel(ot_ref, om_ref, gate_ref, x_ref, w_ref, gp_ref, y_ref):
    gate = gate_ref[...]
    sg = gate * _sigmoid(gate)
    a = (ot_ref[...] * sg[:, :TOK_WIDTH]).astype(BF16)
    b = (om_ref[...] * sg[:, TOK_WIDTH:]).astype(BF16)
    y = (jnp.dot(a, w_ref[0, :TOK_WIDTH], preferred_element_type=F32)
         + jnp.dot(b, w_ref[0, TOK_WIDTH:], preferred_element_type=F32))
    y_ref[...] = x_ref[...] + y * _rms_scale(y) * gp_ref[...]


def out_proj(o_tok, o_mem, main, x, w_o, g_post, *, layer, tm):
    m = x.shape[0]
    row = lambda width: pl.BlockSpec((tm, width), lambda i: (i, 0))
    return pl.pallas_call(
        _out_proj_kernel,
        grid=(m // tm,),
        in_specs=[row(TOK_WIDTH), row(MEM_WIDTH), row(D_MODEL), row(D_MODEL),
                  pl.BlockSpec((1, D_MODEL, D_MODEL), lambda i: (layer, 0, 0)),
                  pl.BlockSpec((1, D_MODEL), lambda i: (0, 0))],
        out_specs=row(D_MODEL),
        out_shape=jax.ShapeDtypeStruct((m, D_MODEL), F32),
        compiler_params=_params("parallel"),
        name="out_proj",
    )(o_tok, o_mem, main, x, w_o, g_post.reshape(1, D_MODEL))


PAIR = 2 * RWKV_HEAD_DIM
assert PAIR == LANES
N_PAIRS = RWKV_HEADS // 2


def _pair_ones():
    row = lax.broadcasted_iota(jnp.int32, (PAIR, PAIR), 0) // RWKV_HEAD_DIM
    col = lax.broadcasted_iota(jnp.int32, (PAIR, PAIR), 1) // RWKV_HEAD_DIM
    return jnp.where(row == col, 1.0, 0.0).astype(F32)


def _head_sum(x, ones):
    return _dot_exact_rhs(x, ones, pieces=2)


def _stack(x):
    lane = lax.broadcasted_iota(jnp.int32, x.shape, 1)
    first = lane < RWKV_HEAD_DIM
    return jnp.concatenate([jnp.where(first, x, 0.0), jnp.where(first, 0.0, x)], axis=0)


def _bdot(a, b):
    return jnp.dot(a.astype(BF16), b.astype(BF16), preferred_element_type=F32)


def _wkv_chunk_terms(r, k, v, a_vec, b_vec, lw, between_stages=lambda: None):
    c = CHUNK
    chunks = [slice(c0, c0 + c) for c0 in range(0, r.shape[0], c)]
    row = lax.broadcasted_iota(jnp.int32, (c, c), 0)
    col = lax.broadcasted_iota(jnp.int32, (c, c), 1)
    tril = jnp.where(col <= row, 1.0, 0.0)
    gi = lax.broadcasted_iota(jnp.int32, (4 * c, 4 * c), 0)
    gj = lax.broadcasted_iota(jnp.int32, (4 * c, 4 * c), 1)
    keep = (gj & (c - 1)) < (gi & (c - 1)) + gi // (2 * c)
    eye = jnp.where(lax.broadcasted_iota(jnp.int32, (2 * c, 2 * c), 0) == lax.broadcasted_iota(jnp.int32, (2 * c, 2 * c), 1),
                    1.0, 0.0).astype(F32)

    cum = [_dot_exact_lhs(tril, lw[s]) for s in chunks]
    between_stages()
    last = [x[c - 1:c, :] for x in cum]
    a_t = [_stack(a_vec[s] * jnp.exp(cu - lw[s])) for s, cu in zip(chunks, cum)]
    r_t = [_stack(r[s] * jnp.exp(cu)) for s, cu in zip(chunks, cum)]
    v_s = [_stack(v[s]) for s in chunks]
    xs = [jnp.concatenate([a, rr], axis=0).astype(BF16) for a, rr in zip(a_t, r_t)]
    ys = [jnp.concatenate([_stack(b_vec[s] * jnp.exp(-cu)), _stack(k[s] * jnp.exp(-cu))], axis=0).astype(BF16)
          for s, cu in zip(chunks, cum)]
    g = [jnp.where(keep, lax.dot_general(x, y, (((1,), (1,)), ((), ())), preferred_element_type=F32), 0.0)
         for x, y in zip(xs, ys)]
    between_stages()
    l_ab = [x[:2 * c, :2 * c] for x in g]
    l_ak = [x[:2 * c, 2 * c:] for x in g]
    l_rb = [x[2 * c:, :2 * c] for x in g]
    l_rk = [x[2 * c:, 2 * c:] for x in g]
    t_inv = [eye + x for x in l_ab]
    l_pow = l_ab
    for _ in range(int(math.log2(c)) - 1):
        l_pow = [_bdot(x, x) for x in l_pow]
        between_stages()
        t_inv = [t + _bdot(p, t) for p, t in zip(l_pow, t_inv)]
        between_stages()
    akv = [_bdot(x, y) for x, y in zip(l_ak, v_s)]
    ta = [_bdot(t, jnp.concatenate([a, u], axis=1)) for t, a, u in zip(t_inv, a_t, akv)]
    between_stages()
    rbx = [_bdot(x, y) for x, y in zip(l_rb, ta)]
    rkv = [_bdot(x, y) for x, y in zip(l_rk, v_s)]
    between_stages()
    r_new = [x + y[:, :PAIR] for x, y in zip(r_t, rbx)]
    y0 = [x[:, PAIR:] + y for x, y in zip(rbx, rkv)]
    e_rem = [jnp.exp(la - cu) for la, cu in zip(last, cum)]
    mn = [_bdot(_stack(b_vec[s] * e).T, t) for s, e, t in zip(chunks, e_rem, ta)]
    kv = [_bdot(_stack(k[s] * e).T, vv) for s, e, vv in zip(chunks, e_rem, v_s)]
    m_c = [x[:, :PAIR] + eye * jnp.exp(la) for x, la in zip(mn, last)]
    n_c = [x[:, PAIR:] + y for x, y in zip(mn, kv)]
    return list(zip(r_new, y0, m_c, n_c))


def _token_shift(cur, prev_row, mu):
    rolled = pltpu.roll(cur, 1, 0)
    first = lax.broadcasted_iota(jnp.int32, cur.shape, 0) == 0
    prev = jnp.where(first, prev_row, rolled)
    return cur + (prev - cur) * mu


def _rwkv_transforms(r, k, v, e, w0, a0, k_k, k_a, ww2, wa2, ones):
    w = -_softplus(-(w0 + jnp.dot(jnp.tanh(e).astype(BF16), ww2, preferred_element_type=F32))) - 0.5
    lw = -jnp.exp(w)
    a = _sigmoid(a0 + jnp.dot(e.astype(BF16), wa2, preferred_element_type=F32))
    kk = k * k_k
    kk = kk / jnp.maximum(jnp.sqrt(_head_sum(kk * kk, ones)), 1e-12)
    k_mod = k * (1.0 + (a - 1.0) * k_a)
    return lw, a, kk, k_mod


def _group_norm(y, ln_w, ln_b, ones):
    mean = _head_sum(y, ones) * (1.0 / RWKV_HEAD_DIM)
    d = y - mean
    var = _head_sum(d * d, ones) * (1.0 / RWKV_HEAD_DIM)
    return d * lax.rsqrt(var + GN_EPS) * ln_w + ln_b


N_WKV_INPUTS = 17
N_WKV_SCRATCH = 11


def _wkv_prompt_kernel(pt_ref, *refs, tb, nt, steps_per_row, n_decode_steps):
    refs = list(refs)
    (r_ref, k_ref, v_ref, e_ref, mur_ref, muk_ref, muv_ref, mue_ref, w0_ref, a0_ref,
     kk_ref, ka_ref, rk_ref, ww2_ref, wa2_ref, lnw_ref, lnb_ref) = refs[:N_WKV_INPUTS]
    decode_inputs = refs[N_WKV_INPUTS:N_WKV_INPUTS + N_DECODE_INPUTS]
    o_ref, st_ref, *decode_outputs = refs[N_WKV_INPUTS + N_DECODE_INPUTS:N_WKV_INPUTS + N_DECODE_INPUTS + 2 + N_DECODE_OUTPUTS]
    scratch = refs[N_WKV_INPUTS + N_DECODE_INPUTS + 2 + N_DECODE_OUTPUTS:]
    (pr_ref, pk_ref, pv_ref, pe_ref, s_ref, y_ref, rn_ref, y0_ref, mc_ref, nc_ref, bonus_ref) = scratch[:N_WKV_SCRATCH]
    n = pl.program_id(0)
    nc = tb // CHUNK
    ones = _pair_ones()

    @pl.when(n == 0)
    def _():
        for ref in (rn_ref, y0_ref, mc_ref, nc_ref, bonus_ref, s_ref):
            ref[...] = jnp.zeros_like(ref)

    @pl.when(lax.rem(n, nt) == 0)
    def _():
        for ref in (pr_ref, pk_ref, pv_ref, pe_ref):
            ref[...] = jnp.zeros_like(ref)

    prev_starts_sequence = lax.rem(n - 1, nt) == 0
    state = [jnp.where(prev_starts_sequence, 0.0, s_ref[...])]
    pending = list(range(nc))

    def recurrence_step():
        if not pending:
            return
        i = pending.pop(0)
        s = state[0]
        ys = _bdot(rn_ref[i], s) + y0_ref[i]
        y_ref[i * CHUNK:(i + 1) * CHUNK, :] = ys[:CHUNK] + ys[CHUNK:]
        state[0] = _bdot(mc_ref[i], s) + nc_ref[i]

    r_cur, k_cur, v_cur, e_cur = r_ref[0, 0], k_ref[0, 0], v_ref[0, 0], e_ref[...]
    r = _token_shift(r_cur, pr_ref[...], mur_ref[...])
    k = _token_shift(k_cur, pk_ref[...], muk_ref[...])
    v = _token_shift(v_cur, pv_ref[...], muv_ref[...])
    e = _token_shift(e_cur, pe_ref[...], mue_ref[...])
    for cur, prev in ((r_cur, pr_ref), (k_cur, pk_ref), (v_cur, pv_ref), (e_cur, pe_ref)):
        prev[...] = cur[tb - 1:tb, :]
    recurrence_step()
    lw, a, kk, k_mod = _rwkv_transforms(r, k, v, e, w0_ref[...], a0_ref[...], kk_ref[...], ka_ref[...],
                                        ww2_ref[...], wa2_ref[...], ones)
    recurrence_step()
    bonus_new = _head_sum(r * k_mod * rk_ref[...], ones) * v
    terms = _wkv_chunk_terms(r, k_mod, v, -kk, kk * a, lw, between_stages=recurrence_step)
    while pending:
        recurrence_step()
    s_ref[...] = state[0]
    state_t = state[0].T
    for hh in range(2):
        block = slice(hh * RWKV_HEAD_DIM, (hh + 1) * RWKV_HEAD_DIM)
        st_ref[0, 0, hh] = state_t[block, block]
    o_ref[...] = _group_norm(y_ref[...], lnw_ref[...], lnb_ref[...], ones) + bonus_ref[...]

    bonus_ref[...] = bonus_new
    for i, (r_new, y0, m_c, n_c) in enumerate(terms):
        rn_ref[i] = r_new
        y0_ref[i] = y0
        mc_ref[i] = m_c
        nc_ref[i] = n_c

    _decode_step(n, pt_ref, decode_inputs, decode_outputs, scratch[N_WKV_SCRATCH:], steps_per_row=steps_per_row,
                 n_steps=n_decode_steps)


def wkv_prompt(r, k, v, ext, mu_rkv, mu_e, w0, a0, k_k, k_a, r_k, ln_w, ln_b, ww2, wa2, decode_args, *, tb, decode_layer):
    batch, _, seq, _ = r.shape
    nt = seq // tb
    n_blocks = batch * N_PAIRS * nt
    decode = _decode_plan(*decode_args, layer=decode_layer, step_of=lambda n: n)
    assert decode["n_steps"] <= n_blocks + 1

    def where(n):
        return lax.div(n, N_PAIRS * nt), lax.rem(lax.div(n, nt), N_PAIRS), lax.rem(n, nt)

    cur = lambda n: where(jnp.minimum(n, n_blocks - 1))
    prev = lambda n: where(jnp.maximum(n - 1, 0))
    rows = pl.BlockSpec((1, 1, tb, LANES), lambda n, pt: (cur(n)[0], cur(n)[1], cur(n)[2], 0))
    vec = lambda off: pl.BlockSpec((1, LANES), lambda n, pt: (0, off + cur(n)[1]))
    vec_prev = pl.BlockSpec((1, LANES), lambda n, pt: (0, prev(n)[1]))
    lora = pl.BlockSpec((RWKV_EXTRA, LANES), lambda n, pt: (0, cur(n)[1]))
    terms = pltpu.VMEM((tb // CHUNK, PAIR, PAIR), F32)
    grid_spec = pltpu.PrefetchScalarGridSpec(
        num_scalar_prefetch=1,
        grid=(n_blocks + 1,),
        in_specs=[
            rows, rows, rows,
            pl.BlockSpec((tb, RWKV_EXTRA), lambda n, pt: (cur(n)[0] * nt + cur(n)[2], 0)),
            vec(0), vec(N_PAIRS), vec(2 * N_PAIRS),
            pl.BlockSpec((1, RWKV_EXTRA), lambda n, pt: (0, 0)),
            vec(0), vec(0), vec(0), vec(0), vec(0), lora, lora, vec_prev, vec_prev,
        ] + decode["in_specs"],
        out_specs=[
            pl.BlockSpec((tb, LANES), lambda n, pt: (prev(n)[0] * nt + prev(n)[2], prev(n)[1])),
            pl.BlockSpec((1, 1, 2, RWKV_HEAD_DIM, RWKV_HEAD_DIM), lambda n, pt: (prev(n)[0], prev(n)[1], 0, 0, 0)),
        ] + decode["out_specs"],
        scratch_shapes=[
            pltpu.VMEM((1, LANES), F32), pltpu.VMEM((1, LANES), F32), pltpu.VMEM((1, LANES), F32),
            pltpu.VMEM((1, RWKV_EXTRA), F32), pltpu.VMEM((PAIR, PAIR), F32), pltpu.VMEM((tb, LANES), F32),
            terms, terms, terms, terms, pltpu.VMEM((tb, LANES), F32),
        ] + decode["scratch_shapes"],
    )
    return pl.pallas_call(
        functools.partial(_wkv_prompt_kernel, tb=tb, nt=nt, steps_per_row=decode["steps_per_row"],
                          n_decode_steps=decode["n_steps"]),
        grid_spec=grid_spec,
        out_shape=[
            jax.ShapeDtypeStruct((batch * seq, TOK_WIDTH), F32),
            jax.ShapeDtypeStruct((batch, N_PAIRS, 2, RWKV_HEAD_DIM, RWKV_HEAD_DIM), F32),
        ] + decode["out_shape"],
        compiler_params=_params("arbitrary"),
        name="wkv_prompt",
    )(decode_args[0], r, k, v, ext, mu_rkv, mu_rkv, mu_rkv, mu_e, w0, a0, k_k, k_a, r_k, ww2, wa2, ln_w, ln_b,
      *decode["args"])


def _wkv_step_kernel(cur_ref, prev_ref, e_ref, pe_ref, mu_ref, mue_ref, w0_ref, a0_ref, kk_ref, ka_ref, rk_ref,
                     lnw_ref, lnb_ref, ww2_ref, wa2_ref, st_ref, o_ref, stn_ref):
    hd = RWKV_HEAD_DIM
    e_cur = e_ref[0]
    e = e_cur + (pe_ref[0] - e_cur) * mue_ref[...]
    e_rows = jnp.broadcast_to(e, (SUBLANES, RWKV_EXTRA))
    e_tanh = jnp.tanh(e_rows).astype(BF16)
    e_lin = e_rows.astype(BF16)
    eye = lax.broadcasted_iota(jnp.int32, (hd, hd), 0) == lax.broadcasted_iota(jnp.int32, (hd, hd), 1)

    def mixed(i, h):
        cur = cur_ref[0, i, h:h + 1, :]
        return cur + (prev_ref[0, i, h:h + 1, :] - cur) * mu_ref[i, h:h + 1, :]

    for h in range(RWKV_HEADS):
        row = lambda ref: ref[h:h + 1, :]
        r, k, v = mixed(0, h), mixed(1, h), mixed(2, h)
        w = -_softplus(-(row(w0_ref) + jnp.dot(e_tanh, ww2_ref[h], preferred_element_type=F32)[0:1])) - 0.5
        decay = jnp.exp(-jnp.exp(w))
        a = _sigmoid(row(a0_ref) + jnp.dot(e_lin, wa2_ref[h], preferred_element_type=F32)[0:1])
        kk = k * row(kk_ref)
        kk = kk / jnp.maximum(jnp.sqrt(jnp.sum(kk * kk, axis=-1, keepdims=True)), 1e-12)
        k_mod = k * (1.0 + (a - 1.0) * row(ka_ref))
        s = st_ref[0, h]
        sa = jnp.sum(s * (-kk), axis=-1, keepdims=True)
        v_col = jnp.sum(jnp.where(eye, jnp.broadcast_to(v, (hd, hd)), 0.0), axis=-1, keepdims=True)
        s_new = s * decay + sa * (kk * a) + v_col * k_mod
        stn_ref[0, h] = s_new
        y_col = jnp.sum(s_new * r, axis=-1, keepdims=True)
        y = jnp.sum(jnp.where(eye, jnp.broadcast_to(y_col, (hd, hd)), 0.0), axis=0, keepdims=True)
        mean = jnp.mean(y, axis=-1, keepdims=True)
        d = y - mean
        var = jnp.mean(d * d, axis=-1, keepdims=True)
        yn = d * lax.rsqrt(var + GN_EPS) * row(lnw_ref) + row(lnb_ref)
        o_ref[0, h:h + 1, :] = yn + jnp.sum(r * k_mod * row(rk_ref), axis=-1, keepdims=True) * v


def wkv_step(cur, prev, e_cur, e_prev, mu, mu_e, w0, a0, k_k, k_a, r_k, ln_w, ln_b, ww2h, wa2h, state):
    batch = cur.shape[0]
    hshape = (RWKV_HEADS, RWKV_HEAD_DIM)
    per_row = lambda shape: pl.BlockSpec((1,) + shape, lambda b: (b,) + (0,) * len(shape))
    const = lambda shape: pl.BlockSpec(shape, lambda b: (0,) * len(shape))
    return pl.pallas_call(
        _wkv_step_kernel,
        grid=(batch,),
        in_specs=[per_row((3,) + hshape), per_row((3,) + hshape), per_row((1, RWKV_EXTRA)), per_row((1, RWKV_EXTRA)),
                  const((3,) + hshape), const((1, RWKV_EXTRA))] + [const(hshape)] * 7
                 + [const((RWKV_HEADS, RWKV_EXTRA, RWKV_HEAD_DIM))] * 2 + [per_row(hshape + (RWKV_HEAD_DIM,))],
        out_specs=[per_row(hshape), per_row(hshape + (RWKV_HEAD_DIM,))],
        out_shape=[jax.ShapeDtypeStruct((batch,) + hshape, F32),
                   jax.ShapeDtypeStruct((batch,) + hshape + (RWKV_HEAD_DIM,), F32)],
        compiler_params=_params("parallel"),
        name="wkv_step",
    )(cur, prev, e_cur, e_prev, mu, mu_e, w0, a0, k_k, k_a, r_k, ln_w, ln_b, ww2h, wa2h, state)


def _relayout_in_proj(w_in, extra_cols, extra_width):
    return _in_proj_operands(jnp.transpose(w_in).astype(BF16), extra_cols, extra_width)


def _in_proj_operands(w_t, extra_cols, extra_width):
    gm_start = 3 * TOK_WIDTH + extra_cols
    return w_t, w_t[gm_start:], jnp.pad(w_t[3 * TOK_WIDTH:gm_start], ((0, extra_width - extra_cols), (0, 0)))


def _pad_lanes(v, width):
    return jnp.pad(v, (0, width - v.shape[0])).reshape(1, width)


def _memory_kv(mem, g_mem, w_mem):
    kv = norm_proj(mem, g_mem, w_mem, tm=mem.shape[0])
    return kv[:, :MEM_WIDTH], kv[:, MEM_WIDTH:]


def kernel(x_prompt, x_sample, cache_fox_k, cache_fox_v, cache_fox_logf, cache_mem_k, cache_mem_v, state_rwkv_wkv, state_rwkv_shift, page_table, mem_prompt, g_pre, g_post, g_mem, w_mem, w_o, fox_w_in, fox_b_f, rwkv_w_in, rwkv_mu, rwkv_w0, rwkv_w_w2, rwkv_a0, rwkv_w_a2, rwkv_k_k, rwkv_k_a, rwkv_r_k, rwkv_ln_w, rwkv_ln_b):
    bp, seq, _ = x_prompt.shape
    bs = x_sample.shape[0]
    xp = x_prompt.reshape(bp * seq, D_MODEL)
    xs = x_sample.reshape(bs, D_MODEL)
    mem = mem_prompt.reshape(bp * N_MEM, D_MODEL)
    mem_shape = (bp, N_MEM, MEM_HEADS, MEM_HEAD_DIM)

    w_fox = _relayout_in_proj(fox_w_in[0], FOX_HEADS, FOX_EXTRA)
    b_pad = _pad_lanes(fox_b_f[0], LANES)
    mk0, mv0 = _memory_kv(mem, g_mem[0], w_mem[0])

    gm_p, q_p, k_p, v_p, ext_p = norm_proj(xp, g_pre[0], *w_fox, tm=PROJ_ROWS, heads=(bp, seq))
    logf_p, ccol, crow = fox_gate(ext_p, b_pad, batch=bp, seq=seq, tb=GATE_ROWS)
    o_tok, w_rwkv_t, w_o = fox_flash(q_p, k_p, v_p, ccol, crow, tq=FLASH_ROWS,
                                     casts=(jnp.transpose(rwkv_w_in[0]), w_o.reshape(-1, D_MODEL)))
    w_o = w_o.reshape(-1, D_MODEL, D_MODEL)
    o_mem = mem_attn(gm_p.reshape(bp, seq, GM_WIDTH), mk0, mv0, tq=MEM_ATTN_ROWS, q_block=D_MODEL // MEM_WIDTH)
    xp = out_proj(o_tok, o_mem.reshape(bp * seq, MEM_WIDTH), gm_p, xp, w_o, g_post[0], layer=0, tm=OUT_PROJ_ROWS)

    main_s0, ext_s0 = norm_proj(xs, g_pre[0], *w_fox, tm=bs)
    q_s, k_s, v_s = (main_s0[:, off:off + TOK_WIDTH] for off in (TOKA_OFF, TOKB_OFF, TOKC_OFF))
    logf_pool_t = jnp.pad(jnp.transpose(cache_fox_logf[0], (0, 2, 1)), ((0, 0), (0, HEAD_PAD - FOX_HEADS), (0, 0)))
    k_pool, v_pool = (jnp.transpose(c, (0, 1, 3, 2, 4)) for c in (cache_fox_k, cache_fox_v))
    decode_args = (page_table, q_s, k_s, v_s, ext_s0, b_pad, k_pool, v_pool, logf_pool_t)

    head_shape = (FOX_HEADS, FOX_HEAD_DIM)
    fox_k_p = jnp.transpose(k_p, (0, 2, 1, 3))[None]
    fox_v_p = jnp.transpose(v_p, (0, 2, 1, 3))[None]
    fox_f_p = logf_p[:, :FOX_HEADS].reshape(1, bp, seq, FOX_HEADS)
    fox_k_s = k_s.reshape((1, bs, 1) + head_shape)
    fox_v_s = v_s.reshape((1, bs, 1) + head_shape)

    w_rwkv = _in_proj_operands(w_rwkv_t, 2 * LORA, RWKV_EXTRA)
    mk1, mv1 = _memory_kv(mem, g_mem[1], w_mem[1])
    mu_rkv = rwkv_mu[0][:3 * TOK_WIDTH].reshape(1, 3 * TOK_WIDTH)
    mu_e = _pad_lanes(rwkv_mu[0][3 * TOK_WIDTH:], RWKV_EXTRA)
    ww2 = jnp.pad(rwkv_w_w2[0], ((0, RWKV_EXTRA - LORA), (0, 0))).astype(BF16)
    wa2 = jnp.pad(rwkv_w_a2[0], ((LORA, RWKV_EXTRA - 2 * LORA), (0, 0))).astype(BF16)
    vecs = [p[0] for p in (rwkv_w0, rwkv_a0, rwkv_k_k, rwkv_k_a, rwkv_r_k, rwkv_ln_w, rwkv_ln_b)]

    gm_p, r_p, k_p, v_p, ext_p = norm_proj(xp, g_pre[1], *w_rwkv, tm=PROJ_ROWS, heads=(bp, seq))
    o_tok, st_pairs, o_tok_s, logf_s = wkv_prompt(r_p, k_p, v_p, ext_p, mu_rkv, mu_e,
                                                  *[v.reshape(1, TOK_WIDTH) for v in vecs], ww2, wa2, decode_args,
                                                  tb=WKV_ROWS, decode_layer=0)
    o_mem = mem_attn(gm_p.reshape(bp, seq, GM_WIDTH), mk1, mv1, tq=MEM_ATTN_ROWS, q_block=D_MODEL // MEM_WIDTH)
    xp = out_proj(o_tok, o_mem.reshape(bp * seq, MEM_WIDTH), gm_p, xp, w_o, g_post[1], layer=1, tm=OUT_PROJ_ROWS)

    fox_f_s = logf_s[:, :FOX_HEADS].reshape(1, bs, 1, FOX_HEADS)
    o_mem_s = mem_attn(main_s0.reshape(bs, 1, MAIN_WIDTH), cache_mem_k[0].reshape(bs * N_MEM, MEM_WIDTH),
                       cache_mem_v[0].reshape(bs * N_MEM, MEM_WIDTH), tq=1, q_block=QMEM_OFF // MEM_WIDTH)
    xs = out_proj(o_tok_s, o_mem_s.reshape(bs, MEM_WIDTH), main_s0, xs, w_o, g_post[0], layer=0, tm=bs)

    main_s, ext_s = norm_proj(xs, g_pre[1], *w_rwkv, tm=bs)
    hshape = (RWKV_HEADS, RWKV_HEAD_DIM)
    shift_prev = state_rwkv_shift[0][:, 0, :]
    cur = main_s[:, TOKA_OFF:QMEM_OFF].reshape((bs, 3) + hshape)
    prev = shift_prev[:, :3 * TOK_WIDTH].reshape((bs, 3) + hshape)
    e_prev = jnp.pad(shift_prev[:, 3 * TOK_WIDTH:], ((0, 0), (0, RWKV_EXTRA - 2 * LORA))).reshape(bs, 1, RWKV_EXTRA)
    per_head = lambda w: jnp.transpose(w.reshape((RWKV_EXTRA,) + hshape), (1, 0, 2))
    o_tok_s, wkv_s = wkv_step(cur, prev, ext_s.reshape(bs, 1, RWKV_EXTRA), e_prev, mu_rkv.reshape((3,) + hshape), mu_e,
                              *[v.reshape(hshape) for v in vecs], per_head(ww2), per_head(wa2), state_rwkv_wkv[0])
    o_mem_s = mem_attn(main_s.reshape(bs, 1, MAIN_WIDTH), cache_mem_k[1].reshape(bs * N_MEM, MEM_WIDTH),
                       cache_mem_v[1].reshape(bs * N_MEM, MEM_WIDTH), tq=1, q_block=QMEM_OFF // MEM_WIDTH)
    xs = out_proj(o_tok_s.reshape(bs, TOK_WIDTH), o_mem_s.reshape(bs, MEM_WIDTH), main_s, xs, w_o, g_post[1], layer=1, tm=bs)

    wkv_p = st_pairs.reshape((bp,) + hshape + (RWKV_HEAD_DIM,))
    last_p = jnp.concatenate([a[:, :, seq - 1, :].reshape(bp, TOK_WIDTH) for a in (r_p, k_p, v_p)], axis=-1)
    last_e = ext_p.reshape(bp, seq, RWKV_EXTRA)[:, seq - 1, :2 * LORA]
    shift_p = jnp.concatenate([last_p, last_e], axis=-1).reshape(1, bp, 1, 3 * TOK_WIDTH + 2 * LORA)
    shift_s = jnp.concatenate([main_s[:, TOKA_OFF:QMEM_OFF], ext_s[:, :2 * LORA]], axis=-1).reshape(1, bs, 1, 3 * TOK_WIDTH + 2 * LORA)

    return (xp.reshape(bp, seq, D_MODEL), xs.reshape(bs, 1, D_MODEL),
            jnp.stack([mk0.reshape(mem_shape), mk1.reshape(mem_shape)]),
            jnp.stack([mv0.reshape(mem_shape), mv1.reshape(mem_shape)]),
            fox_k_p, fox_v_p, fox_f_p, fox_k_s, fox_v_s, fox_f_s,
            wkv_p[None], shift_p, wkv_s[None], shift_s)
```

```python
import functools
import math

import jax
import jax.numpy as jnp
from jax import lax
from jax.experimental import pallas as pl
from jax.experimental.pallas import tpu as pltpu

F32 = jnp.float32
BF16 = jnp.bfloat16

D_MODEL = 2048
N_MEM = 256
MEM_HEADS = 4
MEM_HEAD_DIM = 128
MEM_WIDTH = MEM_HEADS * MEM_HEAD_DIM
TOK_WIDTH = D_MODEL - MEM_WIDTH
FOX_HEAD_DIM = 128
FOX_HEADS = TOK_WIDTH // FOX_HEAD_DIM
RWKV_HEAD_DIM = 64
RWKV_HEADS = TOK_WIDTH // RWKV_HEAD_DIM
LORA = 96
PAGE_SIZE = 128
RMS_EPS = 1e-6
GN_EPS = 64e-5

LANES = 128
SUBLANES = 8
V7X_VMEM_BYTES = 64 * 1024 * 1024
VMEM_LIMIT_BYTES = V7X_VMEM_BYTES - 8 * 1024 * 1024

GATE_OFF = 0
TOKA_OFF = D_MODEL
TOKB_OFF = TOKA_OFF + TOK_WIDTH
TOKC_OFF = TOKB_OFF + TOK_WIDTH
QMEM_OFF = TOKC_OFF + TOK_WIDTH
MAIN_WIDTH = QMEM_OFF + MEM_WIDTH
FOX_EXTRA = LANES
RWKV_EXTRA = 2 * LANES

HEAD_PAD = 16
CHUNK = 64

PROJ_ROWS = 1024
GATE_ROWS = 256
FLASH_ROWS = 512
MEM_ATTN_ROWS = 1024
OUT_PROJ_ROWS = 256
WKV_ROWS = 8 * CHUNK


def _params(*sem):
    return pltpu.CompilerParams(dimension_semantics=sem, vmem_limit_bytes=VMEM_LIMIT_BYTES)


def _rms_scale(x):
    return lax.rsqrt(jnp.mean(x * x, axis=-1, keepdims=True) + RMS_EPS)


def _sigmoid(x):
    return 1.0 / (1.0 + jnp.exp(-x))


def _softplus(x):
    return jnp.maximum(x, 0.0) + jnp.log(1.0 + jnp.exp(-jnp.abs(x)))


def _split(x, pieces):
    out = []
    for _ in range(pieces - 1):
        p = x.astype(BF16)
        out.append(p)
        x = x - p.astype(F32)
    return out + [x.astype(BF16)]


def _dot_exact_rhs(x, rhs, pieces=3):
    rows = x.shape[0]
    parts = jnp.dot(jnp.concatenate(_split(x, pieces), axis=0), rhs.astype(BF16), preferred_element_type=F32)
    return sum(parts[i * rows:(i + 1) * rows] for i in range(pieces))


def _dot_exact_lhs(lhs, x, pieces=3):
    cols = x.shape[1]
    parts = jnp.dot(lhs.astype(BF16), jnp.concatenate(_split(x, pieces), axis=1), preferred_element_type=F32)
    return sum(parts[:, i * cols:(i + 1) * cols] for i in range(pieces))


TN = 4 * LANES
GATE_BLOCKS = D_MODEL // TN
TOK_BLOCKS = TOK_WIDTH // TN
QMEM_BLOCK = GATE_BLOCKS + 3 * TOK_BLOCKS
GM_WIDTH = D_MODEL + MEM_WIDTH
N_GROUPS128 = TOK_WIDTH // LANES
assert MAIN_WIDTH == (QMEM_BLOCK + 1) * TN and MEM_WIDTH == TN


def _nt_dot(a, b):
    return lax.dot_general(a, b, (((1,), (1,)), ((), ())), preferred_element_type=F32)


def _norm_proj_kernel(x_ref, g_ref, wt_ref, *rest, has_gm, has_extra, head_mode):
    rest = list(rest)
    wg_ref = rest.pop(0) if has_gm else None
    we_ref = rest.pop(0) if has_extra else None
    h_ref = rest.pop()
    oe_ref = rest.pop() if has_extra else None
    j = pl.program_id(1)

    @pl.when(j == 0)
    def _():
        x = x_ref[...]
        h = (x * _rms_scale(x) * g_ref[...]).astype(BF16)
        h_ref[...] = h
        if has_extra:
            oe_ref[...] = _nt_dot(h, we_ref[...])

    if not has_gm:
        rest[0][...] = jnp.dot(h_ref[...], wt_ref[...].astype(BF16), preferred_element_type=F32)
        return
    in_gm = (j < GATE_BLOCKS) | (j == QMEM_BLOCK)

    @pl.when(in_gm)
    def _():
        rest[0][...] = _nt_dot(h_ref[...], wg_ref[...])

    if not head_mode:
        @pl.when(jnp.logical_not(in_gm))
        def _():
            rest[0][...] = _nt_dot(h_ref[...], wt_ref[...])
        return

    for gi, ref in enumerate(rest[1:]):
        first = GATE_BLOCKS + gi * TOK_BLOCKS

        @pl.when((j >= first) & (j < first + TOK_BLOCKS))
        def _(ref=ref):
            res = _nt_dot(h_ref[...], wt_ref[...])
            for hh in range(TN // LANES):
                ref[0, hh] = res[:, hh * LANES:(hh + 1) * LANES]


def norm_proj(x, g, w, w_gm_t=None, w_extra_t=None, *, tm, heads=None):
    m, d = x.shape
    has_gm = w_gm_t is not None
    has_extra = w_extra_t is not None
    n = MAIN_WIDTH if has_gm else w.shape[1]
    assert m % tm == 0 and n % TN == 0 and (has_gm or heads is None)
    gm_col = lambda j: jnp.where(j == QMEM_BLOCK, GATE_BLOCKS, jnp.minimum(j, GATE_BLOCKS - 1))
    gm_row = lambda j: jnp.where(j == QMEM_BLOCK, 0, jnp.minimum(j, GATE_BLOCKS - 1) + MEM_WIDTH // TN)
    in_specs = [pl.BlockSpec((tm, d), lambda i, j: (i, 0)), pl.BlockSpec((1, d), lambda i, j: (0, 0))]
    args = [x, g.reshape(1, d), w]
    if has_gm:
        in_specs.append(pl.BlockSpec((TN, d), lambda i, j: (jnp.clip(j - GATE_BLOCKS, 0, 3 * TOK_BLOCKS - 1), 0)))
        in_specs.append(pl.BlockSpec((TN, d), lambda i, j: (gm_row(j), 0)))
        args.append(w_gm_t)
    else:
        in_specs.append(pl.BlockSpec((d, TN), lambda i, j: (0, j)))
    if has_extra:
        in_specs.append(pl.BlockSpec(w_extra_t.shape, lambda i, j: (0, 0)))
        args.append(w_extra_t)
    if heads is None:
        out_specs = [pl.BlockSpec((tm, TN), lambda i, j: (i, j))]
        out_shape = [jax.ShapeDtypeStruct((m, n), F32)]
    else:
        batch, seq = heads
        assert seq % tm == 0
        per_seq = seq // tm
        out_specs = [pl.BlockSpec((tm, TN), lambda i, j: (i, gm_col(j)))]
        out_shape = [jax.ShapeDtypeStruct((m, GM_WIDTH), F32)]
        for gi in range(3):
            first = GATE_BLOCKS + gi * TOK_BLOCKS
            out_specs.append(pl.BlockSpec(
                (1, TN // LANES, tm, LANES),
                lambda i, j, first=first: (i // per_seq, jnp.clip(j - first, 0, TOK_BLOCKS - 1), i % per_seq, 0)))
            out_shape.append(jax.ShapeDtypeStruct((batch, N_GROUPS128, seq, LANES), F32))
    if has_extra:
        e = w_extra_t.shape[0]
        out_specs.append(pl.BlockSpec((tm, e), lambda i, j: (i, 0)))
        out_shape.append(jax.ShapeDtypeStruct((m, e), F32))
    outs = pl.pallas_call(
        functools.partial(_norm_proj_kernel, has_gm=has_gm, has_extra=has_extra, head_mode=heads is not None),
        grid=(m // tm, n // TN),
        in_specs=in_specs,
        out_specs=out_specs,
        out_shape=out_shape,
        scratch_shapes=[pltpu.VMEM((tm, d), BF16)],
        compiler_params=_params("parallel", "arbitrary"),
        name="norm_proj",
    )(*args)
    return outs if len(outs) > 1 else outs[0]


def _fox_gate_kernel(f_ref, b_ref, logf_ref, ccol_ref, crow_ref, carry_ref, *, tb):
    @pl.when(pl.program_id(1) == 0)
    def _():
        carry_ref[...] = jnp.zeros_like(carry_ref)

    logf = -_softplus(-(f_ref[...] + b_ref[...]))
    logf_ref[...] = logf
    row = lax.broadcasted_iota(jnp.int32, (tb, tb), 0)
    col = lax.broadcasted_iota(jnp.int32, (tb, tb), 1)
    c = _dot_exact_lhs(jnp.where(col <= row, 1.0, 0.0), logf) + carry_ref[...]
    ccol_ref[...] = c
    crow_ref[0] = c.T[:HEAD_PAD]
    carry_ref[...] = c[tb - 1:tb, :]


def fox_gate(f_ext, b_pad, *, batch, seq, tb):
    nt = seq // tb
    row_spec = pl.BlockSpec((tb, LANES), lambda b, t: (b * nt + t, 0))
    return pl.pallas_call(
        functools.partial(_fox_gate_kernel, tb=tb),
        grid=(batch, nt),
        in_specs=[row_spec, pl.BlockSpec((1, LANES), lambda b, t: (0, 0))],
        out_specs=[row_spec, row_spec, pl.BlockSpec((1, HEAD_PAD, tb), lambda b, t: (b, 0, t))],
        out_shape=[
            jax.ShapeDtypeStruct((batch * seq, LANES), F32),
            jax.ShapeDtypeStruct((batch * seq, LANES), F32),
            jax.ShapeDtypeStruct((batch, HEAD_PAD, seq), F32),
        ],
        scratch_shapes=[pltpu.VMEM((1, LANES), F32)],
        compiler_params=_params("parallel", "arbitrary"),
        name="fox_gate",
    )(f_ext, b_pad)


_MASKED = -1e30
LOG2E = 1.4426950408889634


def _flash_tile(h, qi, q_ref, k_ref, v_ref, ccol_ref, crow_ref, o_ref, kb_ref, vb_ref, *, tq):
    @pl.when(qi == 0)
    def _():
        kb_ref[...] = k_ref[0, 0].astype(BF16)
        vb_ref[...] = v_ref[0, 0].astype(BF16)

    q = (q_ref[0, 0] * (FOX_HEAD_DIM ** -0.5 * LOG2E)).astype(BF16)
    lane = lax.broadcasted_iota(jnp.int32, (tq, LANES), 1)
    cq = jnp.sum(jnp.where(lane == h, ccol_ref[...], 0.0), axis=-1, keepdims=True) * LOG2E

    def step(c, carry, diagonal):
        m, l, acc = carry
        start = pl.multiple_of(c * tq, tq)
        kc = kb_ref[pl.ds(start, tq), :]
        vc = vb_ref[pl.ds(start, tq), :]
        ck = crow_ref[0, 0, :, pl.ds(start, tq)] * LOG2E
        s = lax.dot_general(q, kc, (((1,), (1,)), ((), ())), preferred_element_type=F32) - ck
        if diagonal:
            row = lax.broadcasted_iota(jnp.int32, (tq, tq), 0)
            col = lax.broadcasted_iota(jnp.int32, (tq, tq), 1)
            s = jnp.where(col <= row, s, _MASKED)
        m_new = jnp.maximum(m, jnp.max(s, axis=-1, keepdims=True) + cq)
        p = jnp.exp2(s - (m_new - cq))
        alpha = jnp.exp2(m - m_new)
        l = alpha * l + jnp.sum(p, axis=-1, keepdims=True)
        acc = alpha * acc + jnp.dot(p.astype(BF16), vc, preferred_element_type=F32)
        return m_new, l, acc

    init = (jnp.full((tq, 1), _MASKED, F32), jnp.zeros((tq, 1), F32), jnp.zeros((tq, FOX_HEAD_DIM), F32))
    carry = lax.fori_loop(0, qi, lambda c, cr: step(c, cr, False), init)
    _, l, acc = step(qi, carry, True)
    o_ref[...] = acc / l


PAGES_PER_STEP = 8


def _head_mask():
    row = lax.broadcasted_iota(jnp.int32, (HEAD_PAD, TOK_WIDTH), 0)
    col = lax.broadcasted_iota(jnp.int32, (HEAD_PAD, TOK_WIDTH), 1)
    lo = row * FOX_HEAD_DIM
    return (col >= lo) & (col < lo + FOX_HEAD_DIM)


def _decode_pages(b, j, last, pt_ref, q_ref, kn_ref, vn_ref, f_ref, bf_ref, lf_ref, k_refs, v_refs,
                  o_ref, lfn_ref, qbd_ref, m_ref, l_ref, acc_ref, carry_ref):
    pps = PAGES_PER_STEP
    n_pages = pt_ref.shape[1]
    scale = FOX_HEAD_DIM ** -0.5
    hmask = _head_mask()

    @pl.when(j == 0)
    def _():
        q = q_ref[pl.ds(b, 1), :]
        q_bd = jnp.where(hmask, jnp.broadcast_to(q, (HEAD_PAD, TOK_WIDTH)), 0.0)
        qbd_ref[...] = q_bd.astype(BF16)
        logf_new = -_softplus(-(f_ref[pl.ds(b, 1), :] + bf_ref[...]))
        lfn_ref[pl.ds(b, 1), :] = logf_new
        eye = lax.broadcasted_iota(jnp.int32, (HEAD_PAD, LANES), 0) == lax.broadcasted_iota(jnp.int32, (HEAD_PAD, LANES), 1)
        carry_ref[...] = jnp.sum(jnp.where(eye, jnp.broadcast_to(logf_new, (HEAD_PAD, LANES)), 0.0), axis=-1, keepdims=True)
        m_ref[...] = jnp.sum(q_bd * kn_ref[pl.ds(b, 1), :], axis=-1, keepdims=True) * scale
        l_ref[...] = jnp.ones_like(l_ref)
        acc_ref[...] = jnp.broadcast_to(vn_ref[pl.ds(b, 1), :], (HEAD_PAD, TOK_WIDTH))

    def page_rows(ref):
        return jnp.concatenate([ref[0, 0, h].astype(BF16) for h in range(FOX_HEADS)], axis=1)

    lfs = [lf_ref[pt_ref[b, n_pages - 1 - j * pps - i]] for i in range(pps)]
    src = lax.broadcasted_iota(jnp.int32, (PAGE_SIZE, PAGE_SIZE), 0)
    dst = lax.broadcasted_iota(jnp.int32, (PAGE_SIZE, PAGE_SIZE), 1)
    inside = _dot_exact_rhs(jnp.concatenate(lfs, axis=0), jnp.where(src > dst, 1.0, 0.0))
    carries = [carry_ref[...]]
    for lf in lfs:
        carries.append(carries[-1] + jnp.sum(lf, axis=-1, keepdims=True))
    carry_ref[...] = carries[-1]

    q_bd = qbd_ref[...]
    scores = []
    for i in range(pps):
        s = lax.dot_general(q_bd, page_rows(k_refs[i]), (((1,), (1,)), ((), ())), preferred_element_type=F32)
        scores.append(s * scale + (inside[i * HEAD_PAD:(i + 1) * HEAD_PAD] + carries[i]))
    m_old = m_ref[...]
    m_new = m_old
    for s in scores:
        m_new = jnp.maximum(m_new, jnp.max(s, axis=-1, keepdims=True))
    alpha = jnp.exp(m_old - m_new)
    l_new = alpha * l_ref[...]
    acc = alpha * acc_ref[...]
    for i, s in enumerate(scores):
        p = jnp.exp(s - m_new)
        l_new = l_new + jnp.sum(p, axis=-1, keepdims=True)
        acc = acc + jnp.dot(p.astype(BF16), page_rows(v_refs[i]), preferred_element_type=F32)
    m_ref[...] = m_new
    l_ref[...] = l_new
    acc_ref[...] = acc

    @pl.when(last)
    def _():
        out = jnp.where(hmask, acc / l_new, 0.0)
        o_ref[pl.ds(b, 1), :] = jnp.sum(out, axis=0, keepdims=True)


def _decode_plan(page_table, q_s, k_new, v_new, f_ext, b_pad, k_pool, v_pool, logf_pool_t, *, layer, step_of):
    rows, n_pages = page_table.shape
    pps = PAGES_PER_STEP
    assert n_pages % pps == 0
    steps_per_row = n_pages // pps
    n_steps = rows * steps_per_row

    def page_id(i):
        def index(*idx):
            pt = idx[-1]
            d = jnp.minimum(step_of(*idx[:-1]), n_steps - 1)
            return pt[lax.div(d, steps_per_row), n_pages - 1 - lax.rem(d, steps_per_row) * pps - i]
        return index

    whole = lambda shape: pl.BlockSpec(shape, lambda *idx: (0,) * len(shape))
    page_spec = lambda i: pl.BlockSpec((1, 1, FOX_HEADS, PAGE_SIZE, FOX_HEAD_DIM),
                                       lambda *idx: (layer, page_id(i)(*idx), 0, 0, 0))
    return dict(
        in_specs=[whole((rows, TOK_WIDTH))] * 3 + [whole((rows, LANES)), whole((1, LANES)), whole(logf_pool_t.shape)]
                 + [page_spec(i) for i in range(pps)] * 2,
        args=[q_s, k_new, v_new, f_ext, b_pad, logf_pool_t] + [k_pool] * pps + [v_pool] * pps,
        out_specs=[whole((rows, TOK_WIDTH)), whole((rows, LANES))],
        out_shape=[jax.ShapeDtypeStruct((rows, TOK_WIDTH), F32), jax.ShapeDtypeStruct((rows, LANES), F32)],
        scratch_shapes=[
            pltpu.VMEM((HEAD_PAD, TOK_WIDTH), BF16),
            pltpu.VMEM((HEAD_PAD, 1), F32),
            pltpu.VMEM((HEAD_PAD, 1), F32),
            pltpu.VMEM((HEAD_PAD, TOK_WIDTH), F32),
            pltpu.VMEM((HEAD_PAD, 1), F32),
        ],
        steps_per_row=steps_per_row, n_steps=n_steps,
    )


N_DECODE_INPUTS = 6 + 2 * PAGES_PER_STEP
N_DECODE_OUTPUTS = 2
N_DECODE_SCRATCH = 5


def _decode_step(n, pt_ref, inputs, outputs, scratch, *, steps_per_row, n_steps):
    pps = PAGES_PER_STEP
    q_ref, kn_ref, vn_ref, f_ref, bf_ref, lf_ref = inputs[:6]
    k_refs, v_refs = inputs[6:6 + pps], inputs[6 + pps:]

    @pl.when(n < n_steps)
    def _():
        j = lax.rem(n, steps_per_row)
        _decode_pages(lax.div(n, steps_per_row), j, j == steps_per_row - 1, pt_ref, q_ref, kn_ref, vn_ref, f_ref,
                      bf_ref, lf_ref, k_refs, v_refs, *outputs, *scratch)


N_FLASH_INPUTS = 5


def _fox_flash_kernel(*refs, tq, nq, cast_blocks):
    n_casts = len(cast_blocks)
    cast_in = refs[N_FLASH_INPUTS:N_FLASH_INPUTS + n_casts]
    o_ref = refs[N_FLASH_INPUTS + n_casts]
    cast_out = refs[N_FLASH_INPUTS + n_casts + 1:N_FLASH_INPUTS + 2 * n_casts + 1]
    kb_ref, vb_ref = refs[N_FLASH_INPUTS + 2 * n_casts + 1:]
    _flash_tile(pl.program_id(1), pl.program_id(2), *refs[:N_FLASH_INPUTS], o_ref, kb_ref, vb_ref, tq=tq)
    step = (pl.program_id(0) * FOX_HEADS + pl.program_id(1)) * nq + pl.program_id(2)
    for src, dst, blocks in zip(cast_in, cast_out, cast_blocks):
        @pl.when(step < blocks)
        def _(src=src, dst=dst):
            dst[...] = src[...].astype(BF16)


def fox_flash(q, k, v, ccol, crow, *, tq, casts=()):
    batch, _, seq, _ = q.shape
    nq = seq // tq
    n_steps = batch * FOX_HEADS * nq
    crow4 = crow.reshape(batch, HEAD_PAD, 1, seq)
    kv_spec = pl.BlockSpec((1, 1, seq, FOX_HEAD_DIM), lambda b, h, i: (b, h, 0, 0))
    cast_specs, cast_shapes, cast_blocks = [], [], []
    for a in casts:
        rows, width = a.shape
        rb = next(r for r in range(2 * SUBLANES, rows + 1, 2 * SUBLANES) if rows % r == 0 and rows // r <= n_steps)
        blocks = rows // rb
        cast_specs.append(pl.BlockSpec(
            (rb, width), lambda b, h, i, blocks=blocks: (jnp.minimum((b * FOX_HEADS + h) * nq + i, blocks - 1), 0)))
        cast_shapes.append(jax.ShapeDtypeStruct(a.shape, BF16))
        cast_blocks.append(blocks)
    outs = pl.pallas_call(
        functools.partial(_fox_flash_kernel, tq=tq, nq=nq, cast_blocks=tuple(cast_blocks)),
        grid=(batch, FOX_HEADS, nq),
        in_specs=[
            pl.BlockSpec((1, 1, tq, FOX_HEAD_DIM), lambda b, h, i: (b, h, i, 0)),
            kv_spec, kv_spec,
            pl.BlockSpec((tq, LANES), lambda b, h, i: (b * nq + i, 0)),
            pl.BlockSpec((1, 1, 1, seq), lambda b, h, i: (b, h, 0, 0)),
        ] + cast_specs,
        out_specs=[pl.BlockSpec((tq, LANES), lambda b, h, i: (b * nq + i, h))] + cast_specs,
        out_shape=[jax.ShapeDtypeStruct((batch * seq, TOK_WIDTH), F32)] + cast_shapes,
        scratch_shapes=[pltpu.VMEM((seq, LANES), BF16), pltpu.VMEM((seq, LANES), BF16)],
        compiler_params=_params("arbitrary", "arbitrary", "arbitrary"),
        name="fox_flash",
    )(q, k, v, ccol, crow4, *casts)
    return outs[0] if not casts else outs


def _mem_attn_kernel(q_ref, gate_ref, k_ref, v_ref, o_ref, *, tq):
    rows = max(tq, SUBLANES)
    scale = MEM_HEAD_DIM ** -0.5
    gate = gate_ref[0]
    silu_gate = gate * _sigmoid(gate)
    for h in range(MEM_HEADS):
        sl = slice(h * MEM_HEAD_DIM, (h + 1) * MEM_HEAD_DIM)
        qh = jnp.broadcast_to(q_ref[0, :, sl], (rows, MEM_HEAD_DIM)).astype(BF16)
        kh = k_ref[:, sl].astype(BF16)
        vh = v_ref[:, sl].astype(BF16)
        s = lax.dot_general(qh, kh, (((1,), (1,)), ((), ())), preferred_element_type=F32) * scale
        e = jnp.exp(s - jnp.max(s, axis=-1, keepdims=True))
        oh = jnp.dot(e.astype(BF16), vh, preferred_element_type=F32) / jnp.sum(e, axis=-1, keepdims=True)
        o_ref[0, :, sl] = (oh[:tq] * silu_gate[:, sl]).astype(BF16)


def mem_attn(main3, mem_k, mem_v, *, tq, q_block):
    batch, seq, _ = main3.shape
    kv_spec = pl.BlockSpec((N_MEM, MEM_WIDTH), lambda b, i: (b, 0))
    return pl.pallas_call(
        functools.partial(_mem_attn_kernel, tq=tq),
        grid=(batch, seq // tq),
        in_specs=[pl.BlockSpec((1, tq, MEM_WIDTH), lambda b, i: (b, i, q_block)),
                  pl.BlockSpec((1, tq, MEM_WIDTH), lambda b, i: (b, i, TOK_WIDTH // MEM_WIDTH)), kv_spec, kv_spec],
        out_specs=pl.BlockSpec((1, tq, MEM_WIDTH), lambda b, i: (b, i, 0)),
        out_shape=jax.ShapeDtypeStruct((batch, seq, MEM_WIDTH), BF16),
        compiler_params=_params("parallel", "parallel"),
        name="mem_attn",
    )(main3, main3, mem_k, mem_v)


def _out_proj_kernel(ot_ref, om_ref, gate_ref, x_ref, w_ref, gp_ref, y_ref):
    gate = gate_ref[...]
    sg = gate * _sigmoid(gate)
    a = (ot_ref[...] * sg).astype(BF16)
    y = (jnp.dot(a, w_ref[0, :TOK_WIDTH], preferred_element_type=F32)
         + jnp.dot(om_ref[...], w_ref[0, TOK_WIDTH:], preferred_element_type=F32))
    y_ref[...] = x_ref[...] + y * _rms_scale(y) * gp_ref[...]


def out_proj(o_tok, o_mem, main, x, w_o, g_post, *, layer, tm):
    m = x.shape[0]
    row = lambda width: pl.BlockSpec((tm, width), lambda i: (i, 0))
    return pl.pallas_call(
        _out_proj_kernel,
        grid=(m // tm,),
        in_specs=[row(TOK_WIDTH), row(MEM_WIDTH), row(TOK_WIDTH), row(D_MODEL),
                  pl.BlockSpec((1, D_MODEL, D_MODEL), lambda i: (layer, 0, 0)),
                  pl.BlockSpec((1, D_MODEL), lambda i: (0, 0))],
        out_specs=row(D_MODEL),
        out_shape=jax.ShapeDtypeStruct((m, D_MODEL), F32),
        compiler_params=_params("parallel"),
        name="out_proj",
    )(o_tok, o_mem, main, x, w_o, g_post.reshape(1, D_MODEL))


PAIR = 2 * RWKV_HEAD_DIM
assert PAIR == LANES
N_PAIRS = RWKV_HEADS // 2


def _pair_ones():
    row = lax.broadcasted_iota(jnp.int32, (PAIR, PAIR), 0) // RWKV_HEAD_DIM
    col = lax.broadcasted_iota(jnp.int32, (PAIR, PAIR), 1) // RWKV_HEAD_DIM
    return jnp.where(row == col, 1.0, 0.0).astype(F32)


def _head_sum(x, ones):
    return _dot_exact_rhs(x, ones, pieces=2)


def _stack(x):
    lane = lax.broadcasted_iota(jnp.int32, x.shape, 1)
    first = lane < RWKV_HEAD_DIM
    return jnp.concatenate([jnp.where(first, x, 0.0), jnp.where(first, 0.0, x)], axis=0)


def _bdot(a, b):
    return jnp.dot(a.astype(BF16), b.astype(BF16), preferred_element_type=F32)


def _wkv_chunk_terms(r, k, v, a_vec, b_vec, lw, between_stages=lambda: None):
    c = CHUNK
    chunks = [slice(c0, c0 + c) for c0 in range(0, r.shape[0], c)]
    row = lax.broadcasted_iota(jnp.int32, (c, c), 0)
    col = lax.broadcasted_iota(jnp.int32, (c, c), 1)
    tril = jnp.where(col <= row, 1.0, 0.0)
    gi = lax.broadcasted_iota(jnp.int32, (4 * c, 4 * c), 0)
    gj = lax.broadcasted_iota(jnp.int32, (4 * c, 4 * c), 1)
    keep = (gj & (c - 1)) < (gi & (c - 1)) + gi // (2 * c)
    eye = jnp.where(lax.broadcasted_iota(jnp.int32, (2 * c, 2 * c), 0) == lax.broadcasted_iota(jnp.int32, (2 * c, 2 * c), 1),
                    1.0, 0.0).astype(F32)

    cum = [_dot_exact_lhs(tril, lw[s]) for s in chunks]
    between_stages()
    last = [x[c - 1:c, :] for x in cum]
    a_t = [_stack(a_vec[s] * jnp.exp(cu - lw[s])) for s, cu in zip(chunks, cum)]
    r_t = [_stack(r[s] * jnp.exp(cu)) for s, cu in zip(chunks, cum)]
    v_s = [_stack(v[s]) for s in chunks]
    xs = [jnp.concatenate([a, rr], axis=0).astype(BF16) for a, rr in zip(a_t, r_t)]
    ys = [jnp.concatenate([_stack(b_vec[s] * jnp.exp(-cu)), _stack(k[s] * jnp.exp(-cu))], axis=0).astype(BF16)
          for s, cu in zip(chunks, cum)]
    g = [jnp.where(keep, lax.dot_general(x, y, (((1,), (1,)), ((), ())), preferred_element_type=F32), 0.0)
         for x, y in zip(xs, ys)]
    between_stages()
    l_ab = [x[:2 * c, :2 * c] for x in g]
    l_ak = [x[:2 * c, 2 * c:] for x in g]
    l_rb = [x[2 * c:, :2 * c] for x in g]
    l_rk = [x[2 * c:, 2 * c:] for x in g]
    t_inv = [eye + x for x in l_ab]
    l_pow = l_ab
    for _ in range(int(math.log2(c)) - 1):
        l_pow = [_bdot(x, x) for x in l_pow]
        between_stages()
        t_inv = [t + _bdot(p, t) for p, t in zip(l_pow, t_inv)]
        between_stages()
    akv = [_bdot(x, y) for x, y in zip(l_ak, v_s)]
    ta = [_bdot(t, jnp.concatenate([a, u], axis=1)) for t, a, u in zip(t_inv, a_t, akv)]
    between_stages()
    rbx = [_bdot(x, y) for x, y in zip(l_rb, ta)]
    rkv = [_bdot(x, y) for x, y in zip(l_rk, v_s)]
    between_stages()
    r_new = [x + y[:, :PAIR] for x, y in zip(r_t, rbx)]
    y0 = [x[:, PAIR:] + y for x, y in zip(rbx, rkv)]
    e_rem = [jnp.exp(la - cu) for la, cu in zip(last, cum)]
    mn = [_bdot(_stack(b_vec[s] * e).T, t) for s, e, t in zip(chunks, e_rem, ta)]
    kv = [_bdot(_stack(k[s] * e).T, vv) for s, e, vv in zip(chunks, e_rem, v_s)]
    m_c = [x[:, :PAIR] + eye * jnp.exp(la) for x, la in zip(mn, last)]
    n_c = [x[:, PAIR:] + y for x, y in zip(mn, kv)]
    return list(zip(r_new, y0, m_c, n_c))


def _token_shift(cur, prev_row, mu):
    rolled = pltpu.roll(cur, 1, 0)
    first = lax.broadcasted_iota(jnp.int32, cur.shape, 0) == 0
    prev = jnp.where(first, prev_row, rolled)
    return cur + (prev - cur) * mu


def _rwkv_transforms(r, k, v, e, w0, a0, k_k, k_a, ww2, wa2, ones):
    w = -_softplus(-(w0 + jnp.dot(jnp.tanh(e).astype(BF16), ww2, preferred_element_type=F32))) - 0.5
    lw = -jnp.exp(w)
    a = _sigmoid(a0 + jnp.dot(e.astype(BF16), wa2, preferred_element_type=F32))
    kk = k * k_k
    kk = kk / jnp.maximum(jnp.sqrt(_head_sum(kk * kk, ones)), 1e-12)
    k_mod = k * (1.0 + (a - 1.0) * k_a)
    return lw, a, kk, k_mod


def _group_norm(y, ln_w, ln_b, ones):
    mean = _head_sum(y, ones) * (1.0 / RWKV_HEAD_DIM)
    d = y - mean
    var = _head_sum(d * d, ones) * (1.0 / RWKV_HEAD_DIM)
    return d * lax.rsqrt(var + GN_EPS) * ln_w + ln_b


N_WKV_INPUTS = 17
N_WKV_SCRATCH = 11


def _wkv_prompt_kernel(pt_ref, *refs, tb, nt, steps_per_row, n_decode_steps):
    refs = list(refs)
    (r_ref, k_ref, v_ref, e_ref, mur_ref, muk_ref, muv_ref, mue_ref, w0_ref, a0_ref,
     kk_ref, ka_ref, rk_ref, ww2_ref, wa2_ref, lnw_ref, lnb_ref) = refs[:N_WKV_INPUTS]
    decode_inputs = refs[N_WKV_INPUTS:N_WKV_INPUTS + N_DECODE_INPUTS]
    o_ref, st_ref, *decode_outputs = refs[N_WKV_INPUTS + N_DECODE_INPUTS:N_WKV_INPUTS + N_DECODE_INPUTS + 2 + N_DECODE_OUTPUTS]
    scratch = refs[N_WKV_INPUTS + N_DECODE_INPUTS + 2 + N_DECODE_OUTPUTS:]
    (pr_ref, pk_ref, pv_ref, pe_ref, s_ref, y_ref, rn_ref, y0_ref, mc_ref, nc_ref, bonus_ref) = scratch[:N_WKV_SCRATCH]
    n = pl.program_id(0)
    nc = tb // CHUNK
    ones = _pair_ones()

    @pl.when(n == 0)
    def _():
        for ref in (rn_ref, y0_ref, mc_ref, nc_ref, bonus_ref, s_ref):
            ref[...] = jnp.zeros_like(ref)

    @pl.when(lax.rem(n, nt) == 0)
    def _():
        for ref in (pr_ref, pk_ref, pv_ref, pe_ref):
            ref[...] = jnp.zeros_like(ref)

    prev_starts_sequence = lax.rem(n - 1, nt) == 0
    state = [jnp.where(prev_starts_sequence, 0.0, s_ref[...])]
    pending = list(range(nc))

    def recurrence_step():
        if not pending:
            return
        i = pending.pop(0)
        s = state[0]
        ys = _bdot(rn_ref[i], s) + y0_ref[i]
        y_ref[i * CHUNK:(i + 1) * CHUNK, :] = ys[:CHUNK] + ys[CHUNK:]
        state[0] = _bdot(mc_ref[i], s) + nc_ref[i]

    r_cur, k_cur, v_cur, e_cur = r_ref[0, 0], k_ref[0, 0], v_ref[0, 0], e_ref[...]
    r = _token_shift(r_cur, pr_ref[...], mur_ref[...])
    k = _token_shift(k_cur, pk_ref[...], muk_ref[...])
    v = _token_shift(v_cur, pv_ref[...], muv_ref[...])
    e = _token_shift(e_cur, pe_ref[...], mue_ref[...])
    for cur, prev in ((r_cur, pr_ref), (k_cur, pk_ref), (v_cur, pv_ref), (e_cur, pe_ref)):
        prev[...] = cur[tb - 1:tb, :]
    recurrence_step()
    lw, a, kk, k_mod = _rwkv_transforms(r, k, v, e, w0_ref[...], a0_ref[...], kk_ref[...], ka_ref[...],
                                        ww2_ref[...], wa2_ref[...], ones)
    recurrence_step()
    bonus_new = _head_sum(r * k_mod * rk_ref[...], ones) * v
    terms = _wkv_chunk_terms(r, k_mod, v, -kk, kk * a, lw, between_stages=recurrence_step)
    while pending:
        recurrence_step()
    s_ref[...] = state[0]
    state_t = state[0].T
    for hh in range(2):
        block = slice(hh * RWKV_HEAD_DIM, (hh + 1) * RWKV_HEAD_DIM)
        st_ref[0, 0, hh] = state_t[block, block]
    o_ref[...] = _group_norm(y_ref[...], lnw_ref[...], lnb_ref[...], ones) + bonus_ref[...]

    bonus_ref[...] = bonus_new
    for i, (r_new, y0, m_c, n_c) in enumerate(terms):
        rn_ref[i] = r_new
        y0_ref[i] = y0
        mc_ref[i] = m_c
        nc_ref[i] = n_c

    _decode_step(n, pt_ref, decode_inputs, decode_outputs, scratch[N_WKV_SCRATCH:], steps_per_row=steps_per_row,
                 n_steps=n_decode_steps)


def wkv_prompt(r, k, v, ext, mu_rkv, mu_e, w0, a0, k_k, k_a, r_k, ln_w, ln_b, ww2, wa2, decode_args, *, tb, decode_layer):
    batch, _, seq, _ = r.shape
    nt = seq // tb
    n_blocks = batch * N_PAIRS * nt
    decode = _decode_plan(*decode_args, layer=decode_layer, step_of=lambda n: n)
    assert decode["n_steps"] <= n_blocks + 1

    def where(n):
        return lax.div(n, N_PAIRS * nt), lax.rem(lax.div(n, nt), N_PAIRS), lax.rem(n, nt)

    cur = lambda n: where(jnp.minimum(n, n_blocks - 1))
    prev = lambda n: where(jnp.maximum(n - 1, 0))
    rows = pl.BlockSpec((1, 1, tb, LANES), lambda n, pt: (cur(n)[0], cur(n)[1], cur(n)[2], 0))
    vec = lambda off: pl.BlockSpec((1, LANES), lambda n, pt: (0, off + cur(n)[1]))
    vec_prev = pl.BlockSpec((1, LANES), lambda n, pt: (0, prev(n)[1]))
    lora = pl.BlockSpec((RWKV_EXTRA, LANES), lambda n, pt: (0, cur(n)[1]))
    terms = pltpu.VMEM((tb // CHUNK, PAIR, PAIR), F32)
    grid_spec = pltpu.PrefetchScalarGridSpec(
        num_scalar_prefetch=1,
        grid=(n_blocks + 1,),
        in_specs=[
            rows, rows, rows,
            pl.BlockSpec((tb, RWKV_EXTRA), lambda n, pt: (cur(n)[0] * nt + cur(n)[2], 0)),
            vec(0), vec(N_PAIRS), vec(2 * N_PAIRS),
            pl.BlockSpec((1, RWKV_EXTRA), lambda n, pt: (0, 0)),
            vec(0), vec(0), vec(0), vec(0), vec(0), lora, lora, vec_prev, vec_prev,
        ] + decode["in_specs"],
        out_specs=[
            pl.BlockSpec((tb, LANES), lambda n, pt: (prev(n)[0] * nt + prev(n)[2], prev(n)[1])),
            pl.BlockSpec((1, 1, 2, RWKV_HEAD_DIM, RWKV_HEAD_DIM), lambda n, pt: (prev(n)[0], prev(n)[1], 0, 0, 0)),
        ] + decode["out_specs"],
        scratch_shapes=[
            pltpu.VMEM((1, LANES), F32), pltpu.VMEM((1, LANES), F32), pltpu.VMEM((1, LANES), F32),
            pltpu.VMEM((1, RWKV_EXTRA), F32), pltpu.VMEM((PAIR, PAIR), F32), pltpu.VMEM((tb, LANES), F32),
            terms, terms, terms, terms, pltpu.VMEM((tb, LANES), F32),
        ] + decode["scratch_shapes"],
    )
    return pl.pallas_call(
        functools.partial(_wkv_prompt_kernel, tb=tb, nt=nt, steps_per_row=decode["steps_per_row"],
                          n_decode_steps=decode["n_steps"]),
        grid_spec=grid_spec,
        out_shape=[
            jax.ShapeDtypeStruct((batch * seq, TOK_WIDTH), F32),
            jax.ShapeDtypeStruct((batch, N_PAIRS, 2, RWKV_HEAD_DIM, RWKV_HEAD_DIM), F32),
        ] + decode["out_shape"],
        compiler_params=_params("arbitrary"),
        name="wkv_prompt",
    )(decode_args[0], r, k, v, ext, mu_rkv, mu_rkv, mu_rkv, mu_e, w0, a0, k_k, k_a, r_k, ww2, wa2, ln_w, ln_b,
      *decode["args"])


def _wkv_step_kernel(cur_ref, prev_ref, e_ref, pe_ref, mu_ref, mue_ref, w0_ref, a0_ref, kk_ref, ka_ref, rk_ref,
                     lnw_ref, lnb_ref, ww2_ref, wa2_ref, st_ref, o_ref, stn_ref):
    hd = RWKV_HEAD_DIM
    e_cur = e_ref[0]
    e = e_cur + (pe_ref[0] - e_cur) * mue_ref[...]
    e_rows = jnp.broadcast_to(e, (SUBLANES, RWKV_EXTRA))
    e_tanh = jnp.tanh(e_rows).astype(BF16)
    e_lin = e_rows.astype(BF16)
    eye = lax.broadcasted_iota(jnp.int32, (hd, hd), 0) == lax.broadcasted_iota(jnp.int32, (hd, hd), 1)

    def mixed(i, h):
        cur = cur_ref[0, i, h:h + 1, :]
        return cur + (prev_ref[0, i, h:h + 1, :] - cur) * mu_ref[i, h:h + 1, :]

    for h in range(RWKV_HEADS):
        row = lambda ref: ref[h:h + 1, :]
        r, k, v = mixed(0, h), mixed(1, h), mixed(2, h)
        w = -_softplus(-(row(w0_ref) + jnp.dot(e_tanh, ww2_ref[h], preferred_element_type=F32)[0:1])) - 0.5
        decay = jnp.exp(-jnp.exp(w))
        a = _sigmoid(row(a0_ref) + jnp.dot(e_lin, wa2_ref[h], preferred_element_type=F32)[0:1])
        kk = k * row(kk_ref)
        kk = kk / jnp.maximum(jnp.sqrt(jnp.sum(kk * kk, axis=-1, keepdims=True)), 1e-12)
        k_mod = k * (1.0 + (a - 1.0) * row(ka_ref))
        s = st_ref[0, h]
        sa = jnp.sum(s * (-kk), axis=-1, keepdims=True)
        v_col = jnp.sum(jnp.where(eye, jnp.broadcast_to(v, (hd, hd)), 0.0), axis=-1, keepdims=True)
        s_new = s * decay + sa * (kk * a) + v_col * k_mod
        stn_ref[0, h] = s_new
        y_col = jnp.sum(s_new * r, axis=-1, keepdims=True)
        y = jnp.sum(jnp.where(eye, jnp.broadcast_to(y_col, (hd, hd)), 0.0), axis=0, keepdims=True)
        mean = jnp.mean(y, axis=-1, keepdims=True)
        d = y - mean
        var = jnp.mean(d * d, axis=-1, keepdims=True)
        yn = d * lax.rsqrt(var + GN_EPS) * row(lnw_ref) + row(lnb_ref)
        o_ref[0, h:h + 1, :] = yn + jnp.sum(r * k_mod * row(rk_ref), axis=-1, keepdims=True) * v


def wkv_step(cur, prev, e_cur, e_prev, mu, mu_e, w0, a0, k_k, k_a, r_k, ln_w, ln_b, ww2h, wa2h, state):
    batch = cur.shape[0]
    hshape = (RWKV_HEADS, RWKV_HEAD_DIM)
    per_row = lambda shape: pl.BlockSpec((1,) + shape, lambda b: (b,) + (0,) * len(shape))
    const = lambda shape: pl.BlockSpec(shape, lambda b: (0,) * len(shape))
    return pl.pallas_call(
        _wkv_step_kernel,
        grid=(batch,),
        in_specs=[per_row((3,) + hshape), per_row((3,) + hshape), per_row((1, RWKV_EXTRA)), per_row((1, RWKV_EXTRA)),
                  const((3,) + hshape), const((1, RWKV_EXTRA))] + [const(hshape)] * 7
                 + [const((RWKV_HEADS, RWKV_EXTRA, RWKV_HEAD_DIM))] * 2 + [per_row(hshape + (RWKV_HEAD_DIM,))],
        out_specs=[per_row(hshape), per_row(hshape + (RWKV_HEAD_DIM,))],
        out_shape=[jax.ShapeDtypeStruct((batch,) + hshape, F32),
                   jax.ShapeDtypeStruct((batch,) + hshape + (RWKV_HEAD_DIM,), F32)],
        compiler_params=_params("parallel"),
        name="wkv_step",
    )(cur, prev, e_cur, e_prev, mu, mu_e, w0, a0, k_k, k_a, r_k, ln_w, ln_b, ww2h, wa2h, state)


def _relayout_in_proj(w_in, extra_cols, extra_width):
    return _in_proj_operands(jnp.transpose(w_in).astype(BF16), extra_cols, extra_width)


def _in_proj_operands(w_t, extra_cols, extra_width):
    gm_start = 3 * TOK_WIDTH + extra_cols
    return w_t, w_t[gm_start:], jnp.pad(w_t[3 * TOK_WIDTH:gm_start], ((0, extra_width - extra_cols), (0, 0)))


def _pad_lanes(v, width):
    return jnp.pad(v, (0, width - v.shape[0])).reshape(1, width)


def _memory_kv(mem, g_mem, w_mem):
    kv = norm_proj(mem, g_mem, w_mem, tm=mem.shape[0])
    return kv[:, :MEM_WIDTH], kv[:, MEM_WIDTH:]


def kernel(x_prompt, x_sample, cache_fox_k, cache_fox_v, cache_fox_logf, cache_mem_k, cache_mem_v, state_rwkv_wkv, state_rwkv_shift, page_table, mem_prompt, g_pre, g_post, g_mem, w_mem, w_o, fox_w_in, fox_b_f, rwkv_w_in, rwkv_mu, rwkv_w0, rwkv_w_w2, rwkv_a0, rwkv_w_a2, rwkv_k_k, rwkv_k_a, rwkv_r_k, rwkv_ln_w, rwkv_ln_b):
    bp, seq, _ = x_prompt.shape
    bs = x_sample.shape[0]
    xp = x_prompt.reshape(bp * seq, D_MODEL)
    xs = x_sample.reshape(bs, D_MODEL)
    mem = mem_prompt.reshape(bp * N_MEM, D_MODEL)
    mem_shape = (bp, N_MEM, MEM_HEADS, MEM_HEAD_DIM)

    w_fox = _relayout_in_proj(fox_w_in[0], FOX_HEADS, FOX_EXTRA)
    b_pad = _pad_lanes(fox_b_f[0], LANES)
    mk0, mv0 = _memory_kv(mem, g_mem[0], w_mem[0])

    gm_p, q_p, k_p, v_p, ext_p = norm_proj(xp, g_pre[0], *w_fox, tm=PROJ_ROWS, heads=(bp, seq))
    logf_p, ccol, crow = fox_gate(ext_p, b_pad, batch=bp, seq=seq, tb=GATE_ROWS)
    o_tok, w_rwkv_t, w_o = fox_flash(q_p, k_p, v_p, ccol, crow, tq=FLASH_ROWS,
                                     casts=(jnp.transpose(rwkv_w_in[0]), w_o.reshape(-1, D_MODEL)))
    w_o = w_o.reshape(-1, D_MODEL, D_MODEL)
    o_mem = mem_attn(gm_p.reshape(bp, seq, GM_WIDTH), mk0, mv0, tq=MEM_ATTN_ROWS, q_block=D_MODEL // MEM_WIDTH)
    xp = out_proj(o_tok, o_mem.reshape(bp * seq, MEM_WIDTH), gm_p, xp, w_o, g_post[0], layer=0, tm=OUT_PROJ_ROWS)

    main_s0, ext_s0 = norm_proj(xs, g_pre[0], *w_fox, tm=bs)
    q_s, k_s, v_s = (main_s0[:, off:off + TOK_WIDTH] for off in (TOKA_OFF, TOKB_OFF, TOKC_OFF))
    logf_pool_t = jnp.pad(jnp.transpose(cache_fox_logf[0], (0, 2, 1)), ((0, 0), (0, HEAD_PAD - FOX_HEADS), (0, 0)))
    k_pool, v_pool = (jnp.transpose(c, (0, 1, 3, 2, 4)) for c in (cache_fox_k, cache_fox_v))
    decode_args = (page_table, q_s, k_s, v_s, ext_s0, b_pad, k_pool, v_pool, logf_pool_t)

    head_shape = (FOX_HEADS, FOX_HEAD_DIM)
    fox_k_p = jnp.transpose(k_p, (0, 2, 1, 3))[None]
    fox_v_p = jnp.transpose(v_p, (0, 2, 1, 3))[None]
    fox_f_p = logf_p[:, :FOX_HEADS].reshape(1, bp, seq, FOX_HEADS)
    fox_k_s = k_s.reshape((1, bs, 1) + head_shape)
    fox_v_s = v_s.reshape((1, bs, 1) + head_shape)

    w_rwkv = _in_proj_operands(w_rwkv_t, 2 * LORA, RWKV_EXTRA)
    mk1, mv1 = _memory_kv(mem, g_mem[1], w_mem[1])
    mu_rkv = rwkv_mu[0][:3 * TOK_WIDTH].reshape(1, 3 * TOK_WIDTH)
    mu_e = _pad_lanes(rwkv_mu[0][3 * TOK_WIDTH:], RWKV_EXTRA)
    ww2 = jnp.pad(rwkv_w_w2[0], ((0, RWKV_EXTRA - LORA), (0, 0))).astype(BF16)
    wa2 = jnp.pad(rwkv_w_a2[0], ((LORA, RWKV_EXTRA - 2 * LORA), (0, 0))).astype(BF16)
    vecs = [p[0] for p in (rwkv_w0, rwkv_a0, rwkv_k_k, rwkv_k_a, rwkv_r_k, rwkv_ln_w, rwkv_ln_b)]

    gm_p, r_p, k_p, v_p, ext_p = norm_proj(xp, g_pre[1], *w_rwkv, tm=PROJ_ROWS, heads=(bp, seq))
    o_tok, st_pairs, o_tok_s, logf_s = wkv_prompt(r_p, k_p, v_p, ext_p, mu_rkv, mu_e,
                                                  *[v.reshape(1, TOK_WIDTH) for v in vecs], ww2, wa2, decode_args,
                                                  tb=WKV_ROWS, decode_layer=0)
    o_mem = mem_attn(gm_p.reshape(bp, seq, GM_WIDTH), mk1, mv1, tq=MEM_ATTN_ROWS, q_block=D_MODEL // MEM_WIDTH)
    xp = out_proj(o_tok, o_mem.reshape(bp * seq, MEM_WIDTH), gm_p, xp, w_o, g_post[1], layer=1, tm=OUT_PROJ_ROWS)

    fox_f_s = logf_s[:, :FOX_HEADS].reshape(1, bs, 1, FOX_HEADS)
    o_mem_s = mem_attn(main_s0.reshape(bs, 1, MAIN_WIDTH), cache_mem_k[0].reshape(bs * N_MEM, MEM_WIDTH),
                       cache_mem_v[0].reshape(bs * N_MEM, MEM_WIDTH), tq=1, q_block=QMEM_OFF // MEM_WIDTH)
    xs = out_proj(o_tok_s, o_mem_s.reshape(bs, MEM_WIDTH), main_s0, xs, w_o, g_post[0], layer=0, tm=bs)

    main_s, ext_s = norm_proj(xs, g_pre[1], *w_rwkv, tm=bs)
    hshape = (RWKV_HEADS, RWKV_HEAD_DIM)
    shift_prev = state_rwkv_shift[0][:, 0, :]
    cur = main_s[:, TOKA_OFF:QMEM_OFF].reshape((bs, 3) + hshape)
    prev = shift_prev[:, :3 * TOK_WIDTH].reshape((bs, 3) + hshape)
    e_prev = jnp.pad(shift_prev[:, 3 * TOK_WIDTH:], ((0, 0), (0, RWKV_EXTRA - 2 * LORA))).reshape(bs, 1, RWKV_EXTRA)
    per_head = lambda w: jnp.transpose(w.reshape((RWKV_EXTRA,) + hshape), (1, 0, 2))
    o_tok_s, wkv_s = wkv_step(cur, prev, ext_s.reshape(bs, 1, RWKV_EXTRA), e_prev, mu_rkv.reshape((3,) + hshape), mu_e,
                              *[v.reshape(hshape) for v in vecs], per_head(ww2), per_head(wa2), state_rwkv_wkv[0])
    o_mem_s = mem_attn(main_s.reshape(bs, 1, MAIN_WIDTH), cache_mem_k[1].reshape(bs * N_MEM, MEM_WIDTH),
                       cache_mem_v[1].reshape(bs * N_MEM, MEM_WIDTH), tq=1, q_block=QMEM_OFF // MEM_WIDTH)
    xs = out_proj(o_tok_s.reshape(bs, TOK_WIDTH), o_mem_s.reshape(bs, MEM_WIDTH), main_s, xs, w_o, g_post[1], layer=1, tm=bs)

    wkv_p = st_pairs.reshape((bp,) + hshape + (RWKV_HEAD_DIM,))
    last_p = jnp.concatenate([a[:, :, seq - 1, :].reshape(bp, TOK_WIDTH) for a in (r_p, k_p, v_p)], axis=-1)
    last_e = ext_p.reshape(bp, seq, RWKV_EXTRA)[:, seq - 1, :2 * LORA]
    shift_p = jnp.concatenate([last_p, last_e], axis=-1).reshape(1, bp, 1, 3 * TOK_WIDTH + 2 * LORA)
    shift_s = jnp.concatenate([main_s[:, TOKA_OFF:QMEM_OFF], ext_s[:, :2 * LORA]], axis=-1).reshape(1, bs, 1, 3 * TOK_WIDTH + 2 * LORA)

    return (xp.reshape(bp, seq, D_MODEL), xs.reshape(bs, 1, D_MODEL),
            jnp.stack([mk0.reshape(mem_shape), mk1.reshape(mem_shape)]),
            jnp.stack([mv0.reshape(mem_shape), mv1.reshape(mem_shape)]),
            fox_k_p, fox_v_p, fox_f_p, fox_k_s, fox_v_s, fox_f_s,
            wkv_p[None], shift_p, wkv_s[None], shift_s)
```
